```python
import jax, jax.numpy as jnp
from jax import lax
import numpy as np

D_MODEL = 2048
BATCH = 8
SEQ = 8192
DEPTH = 1

SB_HEADS = 16
SB_HEAD_DIM = 128
SB_WIDTH = SB_HEADS * SB_HEAD_DIM
SB_BLOCK = 128
GDN_HEADS = 16
GDN_KEY_DIM = 128
GDN_VAL_DIM = 128
GDN_QK_WIDTH = GDN_HEADS * GDN_KEY_DIM
GDN_V_WIDTH = GDN_HEADS * GDN_VAL_DIM
GDN_CONV = 4
GDN_CONV_DIM = 2 * GDN_QK_WIDTH + GDN_V_WIDTH
GDN_CHUNK = 64
IN_WIDTH = 4 * SB_WIDTH + 2 * GDN_QK_WIDTH + 2 * GDN_V_WIDTH + 2 * GDN_HEADS + 2 * D_MODEL
NORM_EPS = 1e-6
L2_EPS = 1e-6

kernel_name = "hybrid_stickbreaking_gated_deltanet_adaln"


def _in_split_points():
    sizes = [SB_WIDTH] * 4 + [GDN_QK_WIDTH, GDN_QK_WIDTH, GDN_V_WIDTH, GDN_V_WIDTH,
                              GDN_HEADS, GDN_HEADS, D_MODEL, D_MODEL]
    return [int(p) for p in np.cumsum(sizes)[:-1]]


def _rmsnorm(x, w):
    xf = x.astype(jnp.float32)
    y = xf * lax.rsqrt(jnp.mean(xf * xf, axis=-1, keepdims=True) + NORM_EPS)
    return (y * w.astype(jnp.float32)).astype(x.dtype)


def _l2norm(x):
    return x * lax.rsqrt(jnp.sum(x * x, axis=-1, keepdims=True) + L2_EPS)


def _heads(x, n_heads):
    b, t, _ = x.shape
    return x.reshape(b, t, n_heads, -1).transpose(0, 2, 1, 3)


def _merge_heads(x):
    b, h, t, d = x.shape
    return x.transpose(0, 2, 1, 3).reshape(b, t, h * d)


def _causal_short_conv(x, w):
    k, ch = w.shape
    return lax.conv_general_dilated(
        x, w[:, None, :].astype(x.dtype), window_strides=(1,), padding=[(k - 1, 0)],
        dimension_numbers=("NWC", "WIO", "NWC"), feature_group_count=ch)


def _stick_breaking_attention(q, k, v):
    t_len = q.shape[2]
    scale = SB_HEAD_DIM ** -0.5
    outs = []
    for blk in range(t_len // SB_BLOCK):
        t0 = blk * SB_BLOCK
        t1 = t0 + SB_BLOCK
        qb = q[:, :, t0:t1].astype(jnp.float32)
        kb = k[:, :, :t1].astype(jnp.float32)
        vb = v[:, :, :t1].astype(jnp.float32)
        z = jnp.einsum("bhtd,bhsd->bhts", qb, kb) * scale
        t_idx = t0 + jnp.arange(SB_BLOCK)[:, None]
        s_idx = jnp.arange(t1)[None, :]
        mask = s_idx < t_idx
        log_beta = jax.nn.log_sigmoid(z)
        log_not_beta = jnp.where(mask, jax.nn.log_sigmoid(-z), 0.0)
        later = lax.cumsum(log_not_beta, axis=3, reverse=True) - log_not_beta
        att = jnp.where(mask, jnp.exp(log_beta + later), 0.0)
        outs.append(jnp.einsum("bhts,bhsd->bhtd", att, vb))
    return jnp.concatenate(outs, axis=2)


def _gated_delta_rule(q, k, v, beta, g):
    b, h, t_len, dk = q.shape
    dv = v.shape[-1]
    c = GDN_CHUNK
    n = t_len // c
    q = q * dk ** -0.5

    def chunks(a):
        return a.reshape(b, h, n, c, *a.shape[3:])

    q, k, v, beta, g = chunks(q), chunks(k), chunks(v), chunks(beta), chunks(g)
    g = jnp.cumsum(g, axis=-1)
    k_beta = k * beta[..., None]
    v_beta = v * beta[..., None]
    tril = jnp.tril(jnp.ones((c, c), dtype=bool))
    strict = jnp.tril(jnp.ones((c, c), dtype=bool), -1)
    decay = jnp.exp(jnp.where(tril, g[..., :, None] - g[..., None, :], -jnp.inf))
    lower = jnp.where(strict, jnp.einsum("bhncd,bhnsd->bhncs", k_beta, k) * decay, 0.0)
    eye = jnp.eye(c, dtype=jnp.float32)
    t_mat = lax.linalg.triangular_solve(eye + lower, jnp.broadcast_to(eye, lower.shape),
                                        left_side=True, lower=True, unit_diagonal=True)
    u = jnp.einsum("bhncs,bhnsv->bhncv", t_mat, v_beta)
    w = jnp.einsum("bhncs,bhnsk->bhnck", t_mat, k_beta * jnp.exp(g)[..., None])
    intra = jnp.where(tril, jnp.einsum("bhncd,bhnsd->bhncs", q, k) * decay, 0.0)
    q_dec = q * jnp.exp(g)[..., None]
    k_dec = k * jnp.exp(g[..., -1:] - g)[..., None]
    g_last = jnp.exp(g[..., -1])

    def step(state, inp):
        q_i, k_i, u_i, w_i, intra_i, gl_i = inp
        v_new = u_i - jnp.einsum("bhck,bhkv->bhcv", w_i, state)
        o_i = (jnp.einsum("bhck,bhkv->bhcv", q_i, state)
               + jnp.einsum("bhcs,bhsv->bhcv", intra_i, v_new))
        state = state * gl_i[..., None, None] + jnp.einsum("bhck,bhcv->bhkv", k_i, v_new)
        return state, o_i

    xs = tuple(jnp.moveaxis(a, 2, 0) for a in (q_dec, k_dec, u, w, intra, g_last))
    s0 = jnp.zeros((b, h, dk, dv), dtype=jnp.float32)
    _, o = lax.scan(step, s0, xs)
    return jnp.moveaxis(o, 0, 2).reshape(b, h, t_len, dv)


def _fwd_setup_inputs(seed: int = 0) -> dict:
    key = jax.random.key(seed)
    ks = jax.random.split(key, 16)
    f32 = jnp.float32
    x = jax.random.normal(ks[0], (BATCH, SEQ, D_MODEL), f32)
    c = jax.random.normal(ks[1], (BATCH, D_MODEL), f32)
    w_ada = jax.random.normal(ks[2], (DEPTH, D_MODEL, 3 * D_MODEL), f32) * D_MODEL ** -0.5
    b_ada = jax.random.normal(ks[3], (DEPTH, 3 * D_MODEL), f32) * 0.02
    norm_w = 1.0 + 0.02 * jax.random.normal(ks[4], (DEPTH, D_MODEL), f32)
    w_in = jax.random.normal(ks[5], (DEPTH, D_MODEL, IN_WIDTH), f32) * D_MODEL ** -0.5
    gdn_conv_w = jax.random.normal(ks[6], (DEPTH, GDN_CONV, GDN_CONV_DIM), f32) * GDN_CONV ** -0.5
    gdn_a_log = jnp.log(jax.random.uniform(ks[7], (DEPTH, GDN_HEADS), f32, 1.0, 16.0))
    dt = jnp.exp(jax.random.uniform(ks[8], (DEPTH, GDN_HEADS), f32, jnp.log(1e-3), jnp.log(1e-1)))
    gdn_dt_bias = dt + jnp.log(-jnp.expm1(-dt))
    gdn_norm_w = 1.0 + 0.02 * jax.random.normal(ks[9], (DEPTH, GDN_VAL_DIM), f32)
    w_proj_sb = jax.random.normal(ks[10], (DEPTH, SB_WIDTH, D_MODEL), f32) * SB_WIDTH ** -0.5
    w_proj_gdn = jax.random.normal(ks[11], (DEPTH, GDN_V_WIDTH, D_MODEL), f32) * GDN_V_WIDTH ** -0.5
    w_out = jax.random.normal(ks[12], (DEPTH, D_MODEL, D_MODEL), f32) * D_MODEL ** -0.5
    final_norm_w = 1.0 + 0.02 * jax.random.normal(ks[13], (D_MODEL,), f32)
    return {"x": x, "c": c, "w_ada": w_ada, "b_ada": b_ada, "norm_w": norm_w, "w_in": w_in,
            "gdn_conv_w": gdn_conv_w, "gdn_a_log": gdn_a_log, "gdn_dt_bias": gdn_dt_bias,
            "gdn_norm_w": gdn_norm_w, "w_proj_sb": w_proj_sb, "w_proj_gdn": w_proj_gdn,
            "w_out": w_out, "final_norm_w": final_norm_w}


def _fwd_reference(x, c, w_ada, b_ada, norm_w, w_in, gdn_conv_w, gdn_a_log, gdn_dt_bias,
              gdn_norm_w, w_proj_sb, w_proj_gdn, w_out, final_norm_w):
    f32 = jnp.float32
    split_points = _in_split_points()
    for l in range(DEPTH):
        mod = jax.nn.silu(c) @ w_ada[l] + b_ada[l]
        shift, scale, gate = jnp.split(mod, 3, axis=-1)
        h = _rmsnorm(x, norm_w[l]) * (1.0 + scale[:, None, :]) + shift[:, None, :]
        proj = h @ w_in[l]
        (sb_q, sb_k, sb_v, sb_z, gq, gk, gv, gz, gb, ga, m_sb, m_gdn) = jnp.split(
            proj, split_points, axis=-1)

        o_sb = _stick_breaking_attention(_heads(sb_q, SB_HEADS), _heads(sb_k, SB_HEADS),
                                         _heads(sb_v, SB_HEADS))
        o_sb = _merge_heads(o_sb).astype(x.dtype) * jax.nn.silu(sb_z)

        qkv = jax.nn.silu(_causal_short_conv(jnp.concatenate([gq, gk, gv], axis=-1), gdn_conv_w[l]))
        cq, ck, cv = jnp.split(qkv, [GDN_QK_WIDTH, 2 * GDN_QK_WIDTH], axis=-1)
        qh = _l2norm(_heads(cq, GDN_HEADS).astype(f32))
        kh = _l2norm(_heads(ck, GDN_HEADS).astype(f32))
        vh = _heads(cv, GDN_HEADS).astype(f32)
        beta = jax.nn.sigmoid(gb.astype(f32)).transpose(0, 2, 1)
        g = (-jnp.exp(gdn_a_log[l].astype(f32))
             * jax.nn.softplus(ga.astype(f32) + gdn_dt_bias[l].astype(f32))).transpose(0, 2, 1)
        o_gdn = _gated_delta_rule(qh, kh, vh, beta, g)
        o_gdn = _rmsnorm(o_gdn, gdn_norm_w[l])
        o_gdn = _merge_heads(o_gdn).astype(x.dtype) * jax.nn.silu(gz)

        y = (jax.nn.sigmoid(m_sb) * (o_sb @ w_proj_sb[l])
             + jax.nn.sigmoid(m_gdn) * (o_gdn @ w_proj_gdn[l]))
        x = x + gate[:, None, :] * (y @ w_out[l])
    return _rmsnorm(x, final_norm_w)


import jax as _jax
import jax.numpy as _jnp

TWIN_FORMAT = 'train_step'
FWD_PARAMS = ['x', 'c', 'w_ada', 'b_ada', 'norm_w', 'w_in', 'gdn_conv_w', 'gdn_a_log', 'gdn_dt_bias', 'gdn_norm_w', 'w_proj_sb', 'w_proj_gdn', 'w_out', 'final_norm_w']
TWIN_WEIGHTS = ['w_ada', 'b_ada', 'norm_w', 'w_in', 'gdn_conv_w', 'gdn_a_log', 'gdn_dt_bias', 'gdn_norm_w', 'w_proj_sb', 'w_proj_gdn', 'w_out', 'final_norm_w']
TWIN_DIFF_INPUT = 'x'
TWIN_INPUTS = ['x', 'c', 'w_ada', 'b_ada', 'norm_w', 'w_in', 'gdn_conv_w', 'gdn_a_log', 'gdn_dt_bias', 'gdn_norm_w', 'w_proj_sb', 'w_proj_gdn', 'w_out', 'final_norm_w', 'loss_target', 'm_w_ada', 'm_b_ada', 'm_norm_w', 'm_w_in', 'm_gdn_conv_w', 'm_gdn_a_log', 'm_gdn_dt_bias', 'm_gdn_norm_w', 'm_w_proj_sb', 'm_w_proj_gdn', 'm_w_out', 'm_final_norm_w', 'v_w_ada', 'v_b_ada', 'v_norm_w', 'v_w_in', 'v_gdn_conv_w', 'v_gdn_a_log', 'v_gdn_dt_bias', 'v_gdn_norm_w', 'v_w_proj_sb', 'v_w_proj_gdn', 'v_w_out', 'v_final_norm_w']
TWIN_OUTPUTS = ['loss', 'grad_x', 'grad_w_ada', 'grad_b_ada', 'grad_norm_w', 'grad_w_in', 'grad_gdn_conv_w', 'grad_gdn_a_log', 'grad_gdn_dt_bias', 'grad_gdn_norm_w', 'grad_w_proj_sb', 'grad_w_proj_gdn', 'grad_w_out', 'grad_final_norm_w', 'delta_w_ada', 'delta_b_ada', 'delta_norm_w', 'delta_w_in', 'delta_gdn_conv_w', 'delta_gdn_a_log', 'delta_gdn_dt_bias', 'delta_gdn_norm_w', 'delta_w_proj_sb', 'delta_w_proj_gdn', 'delta_w_out', 'delta_final_norm_w', 'new_m_w_ada', 'new_m_b_ada', 'new_m_norm_w', 'new_m_w_in', 'new_m_gdn_conv_w', 'new_m_gdn_a_log', 'new_m_gdn_dt_bias', 'new_m_gdn_norm_w', 'new_m_w_proj_sb', 'new_m_w_proj_gdn', 'new_m_w_out', 'new_m_final_norm_w', 'new_v_w_ada', 'new_v_b_ada', 'new_v_norm_w', 'new_v_w_in', 'new_v_gdn_conv_w', 'new_v_gdn_a_log', 'new_v_gdn_dt_bias', 'new_v_gdn_norm_w', 'new_v_w_proj_sb', 'new_v_w_proj_gdn', 'new_v_w_out', 'new_v_final_norm_w']
TWIN_LEAF_KINDS = {'loss': 'loss', 'grad_x': 'grad_x', 'grad_w_ada': 'grad_w', 'grad_b_ada': 'grad_w', 'grad_norm_w': 'grad_w', 'grad_w_in': 'grad_w', 'grad_gdn_conv_w': 'grad_w', 'grad_gdn_a_log': 'grad_w', 'grad_gdn_dt_bias': 'grad_w', 'grad_gdn_norm_w': 'grad_w', 'grad_w_proj_sb': 'grad_w', 'grad_w_proj_gdn': 'grad_w', 'grad_w_out': 'grad_w', 'grad_final_norm_w': 'grad_w', 'delta_w_ada': 'delta_w', 'delta_b_ada': 'delta_w', 'delta_norm_w': 'delta_w', 'delta_w_in': 'delta_w', 'delta_gdn_conv_w': 'delta_w', 'delta_gdn_a_log': 'delta_w', 'delta_gdn_dt_bias': 'delta_w', 'delta_gdn_norm_w': 'delta_w', 'delta_w_proj_sb': 'delta_w', 'delta_w_proj_gdn': 'delta_w', 'delta_w_out': 'delta_w', 'delta_final_norm_w': 'delta_w', 'new_m_w_ada': 'new_m', 'new_m_b_ada': 'new_m', 'new_m_norm_w': 'new_m', 'new_m_w_in': 'new_m', 'new_m_gdn_conv_w': 'new_m', 'new_m_gdn_a_log': 'new_m', 'new_m_gdn_dt_bias': 'new_m', 'new_m_gdn_norm_w': 'new_m', 'new_m_w_proj_sb': 'new_m', 'new_m_w_proj_gdn': 'new_m', 'new_m_w_out': 'new_m', 'new_m_final_norm_w': 'new_m', 'new_v_w_ada': 'new_v', 'new_v_b_ada': 'new_v', 'new_v_norm_w': 'new_v', 'new_v_w_in': 'new_v', 'new_v_gdn_conv_w': 'new_v', 'new_v_gdn_a_log': 'new_v', 'new_v_gdn_dt_bias': 'new_v', 'new_v_gdn_norm_w': 'new_v', 'new_v_w_proj_sb': 'new_v', 'new_v_w_proj_gdn': 'new_v', 'new_v_w_out': 'new_v', 'new_v_final_norm_w': 'new_v'}


def _forward(args):
    return _fwd_reference(*[args[k] for k in FWD_PARAMS])


def _output_shape():
    def fwd():
        inp = _fwd_setup_inputs(0)
        return _fwd_reference(*[inp[k] for k in FWD_PARAMS])
    out = _jax.eval_shape(fwd)
    return out.shape, out.dtype

N_MICROBATCH = 1
ADAM_LR = 0.001
ADAM_B1 = 0.9
ADAM_B2 = 0.999
ADAM_EPS = 1e-08
ADAM_WD = 0.01
ADAM_STEP = 10
PER_EXAMPLE_BATCH_AXIS = {'x': 0, 'c': 0, 'loss_target': 0}
SHARED_INPUTS = []
_WEIGHT_DTYPES = {'w_ada': _jnp.float32, 'b_ada': _jnp.float32, 'norm_w': _jnp.float32, 'w_in': _jnp.float32, 'gdn_conv_w': _jnp.float32, 'gdn_a_log': _jnp.float32, 'gdn_dt_bias': _jnp.float32, 'gdn_norm_w': _jnp.float32, 'w_proj_sb': _jnp.float32, 'w_proj_gdn': _jnp.float32, 'w_out': _jnp.float32, 'final_norm_w': _jnp.float32}
MOMENT_SCALE = {'w_ada': 4.441116e-02, 'b_ada': 7.829924e-02, 'norm_w': 6.562447e-02, 'w_in': 2.433349e-02, 'gdn_conv_w': 2.267828e-02, 'gdn_a_log': 1.115781e-01, 'gdn_dt_bias': 1.088783e-01, 'gdn_norm_w': 1.559079e-01, 'w_proj_sb': 3.274963e-02, 'w_proj_gdn': 3.169193e-02, 'w_out': 4.593540e-02, 'final_norm_w': 3.207540e+01}


def _to_microbatches(a, axis):
    t = _jnp.moveaxis(a, axis, 0)
    t = t.reshape((N_MICROBATCH, t.shape[0] // N_MICROBATCH) + t.shape[1:])
    return _jnp.moveaxis(t, 1, axis + 1)


def setup_inputs(seed: int = 0) -> dict:
    inp = _fwd_setup_inputs(seed)
    key = _jax.random.fold_in(_jax.random.key(seed), 7919)
    shape, _ = _output_shape()
    out = dict(inp)
    out["loss_target"] = _jax.random.normal(_jax.random.fold_in(key, 0), shape, _jnp.float32)
    for i, name in enumerate(TWIN_WEIGHTS):
        w = inp[name].astype(_jnp.float32)
        if MOMENT_SCALE is None:
            s = _jnp.sqrt(_jnp.mean(_jnp.square(w)) + 1e-30)
        else:
            s = MOMENT_SCALE[name]
        km, kv = _jax.random.split(_jax.random.fold_in(key, i + 1))
        out[name] = w
        out["m_" + name] = s * _jax.random.normal(km, w.shape, _jnp.float32)
        out["v_" + name] = (s * s) * _jax.random.uniform(kv, w.shape, _jnp.float32, 0.5, 1.5)
    if N_MICROBATCH > 1:
        for name, axis in PER_EXAMPLE_BATCH_AXIS.items():
            out[name] = _to_microbatches(out[name], axis)
    return {'x': out['x'], 'c': out['c'], 'w_ada': out['w_ada'], 'b_ada': out['b_ada'], 'norm_w': out['norm_w'], 'w_in': out['w_in'], 'gdn_conv_w': out['gdn_conv_w'], 'gdn_a_log': out['gdn_a_log'], 'gdn_dt_bias': out['gdn_dt_bias'], 'gdn_norm_w': out['gdn_norm_w'], 'w_proj_sb': out['w_proj_sb'], 'w_proj_gdn': out['w_proj_gdn'], 'w_out': out['w_out'], 'final_norm_w': out['final_norm_w'], 'loss_target': out['loss_target'], 'm_w_ada': out['m_w_ada'], 'm_b_ada': out['m_b_ada'], 'm_norm_w': out['m_norm_w'], 'm_w_in': out['m_w_in'], 'm_gdn_conv_w': out['m_gdn_conv_w'], 'm_gdn_a_log': out['m_gdn_a_log'], 'm_gdn_dt_bias': out['m_gdn_dt_bias'], 'm_gdn_norm_w': out['m_gdn_norm_w'], 'm_w_proj_sb': out['m_w_proj_sb'], 'm_w_proj_gdn': out['m_w_proj_gdn'], 'm_w_out': out['m_w_out'], 'm_final_norm_w': out['m_final_norm_w'], 'v_w_ada': out['v_w_ada'], 'v_b_ada': out['v_b_ada'], 'v_norm_w': out['v_norm_w'], 'v_w_in': out['v_w_in'], 'v_gdn_conv_w': out['v_gdn_conv_w'], 'v_gdn_a_log': out['v_gdn_a_log'], 'v_gdn_dt_bias': out['v_gdn_dt_bias'], 'v_gdn_norm_w': out['v_gdn_norm_w'], 'v_w_proj_sb': out['v_w_proj_sb'], 'v_w_proj_gdn': out['v_w_proj_gdn'], 'v_w_out': out['v_w_out'], 'v_final_norm_w': out['v_final_norm_w']}


def _loss(weights, diff, rest, loss_target):
    with _jax.named_scope("forward"):
        args = {**rest, TWIN_DIFF_INPUT: diff, **{k: w.astype(_WEIGHT_DTYPES[k]) for k, w in weights.items()}}
        y = _forward(args)
    with _jax.named_scope("loss_head"):
        err = _jnp.square(y.astype(_jnp.float32) - loss_target)
        return 0.5 * _jnp.sum(_jnp.mean(err, axis=-1)) if err.ndim else 0.5 * err


def _adamw(w, g, m, v):
    m = ADAM_B1 * m + (1.0 - ADAM_B1) * g
    v = ADAM_B2 * v + (1.0 - ADAM_B2) * _jnp.square(g)
    m_hat = m / (1.0 - ADAM_B1 ** ADAM_STEP)
    v_hat = v / (1.0 - ADAM_B2 ** ADAM_STEP)
    delta = -ADAM_LR * (m_hat / (_jnp.sqrt(v_hat) + ADAM_EPS) + ADAM_WD * w)
    return delta, m, v


def reference(x, c, w_ada, b_ada, norm_w, w_in, gdn_conv_w, gdn_a_log, gdn_dt_bias, gdn_norm_w, w_proj_sb, w_proj_gdn, w_out, final_norm_w, loss_target, m_w_ada, m_b_ada, m_norm_w, m_w_in, m_gdn_conv_w, m_gdn_a_log, m_gdn_dt_bias, m_gdn_norm_w, m_w_proj_sb, m_w_proj_gdn, m_w_out, m_final_norm_w, v_w_ada, v_b_ada, v_norm_w, v_w_in, v_gdn_conv_w, v_gdn_a_log, v_gdn_dt_bias, v_gdn_norm_w, v_w_proj_sb, v_w_proj_gdn, v_w_out, v_final_norm_w):
    given = dict(x=x, c=c, w_ada=w_ada, b_ada=b_ada, norm_w=norm_w, w_in=w_in, gdn_conv_w=gdn_conv_w, gdn_a_log=gdn_a_log, gdn_dt_bias=gdn_dt_bias, gdn_norm_w=gdn_norm_w, w_proj_sb=w_proj_sb, w_proj_gdn=w_proj_gdn, w_out=w_out, final_norm_w=final_norm_w, loss_target=loss_target, m_w_ada=m_w_ada, m_b_ada=m_b_ada, m_norm_w=m_norm_w, m_w_in=m_w_in, m_gdn_conv_w=m_gdn_conv_w, m_gdn_a_log=m_gdn_a_log, m_gdn_dt_bias=m_gdn_dt_bias, m_gdn_norm_w=m_gdn_norm_w, m_w_proj_sb=m_w_proj_sb, m_w_proj_gdn=m_w_proj_gdn, m_w_out=m_w_out, m_final_norm_w=m_final_norm_w, v_w_ada=v_w_ada, v_b_ada=v_b_ada, v_norm_w=v_norm_w, v_w_in=v_w_in, v_gdn_conv_w=v_gdn_conv_w, v_gdn_a_log=v_gdn_a_log, v_gdn_dt_bias=v_gdn_dt_bias, v_gdn_norm_w=v_gdn_norm_w, v_w_proj_sb=v_w_proj_sb, v_w_proj_gdn=v_w_proj_gdn, v_w_out=v_w_out, v_final_norm_w=v_final_norm_w)
    weights = {n: given[n] for n in TWIN_WEIGHTS}
    shared = {n: given[n] for n in SHARED_INPUTS}
    per_example = {n: given[n] for n in ['x', 'c']}
    grad_fn = _jax.value_and_grad(_loss, argnums=(0, 1))

    def one_microbatch(ex, loss_target):
        ex = dict(ex)
        diff = ex.pop(TWIN_DIFF_INPUT)
        return grad_fn(weights, diff, {**shared, **ex}, loss_target)

    if N_MICROBATCH == 1:
        loss, (grad_w, grad_x) = one_microbatch(per_example, given["loss_target"])
    else:
        def body(carry, xs):
            loss_sum, grad_sum = carry
            l_k, (gw_k, gx_k) = one_microbatch(xs[0], xs[1])
            with _jax.named_scope("update"):
                return (loss_sum + l_k, _jax.tree.map(_jnp.add, grad_sum, gw_k)), gx_k

        init = (_jnp.zeros((), _jnp.float32), _jax.tree.map(_jnp.zeros_like, weights))
        (loss, grad_w), grad_x = _jax.lax.scan(body, init, (per_example, given["loss_target"]))
    with _jax.named_scope("update"):
        delta_w, new_m, new_v = {}, {}, {}
        for n in TWIN_WEIGHTS:
            delta_w[n], new_m[n], new_v[n] = _adamw(weights[n], grad_w[n], given["m_" + n], given["v_" + n])
    return (loss, grad_x, *[grad_w[n] for n in TWIN_WEIGHTS], *[delta_w[n] for n in TWIN_WEIGHTS],
            *[new_m[n] for n in TWIN_WEIGHTS], *[new_v[n] for n in TWIN_WEIGHTS])
```

```python
import jax
import jax.numpy as jnp
from jax import lax
from jax.experimental import pallas as pl
from jax.experimental.pallas import tpu as pltpu

F32 = jnp.float32
BF16 = jnp.bfloat16
N_DEV = 8
HEAD_DIM = 128
LANES = 128
GDN_CHUNK = 64
GDN_CONV = 4
NORM_EPS = 1e-6
L2_EPS = 1e-6
ADAM_LR = 0.001
ADAM_B1 = 0.9
ADAM_B2 = 0.999
ADAM_EPS = 1e-08
ADAM_WD = 0.01
ADAM_STEP = 10
VMEM_LIMIT = 56 * 1024 * 1024
MESH = pl.DeviceIdType.MESH
NT = (((1,), (1,)), ((), ()))
TN = (((0,), (0,)), ((), ()))


def _pick(n, pref):
    t = min(pref, n)
    while n % t:
        t //= 2
    return t


def _sigmoid(x):
    return 1.0 / (1.0 + jnp.exp(-x))


def _bdot(a, b, dims=None):
    a = a.astype(BF16)
    b = b.astype(BF16)
    if dims is None:
        return jnp.dot(a, b, preferred_element_type=F32)
    return lax.dot_general(a, b, dims, preferred_element_type=F32)


def _split2(x):
    hi = x.astype(BF16)
    lo = (x - hi.astype(F32)).astype(BF16)
    return hi, lo


def _split3(x):
    p1 = x.astype(BF16)
    r = x - p1.astype(F32)
    p2 = r.astype(BF16)
    p3 = (r - p2.astype(F32)).astype(BF16)
    return p1, p2, p3


def _dot2(x, c, dims=None):
    hi, lo = _split2(x)
    return _bdot(hi, c, dims) + _bdot(lo, c, dims)


def _cdot3(c, x):
    p1, p2, p3 = _split3(x)
    return _bdot(c, p1) + _bdot(c, p2) + _bdot(c, p3)


def _dot6(a, b, dims=None):
    a1, a2, a3 = _split3(a)
    b1, b2, b3 = _split3(b)
    small = (_bdot(a1, b3, dims) + _bdot(a3, b1, dims)) + _bdot(a2, b2, dims)
    return _bdot(a1, b1, dims) + ((_bdot(a1, b2, dims) + _bdot(a2, b1, dims)) + small)


def _full(a):
    nd = a.ndim
    return (a, a.shape, lambda *idx: (0,) * nd)


def _rowcall(name, body, grid, ins, outs):
    n_in = len(ins)
    modes = [o[4] for o in outs]
    n_ax = len(grid)

    def kern(*refs):
        idx = tuple(pl.program_id(a) for a in range(n_ax))
        vals = body(idx, *[r[...] for r in refs[:n_in]])
        first_all = idx[0] == 0
        for a in range(1, n_ax):
            first_all = jnp.logical_and(first_all, idx[a] == 0)
        first_inner = idx[-1] == 0
        for r, v, mode in zip(refs[n_in:], vals, modes):
            v = v.astype(r.dtype)
            if mode == "set":
                r[...] = v
            else:
                first = first_all if mode == "acc_all" else first_inner

                @pl.when(first)
                def _(r=r, v=v):
                    r[...] = v

                @pl.when(jnp.logical_not(first))
                def _(r=r, v=v):
                    r[...] += v

    return pl.pallas_call(
        kern, name=name, grid=grid,
        in_specs=[pl.BlockSpec(b, m) for (_, b, m) in ins],
        out_specs=[pl.BlockSpec(o[2], o[3]) for o in outs],
        out_shape=[jax.ShapeDtypeStruct(o[0], o[1]) for o in outs],
        compiler_params=pltpu.CompilerParams(
            dimension_semantics=("arbitrary",) * n_ax, vmem_limit_bytes=VMEM_LIMIT),
    )(*[a for (a, _, _) in ins])


def _matmul(name, a, b, out_dtype, n_cols=None, col0=0, tm=1024, tn=1024, tk=1024):
    m, k = a.shape
    n = b.shape[1] if n_cols is None else n_cols
    tm = _pick(m, tm)
    tn = _pick(n, tn)
    while col0 % tn:
        tn //= 2
    tk = _pick(k, tk)
    nk = k // tk
    cb = col0 // tn

    def kern(a_ref, b_ref, o_ref, acc_ref):
        kk = pl.program_id(2)
        part = jnp.dot(a_ref[...], b_ref[...], preferred_element_type=F32)
        if nk == 1:
            o_ref[...] = part.astype(o_ref.dtype)
        else:
            @pl.when(kk == 0)
            def _():
                acc_ref[...] = part

            @pl.when(kk > 0)
            def _():
                acc_ref[...] += part

            @pl.when(kk == nk - 1)
            def _():
                o_ref[...] = acc_ref[...].astype(o_ref.dtype)

    return pl.pallas_call(
        kern, name=name, grid=(m // tm, n // tn, nk),
        in_specs=[pl.BlockSpec((tm, tk), lambda i, j, kk: (i, kk)),
                  pl.BlockSpec((tk, tn), lambda i, j, kk: (kk, j + cb))],
        out_specs=pl.BlockSpec((tm, tn), lambda i, j, kk: (i, j)),
        out_shape=jax.ShapeDtypeStruct((m, n), out_dtype),
        scratch_shapes=[pltpu.VMEM((tm, tn), F32)],
        compiler_params=pltpu.CompilerParams(
            dimension_semantics=("arbitrary", "arbitrary", "arbitrary"),
            vmem_limit_bytes=VMEM_LIMIT),
    )(a, b)


def _exchange(name, x, gather):
    piece = x.shape if gather else x.shape[1:]

    def body(x_ref, out_ref, send_sems, recv_sems, local_sem):
        xi, yi, ci = lax.axis_index("x"), lax.axis_index("y"), lax.axis_index("c")
        me = 4 * xi + 2 * yi + ci
        sends = []
        for k in range(1, N_DEV):
            kx, ky, kc = (k >> 2) & 1, (k >> 1) & 1, k & 1
            peer = (xi ^ kx, yi ^ ky, ci ^ kc)
            pid = me ^ k
            src = x_ref if gather else x_ref.at[pid]
            cp = pltpu.make_async_remote_copy(
                src_ref=src, dst_ref=out_ref.at[me], send_sem=send_sems.at[k - 1],
                recv_sem=recv_sems.at[k - 1], device_id=peer, device_id_type=MESH)
            cp.start()
            sends.append(cp)
        mine = pltpu.make_async_copy(x_ref if gather else x_ref.at[me], out_ref.at[me], local_sem)
        mine.start()
        for k in range(1, N_DEV):
            pid = me ^ k
            pltpu.make_async_remote_copy(
                src_ref=x_ref if gather else x_ref.at[pid], dst_ref=out_ref.at[pid],
                send_sem=send_sems.at[k - 1], recv_sem=recv_sems.at[k - 1],
                device_id=(xi, yi, ci), device_id_type=MESH).wait_recv()
        for cp in sends:
            cp.wait_send()
        mine.wait()

    return pl.pallas_call(
        body, name=name,
        out_shape=jax.ShapeDtypeStruct((N_DEV,) + tuple(piece), x.dtype),
        in_specs=[pl.BlockSpec(memory_space=pl.ANY)],
        out_specs=pl.BlockSpec(memory_space=pl.ANY),
        scratch_shapes=[pltpu.SemaphoreType.DMA((N_DEV - 1,)),
                        pltpu.SemaphoreType.DMA((N_DEV - 1,)),
                        pltpu.SemaphoreType.DMA],
    )(x)


def _adamw(name, parts, w, m, v):
    p, r, c = parts.shape
    tr = r if r <= 64 else _pick(r, 64)

    def body(idx, parts_v, w_v, m_v, v_v):
        g = parts_v[0]
        for s in range(1, p):
            g = g + parts_v[s]
        m2 = ADAM_B1 * m_v + (1.0 - ADAM_B1) * g
        v2 = ADAM_B2 * v_v + (1.0 - ADAM_B2) * (g * g)
        m_hat = m2 / (1.0 - ADAM_B1 ** ADAM_STEP)
        v_hat = v2 / (1.0 - ADAM_B2 ** ADAM_STEP)
        delta = -ADAM_LR * (m_hat / (jnp.sqrt(v_hat) + ADAM_EPS) + ADAM_WD * w_v)
        return g, delta, m2, v2

    rb = ((tr, c), lambda i: (i, 0))
    return _rowcall(
        name, body, (r // tr,),
        [(parts, (p, tr, c), lambda i: (0, i, 0)), (w,) + rb, (m,) + rb, (v,) + rb],
        [((r, c), F32) + rb + ("set",)] * 4)


TR_WIDE = 256


def _norm_mod_fwd(x, shift, scale, norm_w):
    t, d = x.shape
    tr = _pick(t, TR_WIDE)

    def body(idx, x_v, sh, sc, nw):
        rstd = lax.rsqrt(jnp.mean(x_v * x_v, axis=1, keepdims=True) + NORM_EPS)
        return ((x_v * rstd) * nw * (1.0 + sc) + sh,)

    rb = ((tr, d), lambda i: (i, 0))
    return _rowcall("norm_mod_fwd", body, (t // tr,),
                    [(x,) + rb, _full(shift), _full(scale), _full(norm_w)],
                    [((t, d), BF16) + rb + ("set",)])[0]


def _norm_mod_bwd(dh_a, dh_b, x, dx2, scale, norm_w):
    t, d = x.shape
    tr = _pick(t, TR_WIDE)

    def body(idx, dha, dhb, x_v, dx2_v, sc, nw):
        dh = dha + dhb
        rstd = lax.rsqrt(jnp.mean(x_v * x_v, axis=1, keepdims=True) + NORM_EPS)
        xn = x_v * rstd
        m1 = 1.0 + sc
        dxn = dh * nw * m1
        dx = rstd * (dxn - xn * jnp.mean(dxn * xn, axis=1, keepdims=True))
        dhx = dh * xn
        return (dx2_v + dx,
                jnp.sum(dh, axis=0, keepdims=True),
                jnp.sum(dhx * nw, axis=0, keepdims=True),
                jnp.sum(dhx * m1, axis=0, keepdims=True))

    rb = ((tr, d), lambda i: (i, 0))
    vec = ((1, d), F32, (1, d), lambda i: (0, 0), "acc_all")
    return _rowcall("norm_mod_bwd", body, (t // tr,),
                    [(dh_a,) + rb, (dh_b,) + rb, (x,) + rb, (dx2,) + rb, _full(scale), _full(norm_w)],
                    [((t, d), F32) + rb + ("set",), vec, vec, vec])


SB_TILE = 256


def _log_sig_pair(z):
    l = jnp.log(1.0 + jnp.exp(-jnp.abs(z)))
    return jnp.minimum(z, 0.0) - l, jnp.minimum(-z, 0.0) - l


def _sb_fwd(qkv, n_heads):
    t = qkv.shape[0]
    tq = _pick(t, SB_TILE)
    nq = t // tq
    scale = HEAD_DIM ** -0.5

    def kern(q_ref, k_ref, v_ref, o_ref, c_ref):
        i_blk = pl.program_id(1)
        row = lax.broadcasted_iota(jnp.int32, (tq, tq), 0)
        col = lax.broadcasted_iota(jnp.int32, (tq, tq), 1)
        causal = col < row
        after = (row > col).astype(BF16)
        qb = q_ref[...]

        def tile(j_blk, c, acc, diag):
            r0 = pl.multiple_of(j_blk * tq, tq)
            kb = k_ref[pl.ds(r0, tq), :]
            vb = v_ref[pl.ds(r0, tq), :]
            z = lax.dot_general(qb, kb, NT, preferred_element_type=F32) * scale
            lb, lnb = _log_sig_pair(z)
            if diag:
                lnb = jnp.where(causal, lnb, 0.0)
            later = _dot2(lnb, after) + c
            a = jnp.exp(lb + later)
            if diag:
                a = jnp.where(causal, a, 0.0)
            acc = acc + jnp.dot(a.astype(BF16), vb, preferred_element_type=F32)
            c = c + jnp.sum(lnb, axis=1, keepdims=True)
            return c, acc

        c, acc = tile(i_blk, jnp.zeros((tq, 1), F32), jnp.zeros((tq, HEAD_DIM), F32), True)
        c, acc = lax.fori_loop(0, i_blk, lambda jj, ca: tile(i_blk - 1 - jj, ca[0], ca[1], False),
                               (c, acc))
        o_ref[...] = acc
        c_ref[...] = jnp.broadcast_to(c, (tq, HEAD_DIM))

    h = n_heads
    return pl.pallas_call(
        kern, name="sb_fwd", grid=(h, nq),
        in_specs=[pl.BlockSpec((tq, HEAD_DIM), lambda hh, i: (i, hh)),
                  pl.BlockSpec((t, HEAD_DIM), lambda hh, i: (0, h + hh)),
                  pl.BlockSpec((t, HEAD_DIM), lambda hh, i: (0, 2 * h + hh))],
        out_specs=[pl.BlockSpec((tq, HEAD_DIM), lambda hh, i: (i, hh)),
                   pl.BlockSpec((tq, HEAD_DIM), lambda hh, i: (i, hh))],
        out_shape=[jax.ShapeDtypeStruct((t, h * HEAD_DIM), F32),
                   jax.ShapeDtypeStruct((t, h * HEAD_DIM), F32)],
        compiler_params=pltpu.CompilerParams(
            dimension_semantics=("arbitrary", "arbitrary"), vmem_limit_bytes=VMEM_LIMIT),
    )(qkv, qkv, qkv)


def _sb_bwd(qkv, do, ctot, n_heads):
    t = qkv.shape[0]
    tq = _pick(t, SB_TILE)
    nq = t // tq
    scale = HEAD_DIM ** -0.5

    def kern(q_ref, do_ref, c_ref, k_ref, v_ref, dq_ref, dk_ref, dv_ref, dk_acc, dv_acc):
        i_blk = pl.program_id(1)
        row = lax.broadcasted_iota(jnp.int32, (tq, tq), 0)
        col = lax.broadcasted_iota(jnp.int32, (tq, tq), 1)
        causal = col < row
        upto = (row <= col).astype(BF16)
        before = (row < col).astype(BF16)
        qb = q_ref[...]
        dob = do_ref[...]
        ctot_v = c_ref[...][:, 0:1]

        @pl.when(i_blk == 0)
        def _():
            dk_acc[...] = jnp.zeros_like(dk_acc)
            dv_acc[...] = jnp.zeros_like(dv_acc)

        def tile(j_blk, cl, pe, dq, diag):
            r0 = pl.multiple_of(j_blk * tq, tq)
            kb = k_ref[pl.ds(r0, tq), :]
            vb = v_ref[pl.ds(r0, tq), :]
            z = lax.dot_general(qb, kb, NT, preferred_element_type=F32) * scale
            lb, lnb = _log_sig_pair(z)
            if diag:
                lnb = jnp.where(causal, lnb, 0.0)
            later = (ctot_v - cl) - _dot2(lnb, upto)
            a = jnp.exp(lb + later)
            if diag:
                a = jnp.where(causal, a, 0.0)
            da = lax.dot_general(dob, vb, NT, preferred_element_type=F32)
            e = da * a
            p = pe + _dot2(e, before)
            beta = jnp.exp(lb)
            dz = (e - beta * (e + p)) * scale
            if diag:
                dz = jnp.where(causal, dz, 0.0)
            dzb = dz.astype(BF16)
            dq = dq + jnp.dot(dzb, kb, preferred_element_type=F32)
            dk_acc[pl.ds(r0, tq), :] += lax.dot_general(dzb, qb, TN, preferred_element_type=F32)
            dv_acc[pl.ds(r0, tq), :] += lax.dot_general(a.astype(BF16), dob, TN,
                                                        preferred_element_type=F32)
            cl = cl + jnp.sum(lnb, axis=1, keepdims=True)
            pe = pe + jnp.sum(e, axis=1, keepdims=True)
            return cl, pe, dq

        zero = jnp.zeros((tq, 1), F32)
        cl, pe, dq = lax.fori_loop(0, i_blk, lambda j, s: tile(j, s[0], s[1], s[2], False),
                                   (zero, zero, jnp.zeros((tq, HEAD_DIM), F32)))
        cl, pe, dq = tile(i_blk, cl, pe, dq, True)
        dq_ref[...] = dq.astype(BF16)

        @pl.when(i_blk == nq - 1)
        def _():
            dk_ref[...] = dk_acc[...].astype(BF16)
            dv_ref[...] = dv_acc[...].astype(BF16)

    h = n_heads
    qspec = pl.BlockSpec((tq, HEAD_DIM), lambda hh, i: (i, hh))
    hspec = pl.BlockSpec((t, HEAD_DIM), lambda hh, i: (0, hh))
    out = jax.ShapeDtypeStruct((t, h * HEAD_DIM), BF16)
    return pl.pallas_call(
        kern, name="sb_bwd", grid=(h, nq),
        in_specs=[qspec, qspec, qspec,
                  pl.BlockSpec((t, HEAD_DIM), lambda hh, i: (0, h + hh)),
                  pl.BlockSpec((t, HEAD_DIM), lambda hh, i: (0, 2 * h + hh))],
        out_specs=[qspec, hspec, hspec],
        out_shape=[out, out, out],
        scratch_shapes=[pltpu.VMEM((t, HEAD_DIM), F32), pltpu.VMEM((t, HEAD_DIM), F32)],
        compiler_params=pltpu.CompilerParams(
            dimension_semantics=("arbitrary", "arbitrary"), vmem_limit_bytes=VMEM_LIMIT),
    )(qkv, do, ctot, qkv, qkv)


GDN_ROWS = 512
TR_HEAD = 512


def _shift_rows(cur, halo, k, back):
    n = cur.shape[0]
    ext = jnp.concatenate([cur, halo], axis=0)
    return pltpu.roll(ext, k if back else n + 8 - k, 0)[:n]


def _conv_fwd(cur, halo, w):
    out = cur * w[GDN_CONV - 1:GDN_CONV, :]
    for i in range(GDN_CONV - 1):
        out = out + _shift_rows(cur, halo, GDN_CONV - 1 - i, True) * w[i:i + 1, :]
    return out


def _chunk_tri(n, upper):
    row = lax.broadcasted_iota(jnp.int32, (n, n), 0)
    col = lax.broadcasted_iota(jnp.int32, (n, n), 1)
    same = (row // GDN_CHUNK) == (col // GDN_CHUNK)
    tri = (col >= row) if upper else (col <= row)
    return jnp.logical_and(same, tri).astype(BF16)


def _lane_pick(x, lane):
    idx = lax.broadcasted_iota(jnp.int32, x.shape, 1)
    return jnp.sum(jnp.where(idx == lane, x, 0.0), axis=1, keepdims=True)


def _softplus(x):
    y = jnp.exp(-jnp.abs(x))
    u = 1.0 + y
    log1p = jnp.where(u == 1.0, y, jnp.log(u) * (y / jnp.where(u == 1.0, 1.0, u - 1.0)))
    return jnp.maximum(x, 0.0) + log1p


def _gdn_specs(t, tr, h, proj_seg0):
    def slab(seg):
        return ((tr, HEAD_DIM), lambda i, hh: (i, seg * h + hh))

    def halo_before(seg):
        return ((8, HEAD_DIM), lambda i, hh: (jnp.maximum(i * (tr // 8) - 1, 0), seg * h + hh))

    return slab, halo_before


def _gdn_pre_fwd(proj_b, proj_c, conv_w8, ab, n_heads):
    t = proj_b.shape[0]
    h = n_heads
    d = h * HEAD_DIM
    tr = _pick(t, TR_HEAD)
    slab, halo = _gdn_specs(t, tr, h, 1)
    scale = HEAD_DIM ** -0.5

    def body(idx, q_c, q_h, k_c, k_h, v_c, v_h, wq, wk, wv, gbga, ab_v):
        i, hh = idx
        live = (i > 0).astype(F32)
        outs = []
        for cur, hal, w, kind in ((q_c, q_h, wq, "q"), (k_c, k_h, wk, "k"), (v_c, v_h, wv, "v")):
            u = _conv_fwd(cur, hal * live, w)
            s = u * _sigmoid(u)
            if kind != "v":
                s = s * lax.rsqrt(jnp.sum(s * s, axis=1, keepdims=True) + L2_EPS)
            if kind == "q":
                s = s * scale
            outs.append(s)
        beta = _sigmoid(_lane_pick(gbga, hh))
        a_log = _lane_pick(ab_v[0:1, :], hh)
        dt = _lane_pick(ab_v[1:2, :], hh)
        g = -jnp.exp(a_log) * _softplus(_lane_pick(gbga, h + hh) + dt)
        g_rep = jnp.broadcast_to(g, (tr, HEAD_DIM))
        big_g = _cdot3(_chunk_tri(tr, False), g_rep)
        return outs + [jnp.broadcast_to(beta, (tr, HEAD_DIM)), big_g]

    wspec = lambda seg: ((8, HEAD_DIM), lambda i, hh: (0, seg * h + hh))
    out = ((t, d), F32, (tr, HEAD_DIM), lambda i, hh: (i, hh), "set")
    return _rowcall(
        "gdn_pre_fwd", body, (t // tr, h),
        [(proj_b,) + slab(1), (proj_b,) + halo(1), (proj_b,) + slab(2), (proj_b,) + halo(2),
         (proj_b,) + slab(3), (proj_b,) + halo(3),
         (conv_w8,) + wspec(0), (conv_w8,) + wspec(1), (conv_w8,) + wspec(2),
         (proj_c, (tr, LANES), lambda i, hh: (i, 0)), _full(ab)],
        [out] * 5)


def _gdn_consts():
    row = lax.broadcasted_iota(jnp.int32, (GDN_CHUNK, GDN_CHUNK), 0)
    col = lax.broadcasted_iota(jnp.int32, (GDN_CHUNK, GDN_CHUNK), 1)
    lane = lax.broadcasted_iota(jnp.int32, (GDN_CHUNK, HEAD_DIM), 1)
    return row > col, row >= col, (row == col).astype(F32), (lane == 0).astype(BF16)


def _gdn_local(q, k, v, be, ge, consts):
    lower, tril, eye, sel = consts
    kb_ = k * be
    vb_ = v * be
    e_g = jnp.exp(ge)
    kg = kb_ * e_g
    p1, p2, p3 = _split3(ge)
    g_i = _bdot(p1, sel, NT) + _bdot(p2, sel, NT) + _bdot(p3, sel, NT)
    g_j = _bdot(sel, p1, NT) + _bdot(sel, p2, NT) + _bdot(sel, p3, NT)
    dec = jnp.where(tril, jnp.exp(jnp.minimum(g_i - g_j, 0.0)), 0.0)
    kk = _bdot(kb_, k, NT)
    qk = _bdot(q, k, NT)
    g_last = jnp.min(ge, axis=0, keepdims=True)
    kdec_f = jnp.exp(g_last - ge)
    return dict(kb=kb_, vb=vb_, e_g=e_g, kg=kg, dec=dec, kk=kk, qk=qk, kdec_f=kdec_f,
                k_dec=k * kdec_f, q_dec=q * e_g, gamma=jnp.exp(g_last),
                intra=jnp.where(tril, qk * dec, 0.0))


def _wy_lower_t(k, be, ge, sel):
    row = lax.broadcasted_iota(jnp.int32, (GDN_CHUNK, GDN_CHUNK), 0)
    col = lax.broadcasted_iota(jnp.int32, (GDN_CHUNK, GDN_CHUNK), 1)
    p1, p2, p3 = _split3(ge)
    g_row = _bdot(p1, sel, NT) + _bdot(p2, sel, NT) + _bdot(p3, sel, NT)
    g_col = _bdot(sel, p1, NT) + _bdot(sel, p2, NT) + _bdot(sel, p3, NT)
    dec_t = jnp.exp(jnp.minimum(g_col - g_row, 0.0))
    return jnp.where(col > row, _bdot(k, k * be, NT) * dec_t, 0.0)


def _unit_lower_inverse(lw_t):
    shape = lw_t.shape
    row = lax.broadcasted_iota(jnp.int32, shape, 1)
    col = lax.broadcasted_iota(jnp.int32, shape, 2)
    inv = (row == col).astype(F32)
    unit = (lax.broadcasted_iota(jnp.int32, (1, 1, GDN_CHUNK), 2))
    for i in range(1, GDN_CHUNK):
        s = jnp.sum(lw_t[:, :, i:i + 1] * inv, axis=1, keepdims=True)
        inv = jnp.where(row == i, (unit == i).astype(F32) - s, inv)
    return inv


def _gdn_fwd(q, k, v, be, ge, n_heads):
    t = q.shape[0]
    h = n_heads
    tr = _pick(t, GDN_ROWS)
    nc = tr // GDN_CHUNK

    def kern(q_ref, k_ref, v_ref, b_ref, g_ref, o_ref, s_ref, tm_ref, state):
        consts = _gdn_consts()
        lower, tril, eye, sel = consts

        @pl.when(pl.program_id(1) == 0)
        def _():
            state[...] = jnp.zeros_like(state)

        lw_t = []
        for ci in range(nc):
            rows = pl.ds(ci * GDN_CHUNK, GDN_CHUNK)
            lw_t.append(_wy_lower_t(k_ref[rows, :], b_ref[rows, :], g_ref[rows, :], sel))
        t_all = _unit_lower_inverse(jnp.stack(lw_t))
        for ci in range(nc):
            tm_ref[pl.ds(ci * GDN_CHUNK, GDN_CHUNK), :] = t_all[ci]

        def chunk(ci, carry):
            rows = pl.ds(pl.multiple_of(ci * GDN_CHUNK, GDN_CHUNK), GDN_CHUNK)
            loc = _gdn_local(q_ref[rows, :], k_ref[rows, :], v_ref[rows, :], b_ref[rows, :],
                             g_ref[rows, :], consts)
            t_mat = tm_ref[rows, :]
            u = _bdot(t_mat, loc["vb"])
            w = _bdot(t_mat, loc["kg"])
            s = state[...]
            s_ref[ci] = s
            v_new = u - _bdot(w, s)
            o_ref[rows, :] = _bdot(loc["q_dec"], s) + _bdot(loc["intra"], v_new)
            state[...] = s * loc["gamma"] + _bdot(loc["k_dec"], v_new, TN)
            return carry

        lax.fori_loop(0, nc, chunk, 0)

    slab = pl.BlockSpec((tr, HEAD_DIM), lambda hh, j: (j, hh))
    return pl.pallas_call(
        kern, name="gdn_fwd", grid=(h, t // tr),
        in_specs=[slab] * 5,
        out_specs=[slab,
                   pl.BlockSpec((None, nc, HEAD_DIM, HEAD_DIM), lambda hh, j: (hh, j, 0, 0)),
                   pl.BlockSpec((None, tr, GDN_CHUNK), lambda hh, j: (hh, j, 0))],
        out_shape=[jax.ShapeDtypeStruct((t, h * HEAD_DIM), F32),
                   jax.ShapeDtypeStruct((h, t // GDN_CHUNK, HEAD_DIM, HEAD_DIM), F32),
                   jax.ShapeDtypeStruct((h, t, GDN_CHUNK), F32)],
        scratch_shapes=[pltpu.VMEM((HEAD_DIM, HEAD_DIM), F32)],
        compiler_params=pltpu.CompilerParams(
            dimension_semantics=("arbitrary", "arbitrary"), vmem_limit_bytes=VMEM_LIMIT),
    )(q, k, v, be, ge)


def _gdn_bwd(q, k, v, be, ge, do, states, tms, n_heads):
    t = q.shape[0]
    h = n_heads
    tr = _pick(t, GDN_ROWS)
    nc = tr // GDN_CHUNK
    nj = t // tr

    def kern(q_ref, k_ref, v_ref, b_ref, g_ref, do_ref, s_ref, tm_ref,
             dq_ref, dk_ref, dv_ref, db_ref, dg_ref, dstate):
        consts = _gdn_consts()
        lower, tril, eye, sel = consts
        last_row = lax.broadcasted_iota(jnp.int32, (GDN_CHUNK, HEAD_DIM), 0) == GDN_CHUNK - 1

        @pl.when(pl.program_id(1) == 0)
        def _():
            dstate[...] = jnp.zeros_like(dstate)

        def lsum(x):
            return jnp.sum(x, axis=1, keepdims=True)

        def chunk(cc, carry):
            ci = nc - 1 - cc
            rows = pl.ds(pl.multiple_of(ci * GDN_CHUNK, GDN_CHUNK), GDN_CHUNK)
            qv, kv, vv, bev = q_ref[rows, :], k_ref[rows, :], v_ref[rows, :], b_ref[rows, :]
            loc = _gdn_local(qv, kv, vv, bev, g_ref[rows, :], consts)
            t_mat = tm_ref[rows, :]
            s = s_ref[ci]
            d_o = do_ref[rows, :]
            d_s = dstate[...]
            u = _bdot(t_mat, loc["vb"])
            w = _bdot(t_mat, loc["kg"])
            v_new = u - _bdot(w, s)
            dv_new = _bdot(loc["intra"], d_o, TN) + _bdot(loc["k_dec"], d_s)
            d_intra = jnp.where(tril, _bdot(d_o, v_new, NT), 0.0)
            dq_dec = _bdot(d_o, s, NT)
            dk_dec = _bdot(v_new, d_s, NT)
            dgamma = jnp.sum(lsum(d_s * s), axis=0, keepdims=True)
            dw = -_bdot(dv_new, s, NT)
            dstate[...] = (_bdot(loc["q_dec"], d_o, TN) + loc["gamma"] * d_s
                           - _bdot(w, dv_new, TN))
            dtm = _bdot(dv_new, loc["vb"], NT) + _bdot(dw, loc["kg"], NT)
            dvb = _bdot(t_mat, dv_new, TN)
            dkg = _bdot(t_mat, dw, TN)
            dlw = jnp.where(lower, -_dot6(t_mat, _dot6(dtm, t_mat, NT), TN), 0.0)
            dkk = dlw * loc["dec"]
            dqk = d_intra * loc["dec"]
            ddec = dlw * loc["kk"] + d_intra * loc["qk"]
            dkb = _bdot(dkk, kv) + dkg * loc["e_g"]
            dk = (_bdot(dkk, loc["kb"], TN) + _bdot(dqk, qv, TN) + dk_dec * loc["kdec_f"]
                  + dkb * bev)
            dq = _bdot(dqk, kv) + dq_dec * loc["e_g"]
            dgd = ddec * loc["dec"]
            r_kdec = lsum(dk_dec * loc["k_dec"])
            col_sums = lsum(_dot2(dgd, sel, TN))
            d_big_g = (lsum(dgd) - col_sums
                       + lsum(dq_dec * loc["q_dec"]) - r_kdec + lsum(dkg * loc["kg"]))
            d_last = jnp.sum(r_kdec, axis=0, keepdims=True) + dgamma * loc["gamma"][:, 0:1]
            d_big_g = jnp.broadcast_to(d_big_g, (GDN_CHUNK, HEAD_DIM)) + jnp.where(last_row, d_last,
                                                                                  0.0)
            dq_ref[rows, :] = dq
            dk_ref[rows, :] = dk
            dv_ref[rows, :] = dvb * bev
            db_ref[rows, :] = jnp.broadcast_to(lsum(dkb * kv) + lsum(dvb * vv),
                                               (GDN_CHUNK, HEAD_DIM))
            dg_ref[rows, :] = d_big_g
            return carry

        lax.fori_loop(0, nc, chunk, 0)

    slab = pl.BlockSpec((tr, HEAD_DIM), lambda hh, j: (nj - 1 - j, hh))
    out = jax.ShapeDtypeStruct((t, h * HEAD_DIM), F32)
    return pl.pallas_call(
        kern, name="gdn_bwd", grid=(h, nj),
        in_specs=[slab] * 6 + [
            pl.BlockSpec((None, nc, HEAD_DIM, HEAD_DIM), lambda hh, j: (hh, nj - 1 - j, 0, 0)),
            pl.BlockSpec((None, tr, GDN_CHUNK), lambda hh, j: (hh, nj - 1 - j, 0))],
        out_specs=[slab] * 5,
        out_shape=[out] * 5,
        scratch_shapes=[pltpu.VMEM((HEAD_DIM, HEAD_DIM), F32)],
        compiler_params=pltpu.CompilerParams(
            dimension_semantics=("arbitrary", "arbitrary"), vmem_limit_bytes=VMEM_LIMIT),
    )(q, k, v, be, ge, do, states, tms)


def _gdn_pre_bwd(proj_b, proj_c, conv_w8, ab, dq, dk, dv, dbe, dge, n_heads):
    t = proj_b.shape[0]
    h = n_heads
    d = h * HEAD_DIM
    tr = _pick(t, TR_HEAD)
    slab, halo = _gdn_specs(t, tr, h, 1)
    scale = HEAD_DIM ** -0.5

    def body(idx, q_c, q_h, k_c, k_h, v_c, v_h, wq, wk, wv, gbga, ab_v, dq_v, dk_v, dv_v, dbe_v,
             dge_v):
        i, hh = idx
        live = (i > 0).astype(F32)
        outs = []
        for cur, hal, w, dy, kind in ((q_c, q_h, wq, dq_v, "q"), (k_c, k_h, wk, dk_v, "k"),
                                      (v_c, v_h, wv, dv_v, "v")):
            u = _conv_fwd(cur, hal * live, w)
            sg = _sigmoid(u)
            if kind == "v":
                ds = dy
            else:
                s = u * sg
                r = lax.rsqrt(jnp.sum(s * s, axis=1, keepdims=True) + L2_EPS)
                y = s * r
                if kind == "q":
                    dy = dy * scale
                ds = r * (dy - y * jnp.sum(dy * y, axis=1, keepdims=True))
            outs.append(ds * (sg * (1.0 + u * (1.0 - sg))))
        lane = lax.broadcasted_iota(jnp.int32, (tr, LANES), 1)
        lane1 = lax.broadcasted_iota(jnp.int32, (1, LANES), 1)
        beta = _sigmoid(_lane_pick(gbga, hh))
        a_neg = -jnp.exp(_lane_pick(ab_v[0:1, :], hh))
        xg = _lane_pick(gbga, h + hh) + _lane_pick(ab_v[1:2, :], hh)
        g = a_neg * _softplus(xg)
        dgb = dbe_v * (beta * (1.0 - beta))
        dg = _cdot3(_chunk_tri(tr, True), dge_v)
        dga = dg * (a_neg * _sigmoid(xg))
        d_gates = jnp.where(lane == hh, dgb, 0.0) + jnp.where(lane == h + hh, dga, 0.0)
        d_ab = (jnp.where(lane1 == hh, jnp.sum(dg * g, axis=0, keepdims=True), 0.0)
                + jnp.where(lane1 == h + hh, jnp.sum(dga, axis=0, keepdims=True), 0.0))
        return outs + [d_gates, d_ab]

    wspec = lambda seg: ((8, HEAD_DIM), lambda i, hh: (0, seg * h + hh))
    hs = ((tr, HEAD_DIM), lambda i, hh: (i, hh))
    out = ((t, d), F32) + hs + ("set",)
    return _rowcall(
        "gdn_pre_bwd", body, (t // tr, h),
        [(proj_b,) + slab(1), (proj_b,) + halo(1), (proj_b,) + slab(2), (proj_b,) + halo(2),
         (proj_b,) + slab(3), (proj_b,) + halo(3),
         (conv_w8,) + wspec(0), (conv_w8,) + wspec(1), (conv_w8,) + wspec(2),
         (proj_c, (tr, LANES), lambda i, hh: (i, 0)), _full(ab),
         (dq,) + hs, (dk,) + hs, (dv,) + hs, (dbe,) + hs, (dge,) + hs],
        [out, out, out,
         ((t, LANES), F32, (tr, LANES), lambda i, hh: (i, 0), "acc_inner"),
         ((1, LANES), F32, (1, LANES), lambda i, hh: (0, 0), "acc_all")])


def _conv_bwd(proj_b, conv_w8, du_q, du_k, du_v, n_heads):
    t = proj_b.shape[0]
    h = n_heads
    d = h * HEAD_DIM
    tr = _pick(t, TR_HEAD)
    ni = t // tr

    def body(idx, q_c, q_h, k_c, k_h, v_c, v_h, wq, wk, wv, uq, uq_n, uk, uk_n, uv, uv_n):
        hh, i = idx
        live_b = (i > 0).astype(F32)
        live_a = (i < ni - 1).astype(F32)
        d_ins, d_ws = [], []
        for cur, hal, w, du, du_n in ((q_c, q_h, wq, uq, uq_n), (k_c, k_h, wk, uk, uk_n),
                                      (v_c, v_h, wv, uv, uv_n)):
            hal = hal * live_b
            du_n = du_n * live_a
            d_in = du * w[GDN_CONV - 1:GDN_CONV, :]
            rows = [jnp.sum(du * cur, axis=0, keepdims=True)]
            for i_tap in range(GDN_CONV - 2, -1, -1):
                kshift = GDN_CONV - 1 - i_tap
                d_in = d_in + _shift_rows(du, du_n, kshift, False) * w[i_tap:i_tap + 1, :]
                rows.insert(0, jnp.sum(du * _shift_rows(cur, hal, kshift, True), axis=0,
                                       keepdims=True))
            d_ins.append(d_in)
            tap = lax.broadcasted_iota(jnp.int32, (8, HEAD_DIM), 0)
            d_w = jnp.zeros((8, HEAD_DIM), F32)
            for i_tap in range(GDN_CONV):
                d_w = d_w + jnp.where(tap == i_tap, rows[i_tap], 0.0)
            d_ws.append(d_w)
        return d_ins + d_ws

    def slab(seg):
        return ((tr, HEAD_DIM), lambda hh, i: (i, seg * h + hh))

    def halo_b(seg):
        return ((8, HEAD_DIM), lambda hh, i: (jnp.maximum(i * (tr // 8) - 1, 0), seg * h + hh))

    hs = ((tr, HEAD_DIM), lambda hh, i: (i, hh))
    halo_a = ((8, HEAD_DIM), lambda hh, i: (jnp.minimum((i + 1) * (tr // 8), t // 8 - 1), hh))
    wspec = lambda seg: ((8, HEAD_DIM), lambda hh, i: (0, seg * h + hh))
    wout = ((8, d), F32, (8, HEAD_DIM), lambda hh, i: (0, hh), "acc_inner")
    out = ((t, d), BF16) + hs + ("set",)
    res = _rowcall(
        "conv_bwd", body, (h, ni),
        [(proj_b,) + slab(1), (proj_b,) + halo_b(1), (proj_b,) + slab(2), (proj_b,) + halo_b(2),
         (proj_b,) + slab(3), (proj_b,) + halo_b(3),
         (conv_w8,) + wspec(0), (conv_w8,) + wspec(1), (conv_w8,) + wspec(2),
         (du_q,) + hs, (du_q,) + halo_a, (du_k,) + hs, (du_k,) + halo_a, (du_v,) + hs,
         (du_v,) + halo_a],
        [out, out, out, wout, wout, wout])
    return res


def _gdn_post_fwd(o, proj_b, gnw, n_heads):
    t, d = o.shape
    h = n_heads
    tr = _pick(t, TR_HEAD)

    def body(idx, o_v, z, w):
        rstd = lax.rsqrt(jnp.mean(o_v * o_v, axis=1, keepdims=True) + NORM_EPS)
        return ((o_v * rstd) * w * (z * _sigmoid(z)),)

    hs = ((tr, HEAD_DIM), lambda i, hh: (i, hh))
    return _rowcall("gdn_post_fwd", body, (t // tr, h),
                    [(o,) + hs, (proj_b, (tr, HEAD_DIM), lambda i, hh: (i, 4 * h + hh)), _full(gnw)],
                    [((t, d), BF16) + hs + ("set",)])[0]


def _gdn_post_bwd(do_gdn, o, proj_b, gnw, n_heads):
    t, d = o.shape
    h = n_heads
    tr = _pick(t, TR_HEAD)

    def body(idx, dog, o_v, z, w):
        rstd = lax.rsqrt(jnp.mean(o_v * o_v, axis=1, keepdims=True) + NORM_EPS)
        n = o_v * rstd
        sg = _sigmoid(z)
        don = dog * (z * sg)
        dz = dog * (n * w) * (sg * (1.0 + z * (1.0 - sg)))
        dn = don * w
        d_o = rstd * (dn - n * jnp.mean(dn * n, axis=1, keepdims=True))
        return d_o, dz, jnp.sum(don * n, axis=0, keepdims=True)

    hs = ((tr, HEAD_DIM), lambda i, hh: (i, hh))
    return _rowcall("gdn_post_bwd", body, (t // tr, h),
                    [(do_gdn,) + hs, (o,) + hs,
                     (proj_b, (tr, HEAD_DIM), lambda i, hh: (i, 4 * h + hh)), _full(gnw)],
                    [((t, d), F32) + hs + ("set",), ((t, d), BF16) + hs + ("set",),
                     ((1, HEAD_DIM), F32, (1, HEAD_DIM), lambda i, hh: (0, 0), "acc_all")])


def _seg(arr, tr, d, seg):
    return (arr, (tr, d), lambda i: (i, seg))


def _sb_post_fwd(o_raw, proj_b):
    t, d = o_raw.shape
    tr = _pick(t, TR_WIDE)

    def body(idx, o_v, z):
        return (o_v * (z * _sigmoid(z)),)

    return _rowcall("sb_post_fwd", body, (t // tr,), [_seg(o_raw, tr, d, 0), _seg(proj_b, tr, d, 0)],
                    [((t, d), BF16, (tr, d), lambda i: (i, 0), "set")])[0]


def _sb_post_bwd(do_sb, o_raw, proj_b):
    t, d = o_raw.shape
    tr = _pick(t, TR_WIDE)

    def body(idx, dos, o_v, z):
        sg = _sigmoid(z)
        return dos * (z * sg), dos * o_v * (sg * (1.0 + z * (1.0 - sg)))

    out = ((t, d), BF16, (tr, d), lambda i: (i, 0), "set")
    return _rowcall("sb_post_bwd", body, (t // tr,),
                    [_seg(do_sb, tr, d, 0), _seg(o_raw, tr, d, 0), _seg(proj_b, tr, d, 0)],
                    [out, out])


def _merge_fwd(proj_b, p_sb, p_gdn):
    t, d = p_sb.shape
    tr = _pick(t, TR_WIDE)

    def body(idx, m_sb, m_gdn, ps, pg):
        return (_sigmoid(m_sb) * ps + _sigmoid(m_gdn) * pg,)

    return _rowcall("merge_fwd", body, (t // tr,),
                    [_seg(proj_b, tr, d, 5), _seg(proj_b, tr, d, 6), _seg(p_sb, tr, d, 0),
                     _seg(p_gdn, tr, d, 0)],
                    [((t, d), BF16, (tr, d), lambda i: (i, 0), "set")])[0]


def _merge_bwd(dy, proj_b, p_sb, p_gdn):
    t, d = p_sb.shape
    tr = _pick(t, TR_WIDE)

    def body(idx, dy_v, m_sb, m_gdn, ps, pg):
        s1 = _sigmoid(m_sb)
        s2 = _sigmoid(m_gdn)
        return s1 * dy_v, s2 * dy_v, dy_v * ps * (s1 * (1.0 - s1)), dy_v * pg * (s2 * (1.0 - s2))

    out = ((t, d), BF16, (tr, d), lambda i: (i, 0), "set")
    return _rowcall("merge_bwd", body, (t // tr,),
                    [_seg(dy, tr, d, 0), _seg(proj_b, tr, d, 5), _seg(proj_b, tr, d, 6),
                     _seg(p_sb, tr, d, 0), _seg(p_gdn, tr, d, 0)],
                    [out] * 4)


def _tail(x, r, target, gate, final_w):
    t, d = x.shape
    tr = _pick(t, TR_WIDE)

    def body(idx, x_v, r_v, tg, gt, fw):
        x2 = x_v + gt * r_v
        rstd = lax.rsqrt(jnp.mean(x2 * x2, axis=1, keepdims=True) + NORM_EPS)
        n = x2 * rstd
        diff = n * fw - tg
        loss = 0.5 * jnp.sum(jnp.mean(diff * diff, axis=1, keepdims=True), axis=0, keepdims=True)
        dout = diff * (1.0 / d)
        dn = dout * fw
        dx2 = rstd * (dn - n * jnp.mean(dn * n, axis=1, keepdims=True))
        return (dx2, gt * dx2, jnp.sum(dout * n, axis=0, keepdims=True),
                jnp.sum(dx2 * r_v, axis=0, keepdims=True), jnp.broadcast_to(loss, (1, LANES)))

    rb = ((tr, d), lambda i: (i, 0))
    vec = ((1, d), F32, (1, d), lambda i: (0, 0), "acc_all")
    return _rowcall("tail", body, (t // tr,),
                    [(x,) + rb, (r,) + rb, (target,) + rb, _full(gate), _full(final_w)],
                    [((t, d), F32) + rb + ("set",), ((t, d), BF16) + rb + ("set",), vec, vec,
                     ((1, LANES), F32, (1, LANES), lambda i: (0, 0), "acc_all")])


def _pad_to(a, rows, cols):
    return jnp.pad(a, ((0, rows - a.shape[0]), (0, cols - a.shape[1])))


def kernel(x, c, w_ada, b_ada, norm_w, w_in, gdn_conv_w, gdn_a_log, gdn_dt_bias, gdn_norm_w, w_proj_sb, w_proj_gdn, w_out, final_norm_w, loss_target, m_w_ada, m_b_ada, m_norm_w, m_w_in, m_gdn_conv_w, m_gdn_a_log, m_gdn_dt_bias, m_gdn_norm_w, m_w_proj_sb, m_w_proj_gdn, m_w_out, m_final_norm_w, v_w_ada, v_b_ada, v_norm_w, v_w_in, v_gdn_conv_w, v_gdn_a_log, v_gdn_dt_bias, v_gdn_norm_w, v_w_proj_sb, v_w_proj_gdn, v_w_out, v_final_norm_w):
    t, d = x.shape[1], x.shape[2]
    h = d // HEAD_DIM
    me = 4 * lax.axis_index("x") + 2 * lax.axis_index("y") + lax.axis_index("c")
    x2d = x[0]
    tgt = loss_target[0]
    ada_cols = w_ada.shape[2]
    in_cols = w_in.shape[2]
    rows_p = w_out.shape[1]

    w_in_all = _exchange("gather_w_in", w_in[0].astype(BF16), True)
    w_in_full = jnp.transpose(w_in_all, (1, 0, 2)).reshape(d, N_DEV * in_cols)
    w_main = jnp.concatenate([w_in_full[:, :8 * d], w_in_full[:, 8 * d + 2 * h:]], axis=1)
    w_g = _pad_to(w_in_full[:, 8 * d:8 * d + 2 * h], d, LANES)
    w_main_t = w_main.T
    w_g_t = w_g.T
    w_sq = jnp.stack([w_proj_sb[0], w_proj_gdn[0], w_out[0]]).astype(BF16)
    w_sq_all = _exchange("gather_w_sq", w_sq, True)
    w_sq_full = jnp.transpose(w_sq_all, (1, 0, 2, 3)).reshape(3, d, d)
    wp_sb, wp_gdn, wo = w_sq_full[0], w_sq_full[1], w_sq_full[2]
    conv_all = _exchange("gather_conv", _pad_to(gdn_conv_w[0], 8, gdn_conv_w.shape[2]), True)
    conv_w8 = jnp.transpose(conv_all, (1, 0, 2)).reshape(8, 3 * d)
    c_all = _exchange("gather_c", _pad_to(c, 8, d), True)[:, 0, :]

    sc_all = c_all * _sigmoid(c_all)
    mod_part = _matmul("ada_fwd", _pad_to(sc_all, 16, d).astype(BF16), w_ada[0].astype(BF16), F32)
    mod_part = mod_part[:N_DEV] + lax.dynamic_slice(b_ada, (0, me * ada_cols), (1, ada_cols))
    mod_rows = _exchange("a2a_mod", _pad_to(mod_part, 8, ada_cols).reshape(N_DEV, 1, ada_cols)
                         * jnp.ones((1, 8, 1), F32), False)
    mod = mod_rows[:, 0, :].reshape(1, 3 * d)
    shift, scale, gate = mod[:, :d], mod[:, d:2 * d], mod[:, 2 * d:]

    hmod = _norm_mod_fwd(x2d, shift, scale, norm_w)
    proj_a = _matmul("in_proj_a", hmod, w_main, BF16, n_cols=3 * d, col0=0)
    proj_b = _matmul("in_proj_b", hmod, w_main, F32, n_cols=7 * d, col0=3 * d)
    proj_c = _matmul("in_proj_c", hmod, w_g, F32)
    o_sb_raw, ctot = _sb_fwd(proj_a, h)
    o_sb = _sb_post_fwd(o_sb_raw, proj_b)
    ab = _pad_to(jnp.concatenate([gdn_a_log, gdn_dt_bias], axis=0), 8, LANES)
    gq, gk, gv, g_beta, g_cum = _gdn_pre_fwd(proj_b, proj_c, conv_w8, ab, h)
    o_gdn_raw, states, tms = _gdn_fwd(gq, gk, gv, g_beta, g_cum, h)
    o_gdn = _gdn_post_fwd(o_gdn_raw, proj_b, gdn_norm_w, h)
    p_sb = _matmul("proj_sb", o_sb, wp_sb, F32)
    p_gdn = _matmul("proj_gdn", o_gdn, wp_gdn, F32)
    y = _merge_fwd(proj_b, p_sb, p_gdn)
    r = _matmul("out_proj", y, wo, F32)
    dx2, dr, d_final_w, d_gate, loss_part = _tail(x2d, r, tgt, gate, final_norm_w.reshape(1, d))

    dy = _matmul("d_out_proj", dr, wo.T, F32)
    dw_out = _matmul("dw_out", y.T, dr, F32)
    dp_sb, dp_gdn, dm_sb, dm_gdn = _merge_bwd(dy, proj_b, p_sb, p_gdn)
    do_sb = _matmul("d_proj_sb", dp_sb, wp_sb.T, F32)
    dw_p_sb = _matmul("dw_proj_sb", o_sb.T, dp_sb, F32)
    do_gdn = _matmul("d_proj_gdn", dp_gdn, wp_gdn.T, F32)
    dw_p_gdn = _matmul("dw_proj_gdn", o_gdn.T, dp_gdn, F32)
    do_sb_raw, d_sbz = _sb_post_bwd(do_sb, o_sb_raw, proj_b)
    d_sbq, d_sbk, d_sbv = _sb_bwd(proj_a, do_sb_raw, ctot, h)
    d_o_gdn, d_gz, d_gnw = _gdn_post_bwd(do_gdn, o_gdn_raw, proj_b, gdn_norm_w, h)
    dgq, dgk, dgv, dgbe, dgcum = _gdn_bwd(gq, gk, gv, g_beta, g_cum, d_o_gdn, states, tms, h)
    du_q, du_k, du_v, d_gates, d_ab = _gdn_pre_bwd(proj_b, proj_c, conv_w8, ab, dgq, dgk, dgv,
                                                  dgbe, dgcum, h)
    d_gq, d_gk, d_gv, dcw_q, dcw_k, dcw_v = _conv_bwd(proj_b, conv_w8, du_q, du_k, du_v, h)
    dproj = jnp.concatenate([d_sbq, d_sbk, d_sbv, d_sbz, d_gq, d_gk, d_gv, d_gz, dm_sb, dm_gdn],
                            axis=1)
    d_gates_b = d_gates.astype(BF16)
    dh_a = _matmul("d_in_proj", dproj, w_main_t, F32)
    dh_b = _matmul("d_in_proj_g", d_gates_b, w_g_t, F32)
    hmod_t = hmod.T
    dw_main = _matmul("dw_in", hmod_t, dproj, F32)
    dw_g = _matmul("dw_in_g", hmod_t, d_gates_b, F32)
    grad_x, d_shift, d_scale, d_norm_w = _norm_mod_bwd(dh_a, dh_b, x2d, dx2, scale, norm_w)

    dmod = jnp.concatenate([d_shift, d_scale, d_gate], axis=1)
    small = jnp.concatenate([dmod, d_norm_w, d_final_w, d_ab[:, :h], d_ab[:, h:2 * h], d_gnw,
                             loss_part], axis=1)
    n_small = small.shape[1]
    small_all = _exchange("gather_small", _pad_to(small, 8, n_small), True)[:, 0:1, :]
    small_w = jnp.concatenate([b_ada, norm_w, final_norm_w.reshape(1, d), gdn_a_log, gdn_dt_bias,
                               gdn_norm_w, jnp.zeros((1, LANES), F32)], axis=1)
    small_m = jnp.concatenate([m_b_ada, m_norm_w, m_final_norm_w.reshape(1, d), m_gdn_a_log,
                               m_gdn_dt_bias, m_gdn_norm_w, jnp.zeros((1, LANES), F32)], axis=1)
    small_v = jnp.concatenate([v_b_ada, v_norm_w, v_final_norm_w.reshape(1, d), v_gdn_a_log,
                               v_gdn_dt_bias, v_gdn_norm_w, jnp.ones((1, LANES), F32)], axis=1)
    s_g, s_d, s_m, s_v = _adamw("adamw_small", small_all, small_w, small_m, small_v)
    cuts = [3 * d, 4 * d, 5 * d, 5 * d + h, 5 * d + 2 * h, 5 * d + 2 * h + HEAD_DIM]

    def split_small(a):
        b, nw, fw, al, dtb, gn, _ = jnp.split(a, cuts, axis=1)
        return b, nw, fw.reshape(d), al, dtb, gn

    loss = s_g[0, cuts[-1]]

    dmod_all = small_all[:, 0, :3 * d]
    dmod_mine = lax.dynamic_slice(dmod_all, (0, me * ada_cols), (N_DEV, ada_cols))
    dw_ada = _matmul("dw_ada", _pad_to(sc_all.T, d, LANES).astype(BF16),
                     _pad_to(dmod_mine, LANES, ada_cols).astype(BF16), F32)
    ada = _adamw("adamw_ada", dw_ada[None], w_ada[0], m_w_ada[0], v_w_ada[0])

    dw_in_full = jnp.concatenate([dw_main[:, :8 * d], dw_g[:, :2 * h], dw_main[:, 8 * d:]], axis=1)
    dw_in_parts = _exchange("a2a_dw_in",
                            jnp.transpose(dw_in_full.reshape(d, N_DEV, in_cols), (1, 0, 2)), False)
    win = _adamw("adamw_w_in", dw_in_parts, w_in[0], m_w_in[0], v_w_in[0])
    dw_sq = jnp.stack([dw_p_sb, dw_p_gdn, dw_out]).reshape(3, N_DEV, rows_p, d)
    dw_sq_parts = _exchange("a2a_dw_sq", jnp.transpose(dw_sq, (1, 0, 2, 3)), False)
    sq = _adamw("adamw_sq", dw_sq_parts.reshape(N_DEV, 3 * rows_p, d),
                jnp.concatenate([w_proj_sb[0], w_proj_gdn[0], w_out[0]], axis=0),
                jnp.concatenate([m_w_proj_sb[0], m_w_proj_gdn[0], m_w_out[0]], axis=0),
                jnp.concatenate([v_w_proj_sb[0], v_w_proj_gdn[0], v_w_out[0]], axis=0))
    dcw = jnp.concatenate([dcw_q, dcw_k, dcw_v], axis=1)
    cw_cols = gdn_conv_w.shape[2]
    dcw_parts = _exchange("a2a_dconv",
                          jnp.transpose(dcw.reshape(8, N_DEV, cw_cols), (1, 0, 2)), False)
    cw = _adamw("adamw_conv", dcw_parts, _pad_to(gdn_conv_w[0], 8, cw_cols),
                _pad_to(m_gdn_conv_w[0], 8, cw_cols),
                jnp.pad(v_gdn_conv_w[0], ((0, 8 - GDN_CONV), (0, 0)), constant_values=1.0))

    outs = [loss, grad_x[None]]
    for k_out in range(4):
        b, nw, fw, al, dtb, gn = split_small((s_g, s_d, s_m, s_v)[k_out])
        sq3 = sq[k_out].reshape(3, 1, rows_p, d)
        outs += [ada[k_out][None], b, nw, win[k_out][None], cw[k_out][None, :GDN_CONV], al, dtb, gn,
                 sq3[0], sq3[1], sq3[2], fw]
    return tuple(outs)
```

```python
import jax
import jax.numpy as jnp
from jax import lax
from jax.experimental import pallas as pl
from jax.experimental.pallas import tpu as pltpu

F32 = jnp.float32
BF16 = jnp.bfloat16
N_DEV = 8
HEAD_DIM = 128
LANES = 128
GDN_CHUNK = 64
GDN_CONV = 4
NORM_EPS = 1e-6
L2_EPS = 1e-6
ADAM_LR = 0.001
ADAM_B1 = 0.9
ADAM_B2 = 0.999
ADAM_EPS = 1e-08
ADAM_WD = 0.01
ADAM_STEP = 10
VMEM_LIMIT = 56 * 1024 * 1024
MESH = pl.DeviceIdType.MESH
NT = (((1,), (1,)), ((), ()))
TN = (((0,), (0,)), ((), ()))


def _pick(n, pref):
    t = min(pref, n)
    while n % t:
        t //= 2
    return t


def _sigmoid(x):
    return 1.0 / (1.0 + jnp.exp(-x))


def _bdot(a, b, dims=None):
    a = a.astype(BF16)
    b = b.astype(BF16)
    if dims is None:
        return jnp.dot(a, b, preferred_element_type=F32)
    return lax.dot_general(a, b, dims, preferred_element_type=F32)


def _split2(x):
    hi = x.astype(BF16)
    lo = (x - hi.astype(F32)).astype(BF16)
    return hi, lo


def _split3(x):
    p1 = x.astype(BF16)
    r = x - p1.astype(F32)
    p2 = r.astype(BF16)
    p3 = (r - p2.astype(F32)).astype(BF16)
    return p1, p2, p3


def _dot2(x, c, dims=None):
    hi, lo = _split2(x)
    return _bdot(hi, c, dims) + _bdot(lo, c, dims)


def _cdot3(c, x):
    p1, p2, p3 = _split3(x)
    return _bdot(c, p1) + _bdot(c, p2) + _bdot(c, p3)


def _dot6(a, b, dims=None):
    a1, a2, a3 = _split3(a)
    b1, b2, b3 = _split3(b)
    small = (_bdot(a1, b3, dims) + _bdot(a3, b1, dims)) + _bdot(a2, b2, dims)
    return _bdot(a1, b1, dims) + ((_bdot(a1, b2, dims) + _bdot(a2, b1, dims)) + small)


def _full(a):
    nd = a.ndim
    return (a, a.shape, lambda *idx: (0,) * nd)


def _rowcall(name, body, grid, ins, outs):
    n_in = len(ins)
    modes = [o[4] for o in outs]
    n_ax = len(grid)

    def kern(*refs):
        idx = tuple(pl.program_id(a) for a in range(n_ax))
        vals = body(idx, *[r[...] for r in refs[:n_in]])
        first_all = idx[0] == 0
        for a in range(1, n_ax):
            first_all = jnp.logical_and(first_all, idx[a] == 0)
        first_inner = idx[-1] == 0
        for r, v, mode in zip(refs[n_in:], vals, modes):
            v = v.astype(r.dtype)
            if mode == "set":
                r[...] = v
            else:
                first = first_all if mode == "acc_all" else first_inner

                @pl.when(first)
                def _(r=r, v=v):
                    r[...] = v

                @pl.when(jnp.logical_not(first))
                def _(r=r, v=v):
                    r[...] += v

    return pl.pallas_call(
        kern, name=name, grid=grid,
        in_specs=[pl.BlockSpec(b, m) for (_, b, m) in ins],
        out_specs=[pl.BlockSpec(o[2], o[3]) for o in outs],
        out_shape=[jax.ShapeDtypeStruct(o[0], o[1]) for o in outs],
        compiler_params=pltpu.CompilerParams(
            dimension_semantics=("arbitrary",) * n_ax, vmem_limit_bytes=VMEM_LIMIT),
    )(*[a for (a, _, _) in ins])


def _matmul(name, a, b, out_dtype, n_cols=None, col0=0, tm=1024, tn=1024, tk=1024):
    m, k = a.shape
    n = b.shape[1] if n_cols is None else n_cols
    tm = _pick(m, tm)
    tn = _pick(n, tn)
    while col0 % tn:
        tn //= 2
    tk = _pick(k, tk)
    nk = k // tk
    cb = col0 // tn

    def kern(a_ref, b_ref, o_ref, acc_ref):
        kk = pl.program_id(2)
        part = jnp.dot(a_ref[...], b_ref[...], preferred_element_type=F32)
        if nk == 1:
            o_ref[...] = part.astype(o_ref.dtype)
        else:
            @pl.when(kk == 0)
            def _():
                acc_ref[...] = part

            @pl.when(kk > 0)
            def _():
                acc_ref[...] += part

            @pl.when(kk == nk - 1)
            def _():
                o_ref[...] = acc_ref[...].astype(o_ref.dtype)

    return pl.pallas_call(
        kern, name=name, grid=(m // tm, n // tn, nk),
        in_specs=[pl.BlockSpec((tm, tk), lambda i, j, kk: (i, kk)),
                  pl.BlockSpec((tk, tn), lambda i, j, kk: (kk, j + cb))],
        out_specs=pl.BlockSpec((tm, tn), lambda i, j, kk: (i, j)),
        out_shape=jax.ShapeDtypeStruct((m, n), out_dtype),
        scratch_shapes=[pltpu.VMEM((tm, tn), F32)],
        compiler_params=pltpu.CompilerParams(
            dimension_semantics=("arbitrary", "arbitrary", "arbitrary"),
            vmem_limit_bytes=VMEM_LIMIT),
    )(a, b)


def _exchange(name, x, gather):
    piece = x.shape if gather else x.shape[1:]

    def body(x_ref, out_ref, send_sems, recv_sems, local_sem):
        xi, yi, ci = lax.axis_index("x"), lax.axis_index("y"), lax.axis_index("c")
        me = 4 * xi + 2 * yi + ci
        sends = []
        for k in range(1, N_DEV):
            kx, ky, kc = (k >> 2) & 1, (k >> 1) & 1, k & 1
            peer = (xi ^ kx, yi ^ ky, ci ^ kc)
            pid = me ^ k
            src = x_ref if gather else x_ref.at[pid]
            cp = pltpu.make_async_remote_copy(
                src_ref=src, dst_ref=out_ref.at[me], send_sem=send_sems.at[k - 1],
                recv_sem=recv_sems.at[k - 1], device_id=peer, device_id_type=MESH)
            cp.start()
            sends.append(cp)
        mine = pltpu.make_async_copy(x_ref if gather else x_ref.at[me], out_ref.at[me], local_sem)
        mine.start()
        for k in range(1, N_DEV):
            pid = me ^ k
            pltpu.make_async_remote_copy(
                src_ref=x_ref if gather else x_ref.at[pid], dst_ref=out_ref.at[pid],
                send_sem=send_sems.at[k - 1], recv_sem=recv_sems.at[k - 1],
                device_id=(xi, yi, ci), device_id_type=MESH).wait_recv()
        for cp in sends:
            cp.wait_send()
        mine.wait()

    return pl.pallas_call(
        body, name=name,
        out_shape=jax.ShapeDtypeStruct((N_DEV,) + tuple(piece), x.dtype),
        in_specs=[pl.BlockSpec(memory_space=pl.ANY)],
        out_specs=pl.BlockSpec(memory_space=pl.ANY),
        scratch_shapes=[pltpu.SemaphoreType.DMA((N_DEV - 1,)),
                        pltpu.SemaphoreType.DMA((N_DEV - 1,)),
                        pltpu.SemaphoreType.DMA],
    )(x)


def _adamw(name, parts, w, m, v):
    p, r, c = parts.shape
    tr = r if r <= 64 else _pick(r, 64)

    def body(idx, parts_v, w_v, m_v, v_v):
        g = parts_v[0].astype(F32)
        for s in range(1, p):
            g = g + parts_v[s].astype(F32)
        m2 = ADAM_B1 * m_v + (1.0 - ADAM_B1) * g
        v2 = ADAM_B2 * v_v + (1.0 - ADAM_B2) * (g * g)
        m_hat = m2 / (1.0 - ADAM_B1 ** ADAM_STEP)
        v_hat = v2 / (1.0 - ADAM_B2 ** ADAM_STEP)
        delta = -ADAM_LR * (m_hat / (jnp.sqrt(v_hat) + ADAM_EPS) + ADAM_WD * w_v)
        return g, delta, m2, v2

    rb = ((tr, c), lambda i: (i, 0))
    return _rowcall(
        name, body, (r // tr,),
        [(parts, (p, tr, c), lambda i: (0, i, 0)), (w,) + rb, (m,) + rb, (v,) + rb],
        [((r, c), F32) + rb + ("set",)] * 4)


TR_WIDE = 256


def _norm_mod_fwd(x, shift, scale, norm_w):
    t, d = x.shape
    tr = _pick(t, TR_WIDE)

    def body(idx, x_v, sh, sc, nw):
        rstd = lax.rsqrt(jnp.mean(x_v * x_v, axis=1, keepdims=True) + NORM_EPS)
        return ((x_v * rstd) * nw * (1.0 + sc) + sh,)

    rb = ((tr, d), lambda i: (i, 0))
    return _rowcall("norm_mod_fwd", body, (t // tr,),
                    [(x,) + rb, _full(shift), _full(scale), _full(norm_w)],
                    [((t, d), BF16) + rb + ("set",)])[0]


def _norm_mod_bwd(dh_a, dh_b, x, dx2, scale, norm_w):
    t, d = x.shape
    tr = _pick(t, TR_WIDE)

    def body(idx, dha, dhb, x_v, dx2_v, sc, nw):
        dh = dha + dhb
        rstd = lax.rsqrt(jnp.mean(x_v * x_v, axis=1, keepdims=True) + NORM_EPS)
        xn = x_v * rstd
        m1 = 1.0 + sc
        dxn = dh * nw * m1
        dx = rstd * (dxn - xn * jnp.mean(dxn * xn, axis=1, keepdims=True))
        dhx = dh * xn
        return (dx2_v + dx,
                jnp.sum(dh, axis=0, keepdims=True),
                jnp.sum(dhx * nw, axis=0, keepdims=True),
                jnp.sum(dhx * m1, axis=0, keepdims=True))

    rb = ((tr, d), lambda i: (i, 0))
    vec = ((1, d), F32, (1, d), lambda i: (0, 0), "acc_all")
    return _rowcall("norm_mod_bwd", body, (t // tr,),
                    [(dh_a,) + rb, (dh_b,) + rb, (x,) + rb, (dx2,) + rb, _full(scale), _full(norm_w)],
                    [((t, d), F32) + rb + ("set",), vec, vec, vec])


SB_TILE = 512


SB_SCAN = 256


def _neg_log_not_beta(z):
    return jnp.maximum(z, 0.0) + jnp.log(1.0 + jnp.exp(-jnp.abs(z)))


def _key_scan(x, tri, later):
    n_blk = x.shape[1] // SB_SCAN
    parts = [x[:, g * SB_SCAN:(g + 1) * SB_SCAN] for g in range(n_blk)]
    sums = [jnp.sum(p, axis=1, keepdims=True) for p in parts]
    outs = []
    for g in range(n_blk):
        o = _dot2(parts[g], tri)
        others = range(g + 1, n_blk) if later else range(g)
        for g2 in others:
            o = o + sums[g2]
        outs.append(o)
    total = sums[0]
    for s in sums[1:]:
        total = total + s
    return (outs[0] if n_blk == 1 else jnp.concatenate(outs, axis=1)), total


def _scan_tri(kind):
    row = lax.broadcasted_iota(jnp.int32, (SB_SCAN, SB_SCAN), 0)
    col = lax.broadcasted_iota(jnp.int32, (SB_SCAN, SB_SCAN), 1)
    return {"after": row > col, "upto": row <= col, "before": row < col}[kind].astype(BF16)


def _sb_fwd(qkv, n_heads):
    t = qkv.shape[0]
    tq = _pick(t, SB_TILE)
    nq = t // tq
    scale = HEAD_DIM ** -0.5

    def kern(q_ref, k_ref, v_ref, o_ref, c_ref):
        i_blk = pl.program_id(1)
        row = lax.broadcasted_iota(jnp.int32, (tq, tq), 0)
        col = lax.broadcasted_iota(jnp.int32, (tq, tq), 1)
        causal = col < row
        after = _scan_tri("after")
        qb = q_ref[...]

        def tile(j_blk, c, acc, diag):
            r0 = pl.multiple_of(j_blk * tq, tq)
            kb = k_ref[pl.ds(r0, tq), :]
            vb = v_ref[pl.ds(r0, tq), :]
            z = lax.dot_general(qb, kb, NT, preferred_element_type=F32) * scale
            n = _neg_log_not_beta(z)
            if diag:
                n = jnp.where(causal, n, 0.0)
            later, total = _key_scan(n, after, True)
            a = jnp.exp(z - (n + later + c))
            if diag:
                a = jnp.where(causal, a, 0.0)
            acc = acc + jnp.dot(a.astype(BF16), vb, preferred_element_type=F32)
            return c + total, acc

        c, acc = tile(i_blk, jnp.zeros((tq, 1), F32), jnp.zeros((tq, HEAD_DIM), F32), True)
        c, acc = lax.fori_loop(0, i_blk, lambda jj, ca: tile(i_blk - 1 - jj, ca[0], ca[1], False),
                               (c, acc))
        o_ref[...] = acc
        c_ref[...] = jnp.broadcast_to(c, (tq, HEAD_DIM))

    h = n_heads
    return pl.pallas_call(
        kern, name="sb_fwd", grid=(h, nq),
        in_specs=[pl.BlockSpec((tq, HEAD_DIM), lambda hh, i: (i, hh)),
                  pl.BlockSpec((t, HEAD_DIM), lambda hh, i: (0, h + hh)),
                  pl.BlockSpec((t, HEAD_DIM), lambda hh, i: (0, 2 * h + hh))],
        out_specs=[pl.BlockSpec((tq, HEAD_DIM), lambda hh, i: (i, hh)),
                   pl.BlockSpec((tq, HEAD_DIM), lambda hh, i: (i, hh))],
        out_shape=[jax.ShapeDtypeStruct((t, h * HEAD_DIM), F32),
                   jax.ShapeDtypeStruct((t, h * HEAD_DIM), F32)],
        compiler_params=pltpu.CompilerParams(
            dimension_semantics=("arbitrary", "arbitrary"), vmem_limit_bytes=VMEM_LIMIT),
    )(qkv, qkv, qkv)


def _sb_bwd(qkv, do, ctot, n_heads):
    t = qkv.shape[0]
    tq = _pick(t, SB_TILE)
    nq = t // tq
    scale = HEAD_DIM ** -0.5

    def kern(q_ref, do_ref, c_ref, k_ref, v_ref, dq_ref, dk_ref, dv_ref, dk_acc, dv_acc):
        i_blk = pl.program_id(1)
        row = lax.broadcasted_iota(jnp.int32, (tq, tq), 0)
        col = lax.broadcasted_iota(jnp.int32, (tq, tq), 1)
        causal = col < row
        upto = _scan_tri("upto")
        before = _scan_tri("before")
        qb = q_ref[...]
        dob = do_ref[...]
        ctot_v = c_ref[...][:, 0:1]

        @pl.when(i_blk == 0)
        def _():
            dk_acc[...] = jnp.zeros_like(dk_acc)
            dv_acc[...] = jnp.zeros_like(dv_acc)

        def tile(j_blk, cl, pe, dq, diag):
            r0 = pl.multiple_of(j_blk * tq, tq)
            kb = k_ref[pl.ds(r0, tq), :]
            vb = v_ref[pl.ds(r0, tq), :]
            z = lax.dot_general(qb, kb, NT, preferred_element_type=F32) * scale
            n = _neg_log_not_beta(z)
            if diag:
                n = jnp.where(causal, n, 0.0)
            upto_s, n_total = _key_scan(n, upto, False)
            lb = z - n
            a = jnp.exp(lb - ((ctot_v - cl) - upto_s))
            if diag:
                a = jnp.where(causal, a, 0.0)
            da = lax.dot_general(dob, vb, NT, preferred_element_type=F32)
            e = da * a
            before_s, e_total = _key_scan(e, before, False)
            beta = jnp.exp(lb)
            dz = (e - beta * (e + (pe + before_s))) * scale
            if diag:
                dz = jnp.where(causal, dz, 0.0)
            dzb = dz.astype(BF16)
            dq = dq + jnp.dot(dzb, kb, preferred_element_type=F32)
            dk_acc[pl.ds(r0, tq), :] += lax.dot_general(dzb, qb, TN, preferred_element_type=F32)
            dv_acc[pl.ds(r0, tq), :] += lax.dot_general(a.astype(BF16), dob, TN,
                                                        preferred_element_type=F32)
            return cl + n_total, pe + e_total, dq

        zero = jnp.zeros((tq, 1), F32)
        cl, pe, dq = lax.fori_loop(0, i_blk, lambda j, s: tile(j, s[0], s[1], s[2], False),
                                   (zero, zero, jnp.zeros((tq, HEAD_DIM), F32)))
        cl, pe, dq = tile(i_blk, cl, pe, dq, True)
        dq_ref[...] = dq.astype(BF16)

        @pl.when(i_blk == nq - 1)
        def _():
            dk_ref[...] = dk_acc[...].astype(BF16)
            dv_ref[...] = dv_acc[...].astype(BF16)

    h = n_heads
    qspec = pl.BlockSpec((tq, HEAD_DIM), lambda hh, i: (i, hh))
    hspec = pl.BlockSpec((t, HEAD_DIM), lambda hh, i: (0, hh))
    out = jax.ShapeDtypeStruct((t, h * HEAD_DIM), BF16)
    return pl.pallas_call(
        kern, name="sb_bwd", grid=(h, nq),
        in_specs=[qspec, qspec, qspec,
                  pl.BlockSpec((t, HEAD_DIM), lambda hh, i: (0, h + hh)),
                  pl.BlockSpec((t, HEAD_DIM), lambda hh, i: (0, 2 * h + hh))],
        out_specs=[qspec, hspec, hspec],
        out_shape=[out, out, out],
        scratch_shapes=[pltpu.VMEM((t, HEAD_DIM), F32), pltpu.VMEM((t, HEAD_DIM), F32)],
        compiler_params=pltpu.CompilerParams(
            dimension_semantics=("arbitrary", "arbitrary"), vmem_limit_bytes=VMEM_LIMIT),
    )(qkv, do, ctot, qkv, qkv)


GDN_ROWS = 512
GDN_HEADS_PER_STEP = 2
TR_HEAD = 512


def _shift_rows(cur, halo, k, back):
    n = cur.shape[0]
    ext = jnp.concatenate([cur, halo], axis=0)
    return pltpu.roll(ext, k if back else n + 8 - k, 0)[:n]


def _conv_fwd(cur, halo, w):
    out = cur * w[GDN_CONV - 1:GDN_CONV, :]
    for i in range(GDN_CONV - 1):
        out = out + _shift_rows(cur, halo, GDN_CONV - 1 - i, True) * w[i:i + 1, :]
    return out


def _chunk_tri(n, upper):
    row = lax.broadcasted_iota(jnp.int32, (n, n), 0)
    col = lax.broadcasted_iota(jnp.int32, (n, n), 1)
    same = (row // GDN_CHUNK) == (col // GDN_CHUNK)
    tri = (col >= row) if upper else (col <= row)
    return jnp.logical_and(same, tri).astype(BF16)


def _lane_pick(x, lane):
    idx = lax.broadcasted_iota(jnp.int32, x.shape, 1)
    return jnp.sum(jnp.where(idx == lane, x, 0.0), axis=1, keepdims=True)


def _softplus(x):
    y = jnp.exp(-jnp.abs(x))
    u = 1.0 + y
    log1p = jnp.where(u == 1.0, y, jnp.log(u) * (y / jnp.where(u == 1.0, 1.0, u - 1.0)))
    return jnp.maximum(x, 0.0) + log1p


def _gdn_specs(t, tr, h, proj_seg0):
    def slab(seg):
        return ((tr, HEAD_DIM), lambda i, hh: (i, seg * h + hh))

    def halo_before(seg):
        return ((8, HEAD_DIM), lambda i, hh: (jnp.maximum(i * (tr // 8) - 1, 0), seg * h + hh))

    return slab, halo_before


def _gdn_pre_fwd(proj_b, proj_c, conv_w8, ab, n_heads):
    t = proj_b.shape[0]
    h = n_heads
    d = h * HEAD_DIM
    tr = _pick(t, TR_HEAD)
    slab, halo = _gdn_specs(t, tr, h, 1)
    scale = HEAD_DIM ** -0.5

    def body(idx, q_c, q_h, k_c, k_h, v_c, v_h, wq, wk, wv, gbga, ab_v):
        i, hh = idx
        live = (i > 0).astype(F32)
        outs = []
        for cur, hal, w, kind in ((q_c, q_h, wq, "q"), (k_c, k_h, wk, "k"), (v_c, v_h, wv, "v")):
            u = _conv_fwd(cur, hal * live, w)
            s = u * _sigmoid(u)
            if kind != "v":
                s = s * lax.rsqrt(jnp.sum(s * s, axis=1, keepdims=True) + L2_EPS)
            if kind == "q":
                s = s * scale
            outs.append(s)
        beta = _sigmoid(_lane_pick(gbga, hh))
        a_log = _lane_pick(ab_v[0:1, :], hh)
        dt = _lane_pick(ab_v[1:2, :], hh)
        g = -jnp.exp(a_log) * _softplus(_lane_pick(gbga, h + hh) + dt)
        g_rep = jnp.broadcast_to(g, (tr, HEAD_DIM))
        big_g = _cdot3(_chunk_tri(tr, False), g_rep)
        return outs + [jnp.broadcast_to(beta, (tr, HEAD_DIM)), big_g]

    wspec = lambda seg: ((8, HEAD_DIM), lambda i, hh: (0, seg * h + hh))
    out = ((t, d), F32, (tr, HEAD_DIM), lambda i, hh: (i, hh), "set")
    return _rowcall(
        "gdn_pre_fwd", body, (t // tr, h),
        [(proj_b,) + slab(1), (proj_b,) + halo(1), (proj_b,) + slab(2), (proj_b,) + halo(2),
         (proj_b,) + slab(3), (proj_b,) + halo(3),
         (conv_w8,) + wspec(0), (conv_w8,) + wspec(1), (conv_w8,) + wspec(2),
         (proj_c, (tr, LANES), lambda i, hh: (i, 0)), _full(ab)],
        [out] * 5)


def _gdn_consts():
    row = lax.broadcasted_iota(jnp.int32, (GDN_CHUNK, GDN_CHUNK), 0)
    col = lax.broadcasted_iota(jnp.int32, (GDN_CHUNK, GDN_CHUNK), 1)
    lane = lax.broadcasted_iota(jnp.int32, (GDN_CHUNK, HEAD_DIM), 1)
    return row > col, row >= col, (row == col).astype(F32), (lane == 0).astype(BF16)


def _gdn_local(q, k, v, be, ge, consts):
    lower, tril, eye, sel = consts
    kb_ = k * be
    vb_ = v * be
    e_g = jnp.exp(ge)
    kg = kb_ * e_g
    p1, p2, p3 = _split3(ge)
    g_i = _bdot(p1, sel, NT) + _bdot(p2, sel, NT) + _bdot(p3, sel, NT)
    g_j = _bdot(sel, p1, NT) + _bdot(sel, p2, NT) + _bdot(sel, p3, NT)
    dec = jnp.where(tril, jnp.exp(jnp.minimum(g_i - g_j, 0.0)), 0.0)
    kk = _bdot(kb_, k, NT)
    qk = _bdot(q, k, NT)
    g_last = jnp.min(ge, axis=0, keepdims=True)
    kdec_f = jnp.exp(g_last - ge)
    return dict(kb=kb_, vb=vb_, e_g=e_g, kg=kg, dec=dec, kk=kk, qk=qk, kdec_f=kdec_f,
                k_dec=k * kdec_f, q_dec=q * e_g, gamma=jnp.exp(g_last),
                intra=jnp.where(tril, qk * dec, 0.0))


def _wy_lower_t(k, be, ge, sel):
    row = lax.broadcasted_iota(jnp.int32, (GDN_CHUNK, GDN_CHUNK), 0)
    col = lax.broadcasted_iota(jnp.int32, (GDN_CHUNK, GDN_CHUNK), 1)
    p1, p2, p3 = _split3(ge)
    g_row = _bdot(p1, sel, NT) + _bdot(p2, sel, NT) + _bdot(p3, sel, NT)
    g_col = _bdot(sel, p1, NT) + _bdot(sel, p2, NT) + _bdot(sel, p3, NT)
    dec_t = jnp.exp(jnp.minimum(g_col - g_row, 0.0))
    return jnp.where(col > row, _bdot(k, k * be, NT) * dec_t, 0.0)


def _unit_lower_inverse(lw_t):
    n = lw_t.shape[0]
    n_slab = GDN_CHUNK // 8
    row = lax.broadcasted_iota(jnp.int32, (n, 8, GDN_CHUNK), 1)
    col = lax.broadcasted_iota(jnp.int32, (n, 8, GDN_CHUNK), 2)
    unit = lax.broadcasted_iota(jnp.int32, (1, 1, GDN_CHUNK), 2)
    lw = [lw_t[:, 8 * g:8 * g + 8, :] for g in range(n_slab)]
    inv = [(col == row + 8 * g).astype(F32) for g in range(n_slab)]
    for i in range(1, GDN_CHUNK):
        acc = lw[0][:, :, i:i + 1] * inv[0]
        for g in range(1, (i - 1) // 8 + 1):
            acc = acc + lw[g][:, :, i:i + 1] * inv[g]
        new_row = (unit == i).astype(F32) - jnp.sum(acc, axis=1, keepdims=True)
        inv[i // 8] = jnp.where(row == i % 8, new_row, inv[i // 8])
    return jnp.concatenate(inv, axis=1)


def _gdn_fwd(q, k, v, be, ge, n_heads):
    t = q.shape[0]
    h = n_heads
    hg = GDN_HEADS_PER_STEP
    tr = _pick(t, GDN_ROWS)
    nc = tr // GDN_CHUNK

    def kern(q_ref, k_ref, v_ref, b_ref, g_ref, o_ref, s_ref, tm_ref, state):
        consts = _gdn_consts()
        lower, tril, eye, sel = consts
        lanes = [pl.ds(hs * HEAD_DIM, HEAD_DIM) for hs in range(hg)]

        @pl.when(pl.program_id(1) == 0)
        def _():
            state[...] = jnp.zeros_like(state)

        lw_t = []
        for hs in range(hg):
            for ci in range(nc):
                rows = pl.ds(ci * GDN_CHUNK, GDN_CHUNK)
                lw_t.append(_wy_lower_t(k_ref[rows, lanes[hs]], b_ref[rows, lanes[hs]],
                                        g_ref[rows, lanes[hs]], sel))
        t_all = _unit_lower_inverse(jnp.stack(lw_t))
        for hs in range(hg):
            for ci in range(nc):
                tm_ref[hs, pl.ds(ci * GDN_CHUNK, GDN_CHUNK), :] = t_all[hs * nc + ci]

        def chunk(ci, carry):
            rows = pl.ds(pl.multiple_of(ci * GDN_CHUNK, GDN_CHUNK), GDN_CHUNK)
            results = []
            for hs in range(hg):
                ln = lanes[hs]
                loc = _gdn_local(q_ref[rows, ln], k_ref[rows, ln], v_ref[rows, ln], b_ref[rows, ln],
                                 g_ref[rows, ln], consts)
                t_mat = tm_ref[hs, rows, :]
                u = _bdot(t_mat, loc["vb"])
                w = _bdot(t_mat, loc["kg"])
                s = state[hs]
                v_new = u - _bdot(w, s)
                results.append((s, _bdot(loc["q_dec"], s) + _bdot(loc["intra"], v_new),
                                s * loc["gamma"] + _bdot(loc["k_dec"], v_new, TN)))
            for hs in range(hg):
                s_ref[hs, ci] = results[hs][0]
                o_ref[rows, lanes[hs]] = results[hs][1]
                state[hs] = results[hs][2]
            return carry

        lax.fori_loop(0, nc, chunk, 0)

    slab = pl.BlockSpec((tr, hg * HEAD_DIM), lambda hp, j: (j, hp))
    return pl.pallas_call(
        kern, name="gdn_fwd", grid=(h // hg, t // tr),
        in_specs=[slab] * 5,
        out_specs=[slab,
                   pl.BlockSpec((hg, nc, HEAD_DIM, HEAD_DIM), lambda hp, j: (hp, j, 0, 0)),
                   pl.BlockSpec((hg, tr, GDN_CHUNK), lambda hp, j: (hp, j, 0))],
        out_shape=[jax.ShapeDtypeStruct((t, h * HEAD_DIM), F32),
                   jax.ShapeDtypeStruct((h, t // GDN_CHUNK, HEAD_DIM, HEAD_DIM), F32),
                   jax.ShapeDtypeStruct((h, t, GDN_CHUNK), F32)],
        scratch_shapes=[pltpu.VMEM((hg, HEAD_DIM, HEAD_DIM), F32)],
        compiler_params=pltpu.CompilerParams(
            dimension_semantics=("arbitrary", "arbitrary"), vmem_limit_bytes=VMEM_LIMIT),
    )(q, k, v, be, ge)


def _gdn_bwd(q, k, v, be, ge, do, states, tms, n_heads):
    t = q.shape[0]
    h = n_heads
    hg = GDN_HEADS_PER_STEP
    tr = _pick(t, GDN_ROWS)
    nc = tr // GDN_CHUNK
    nj = t // tr

    def kern(q_ref, k_ref, v_ref, b_ref, g_ref, do_ref, s_ref, tm_ref,
             dq_ref, dk_ref, dv_ref, db_ref, dg_ref, dstate):
        consts = _gdn_consts()
        lower, tril, eye, sel = consts
        last_row = lax.broadcasted_iota(jnp.int32, (GDN_CHUNK, HEAD_DIM), 0) == GDN_CHUNK - 1

        @pl.when(pl.program_id(1) == 0)
        def _():
            dstate[...] = jnp.zeros_like(dstate)

        def lsum(x):
            return jnp.sum(x, axis=1, keepdims=True)

        def chunk(cc, carry):
            ci = nc - 1 - cc
            rows = pl.ds(pl.multiple_of(ci * GDN_CHUNK, GDN_CHUNK), GDN_CHUNK)
            results = [one_head(ci, hs) for hs in range(hg)]
            for hs in range(hg):
                ln = pl.ds(hs * HEAD_DIM, HEAD_DIM)
                dstate[hs] = results[hs][0]
                for ref, val in zip((dq_ref, dk_ref, dv_ref, db_ref, dg_ref), results[hs][1:]):
                    ref[rows, ln] = val
            return carry

        def one_head(ci, hs):
            rows = pl.ds(pl.multiple_of(ci * GDN_CHUNK, GDN_CHUNK), GDN_CHUNK)
            ln = pl.ds(hs * HEAD_DIM, HEAD_DIM)
            qv, kv, vv, bev = q_ref[rows, ln], k_ref[rows, ln], v_ref[rows, ln], b_ref[rows, ln]
            loc = _gdn_local(qv, kv, vv, bev, g_ref[rows, ln], consts)
            t_mat = tm_ref[hs, rows, :]
            s = s_ref[hs, ci]
            d_o = do_ref[rows, ln]
            d_s = dstate[hs]
            u = _bdot(t_mat, loc["vb"])
            w = _bdot(t_mat, loc["kg"])
            v_new = u - _bdot(w, s)
            dv_new = _bdot(loc["intra"], d_o, TN) + _bdot(loc["k_dec"], d_s)
            d_intra = jnp.where(tril, _bdot(d_o, v_new, NT), 0.0)
            dq_dec = _bdot(d_o, s, NT)
            dk_dec = _bdot(v_new, d_s, NT)
            dgamma = jnp.sum(lsum(d_s * s), axis=0, keepdims=True)
            dw = -_bdot(dv_new, s, NT)
            d_s_new = (_bdot(loc["q_dec"], d_o, TN) + loc["gamma"] * d_s
                       - _bdot(w, dv_new, TN))
            dtm = _bdot(dv_new, loc["vb"], NT) + _bdot(dw, loc["kg"], NT)
            dvb = _bdot(t_mat, dv_new, TN)
            dkg = _bdot(t_mat, dw, TN)
            dlw = jnp.where(lower, -_dot6(t_mat, _dot6(dtm, t_mat, NT), TN), 0.0)
            dkk = dlw * loc["dec"]
            dqk = d_intra * loc["dec"]
            ddec = dlw * loc["kk"] + d_intra * loc["qk"]
            dkb = _bdot(dkk, kv) + dkg * loc["e_g"]
            dk = (_bdot(dkk, loc["kb"], TN) + _bdot(dqk, qv, TN) + dk_dec * loc["kdec_f"]
                  + dkb * bev)
            dq = _bdot(dqk, kv) + dq_dec * loc["e_g"]
            dgd = ddec * loc["dec"]
            r_kdec = lsum(dk_dec * loc["k_dec"])
            col_sums = lsum(_dot2(dgd, sel, TN))
            d_big_g = (lsum(dgd) - col_sums
                       + lsum(dq_dec * loc["q_dec"]) - r_kdec + lsum(dkg * loc["kg"]))
            d_last = jnp.sum(r_kdec, axis=0, keepdims=True) + dgamma * loc["gamma"][:, 0:1]
            d_big_g = jnp.broadcast_to(d_big_g, (GDN_CHUNK, HEAD_DIM)) + jnp.where(last_row, d_last,
                                                                                  0.0)
            d_beta = jnp.broadcast_to(lsum(dkb * kv) + lsum(dvb * vv), (GDN_CHUNK, HEAD_DIM))
            return d_s_new, dq, dk, dvb * bev, d_beta, d_big_g

        lax.fori_loop(0, nc, chunk, 0)

    slab = pl.BlockSpec((tr, hg * HEAD_DIM), lambda hp, j: (nj - 1 - j, hp))
    out = jax.ShapeDtypeStruct((t, h * HEAD_DIM), F32)
    return pl.pallas_call(
        kern, name="gdn_bwd", grid=(h // hg, nj),
        in_specs=[slab] * 6 + [
            pl.BlockSpec((hg, nc, HEAD_DIM, HEAD_DIM), lambda hp, j: (hp, nj - 1 - j, 0, 0)),
            pl.BlockSpec((hg, tr, GDN_CHUNK), lambda hp, j: (hp, nj - 1 - j, 0))],
        out_specs=[slab] * 5,
        out_shape=[out] * 5,
        scratch_shapes=[pltpu.VMEM((hg, HEAD_DIM, HEAD_DIM), F32)],
        compiler_params=pltpu.CompilerParams(
            dimension_semantics=("arbitrary", "arbitrary"), vmem_limit_bytes=VMEM_LIMIT),
    )(q, k, v, be, ge, do, states, tms)


def _gdn_pre_bwd(proj_b, proj_c, conv_w8, ab, dq, dk, dv, dbe, dge, n_heads):
    t = proj_b.shape[0]
    h = n_heads
    d = h * HEAD_DIM
    tr = _pick(t, TR_HEAD)
    slab, halo = _gdn_specs(t, tr, h, 1)
    scale = HEAD_DIM ** -0.5

    def body(idx, q_c, q_h, k_c, k_h, v_c, v_h, wq, wk, wv, gbga, ab_v, dq_v, dk_v, dv_v, dbe_v,
             dge_v):
        i, hh = idx
        live = (i > 0).astype(F32)
        outs = []
        for cur, hal, w, dy, kind in ((q_c, q_h, wq, dq_v, "q"), (k_c, k_h, wk, dk_v, "k"),
                                      (v_c, v_h, wv, dv_v, "v")):
            u = _conv_fwd(cur, hal * live, w)
            sg = _sigmoid(u)
            if kind == "v":
                ds = dy
            else:
                s = u * sg
                r = lax.rsqrt(jnp.sum(s * s, axis=1, keepdims=True) + L2_EPS)
                y = s * r
                if kind == "q":
                    dy = dy * scale
                ds = r * (dy - y * jnp.sum(dy * y, axis=1, keepdims=True))
            outs.append(ds * (sg * (1.0 + u * (1.0 - sg))))
        lane = lax.broadcasted_iota(jnp.int32, (tr, LANES), 1)
        lane1 = lax.broadcasted_iota(jnp.int32, (1, LANES), 1)
        beta = _sigmoid(_lane_pick(gbga, hh))
        a_neg = -jnp.exp(_lane_pick(ab_v[0:1, :], hh))
        xg = _lane_pick(gbga, h + hh) + _lane_pick(ab_v[1:2, :], hh)
        g = a_neg * _softplus(xg)
        dgb = dbe_v * (beta * (1.0 - beta))
        dg = _cdot3(_chunk_tri(tr, True), dge_v)
        dga = dg * (a_neg * _sigmoid(xg))
        d_gates = jnp.where(lane == hh, dgb, 0.0) + jnp.where(lane == h + hh, dga, 0.0)
        d_ab = (jnp.where(lane1 == hh, jnp.sum(dg * g, axis=0, keepdims=True), 0.0)
                + jnp.where(lane1 == h + hh, jnp.sum(dga, axis=0, keepdims=True), 0.0))
        return outs + [d_gates, d_ab]

    wspec = lambda seg: ((8, HEAD_DIM), lambda i, hh: (0, seg * h + hh))
    hs = ((tr, HEAD_DIM), lambda i, hh: (i, hh))
    out = ((t, d), F32) + hs + ("set",)
    return _rowcall(
        "gdn_pre_bwd", body, (t // tr, h),
        [(proj_b,) + slab(1), (proj_b,) + halo(1), (proj_b,) + slab(2), (proj_b,) + halo(2),
         (proj_b,) + slab(3), (proj_b,) + halo(3),
         (conv_w8,) + wspec(0), (conv_w8,) + wspec(1), (conv_w8,) + wspec(2),
         (proj_c, (tr, LANES), lambda i, hh: (i, 0)), _full(ab),
         (dq,) + hs, (dk,) + hs, (dv,) + hs, (dbe,) + hs, (dge,) + hs],
        [out, out, out,
         ((t, LANES), F32, (tr, LANES), lambda i, hh: (i, 0), "acc_inner"),
         ((1, LANES), F32, (1, LANES), lambda i, hh: (0, 0), "acc_all")])


def _conv_bwd(proj_b, conv_w8, du_q, du_k, du_v, n_heads):
    t = proj_b.shape[0]
    h = n_heads
    d = h * HEAD_DIM
    tr = _pick(t, TR_HEAD)
    ni = t // tr

    def body(idx, q_c, q_h, k_c, k_h, v_c, v_h, wq, wk, wv, uq, uq_n, uk, uk_n, uv, uv_n):
        hh, i = idx
        live_b = (i > 0).astype(F32)
        live_a = (i < ni - 1).astype(F32)
        d_ins, d_ws = [], []
        for cur, hal, w, du, du_n in ((q_c, q_h, wq, uq, uq_n), (k_c, k_h, wk, uk, uk_n),
                                      (v_c, v_h, wv, uv, uv_n)):
            hal = hal * live_b
            du_n = du_n * live_a
            d_in = du * w[GDN_CONV - 1:GDN_CONV, :]
            rows = [jnp.sum(du * cur, axis=0, keepdims=True)]
            for i_tap in range(GDN_CONV - 2, -1, -1):
                kshift = GDN_CONV - 1 - i_tap
                d_in = d_in + _shift_rows(du, du_n, kshift, False) * w[i_tap:i_tap + 1, :]
                rows.insert(0, jnp.sum(du * _shift_rows(cur, hal, kshift, True), axis=0,
                                       keepdims=True))
            d_ins.append(d_in)
            tap = lax.broadcasted_iota(jnp.int32, (8, HEAD_DIM), 0)
            d_w = jnp.zeros((8, HEAD_DIM), F32)
            for i_tap in range(GDN_CONV):
                d_w = d_w + jnp.where(tap == i_tap, rows[i_tap], 0.0)
            d_ws.append(d_w)
        return d_ins + d_ws

    def slab(seg):
        return ((tr, HEAD_DIM), lambda hh, i: (i, seg * h + hh))

    def halo_b(seg):
        return ((8, HEAD_DIM), lambda hh, i: (jnp.maximum(i * (tr // 8) - 1, 0), seg * h + hh))

    hs = ((tr, HEAD_DIM), lambda hh, i: (i, hh))
    halo_a = ((8, HEAD_DIM), lambda hh, i: (jnp.minimum((i + 1) * (tr // 8), t // 8 - 1), hh))
    wspec = lambda seg: ((8, HEAD_DIM), lambda hh, i: (0, seg * h + hh))
    wout = ((8, d), F32, (8, HEAD_DIM), lambda hh, i: (0, hh), "acc_inner")
    out = ((t, d), BF16) + hs + ("set",)
    res = _rowcall(
        "conv_bwd", body, (h, ni),
        [(proj_b,) + slab(1), (proj_b,) + halo_b(1), (proj_b,) + slab(2), (proj_b,) + halo_b(2),
         (proj_b,) + slab(3), (proj_b,) + halo_b(3),
         (conv_w8,) + wspec(0), (conv_w8,) + wspec(1), (conv_w8,) + wspec(2),
         (du_q,) + hs, (du_q,) + halo_a, (du_k,) + hs, (du_k,) + halo_a, (du_v,) + hs,
         (du_v,) + halo_a],
        [out, out, out, wout, wout, wout])
    return res


def _gdn_post_fwd(o, proj_b, gnw, n_heads):
    t, d = o.shape
    h = n_heads
    tr = _pick(t, TR_HEAD)

    def body(idx, o_v, z, w):
        rstd = lax.rsqrt(jnp.mean(o_v * o_v, axis=1, keepdims=True) + NORM_EPS)
        return ((o_v * rstd) * w * (z * _sigmoid(z)),)

    hs = ((tr, HEAD_DIM), lambda i, hh: (i, hh))
    return _rowcall("gdn_post_fwd", body, (t // tr, h),
                    [(o,) + hs, (proj_b, (tr, HEAD_DIM), lambda i, hh: (i, 4 * h + hh)), _full(gnw)],
                    [((t, d), BF16) + hs + ("set",)])[0]


def _gdn_post_bwd(do_gdn, o, proj_b, gnw, n_heads):
    t, d = o.shape
    h = n_heads
    tr = _pick(t, TR_HEAD)

    def body(idx, dog, o_v, z, w):
        rstd = lax.rsqrt(jnp.mean(o_v * o_v, axis=1, keepdims=True) + NORM_EPS)
        n = o_v * rstd
        sg = _sigmoid(z)
        don = dog * (z * sg)
        dz = dog * (n * w) * (sg * (1.0 + z * (1.0 - sg)))
        dn = don * w
        d_o = rstd * (dn - n * jnp.mean(dn * n, axis=1, keepdims=True))
        return d_o, dz, jnp.sum(don * n, axis=0, keepdims=True)

    hs = ((tr, HEAD_DIM), lambda i, hh: (i, hh))
    return _rowcall("gdn_post_bwd", body, (t // tr, h),
                    [(do_gdn,) + hs, (o,) + hs,
                     (proj_b, (tr, HEAD_DIM), lambda i, hh: (i, 4 * h + hh)), _full(gnw)],
                    [((t, d), F32) + hs + ("set",), ((t, d), BF16) + hs + ("set",),
                     ((1, HEAD_DIM), F32, (1, HEAD_DIM), lambda i, hh: (0, 0), "acc_all")])


def _seg(arr, tr, d, seg):
    return (arr, (tr, d), lambda i: (i, seg))


def _sb_post_fwd(o_raw, proj_b):
    t, d = o_raw.shape
    tr = _pick(t, TR_WIDE)

    def body(idx, o_v, z):
        return (o_v * (z * _sigmoid(z)),)

    return _rowcall("sb_post_fwd", body, (t // tr,), [_seg(o_raw, tr, d, 0), _seg(proj_b, tr, d, 0)],
                    [((t, d), BF16, (tr, d), lambda i: (i, 0), "set")])[0]


def _sb_post_bwd(do_sb, o_raw, proj_b):
    t, d = o_raw.shape
    tr = _pick(t, TR_WIDE)

    def body(idx, dos, o_v, z):
        sg = _sigmoid(z)
        return dos * (z * sg), dos * o_v * (sg * (1.0 + z * (1.0 - sg)))

    out = ((t, d), BF16, (tr, d), lambda i: (i, 0), "set")
    return _rowcall("sb_post_bwd", body, (t // tr,),
                    [_seg(do_sb, tr, d, 0), _seg(o_raw, tr, d, 0), _seg(proj_b, tr, d, 0)],
                    [out, out])


def _merge_fwd(proj_b, p_sb, p_gdn):
    t, d = p_sb.shape
    tr = _pick(t, TR_WIDE)

    def body(idx, m_sb, m_gdn, ps, pg):
        return (_sigmoid(m_sb) * ps + _sigmoid(m_gdn) * pg,)

    return _rowcall("merge_fwd", body, (t // tr,),
                    [_seg(proj_b, tr, d, 5), _seg(proj_b, tr, d, 6), _seg(p_sb, tr, d, 0),
                     _seg(p_gdn, tr, d, 0)],
                    [((t, d), BF16, (tr, d), lambda i: (i, 0), "set")])[0]


def _merge_bwd(dy, proj_b, p_sb, p_gdn):
    t, d = p_sb.shape
    tr = _pick(t, TR_WIDE)

    def body(idx, dy_v, m_sb, m_gdn, ps, pg):
        s1 = _sigmoid(m_sb)
        s2 = _sigmoid(m_gdn)
        return s1 * dy_v, s2 * dy_v, dy_v * ps * (s1 * (1.0 - s1)), dy_v * pg * (s2 * (1.0 - s2))

    out = ((t, d), BF16, (tr, d), lambda i: (i, 0), "set")
    return _rowcall("merge_bwd", body, (t // tr,),
                    [_seg(dy, tr, d, 0), _seg(proj_b, tr, d, 5), _seg(proj_b, tr, d, 6),
                     _seg(p_sb, tr, d, 0), _seg(p_gdn, tr, d, 0)],
                    [out] * 4)


def _tail(x, r, target, gate, final_w):
    t, d = x.shape
    tr = _pick(t, TR_WIDE)

    def body(idx, x_v, r_v, tg, gt, fw):
        x2 = x_v + gt * r_v
        rstd = lax.rsqrt(jnp.mean(x2 * x2, axis=1, keepdims=True) + NORM_EPS)
        n = x2 * rstd
        diff = n * fw - tg
        loss = 0.5 * jnp.sum(jnp.mean(diff * diff, axis=1, keepdims=True), axis=0, keepdims=True)
        dout = diff * (1.0 / d)
        dn = dout * fw
        dx2 = rstd * (dn - n * jnp.mean(dn * n, axis=1, keepdims=True))
        return (dx2, gt * dx2, jnp.sum(dout * n, axis=0, keepdims=True),
                jnp.sum(dx2 * r_v, axis=0, keepdims=True), jnp.broadcast_to(loss, (1, LANES)))

    rb = ((tr, d), lambda i: (i, 0))
    vec = ((1, d), F32, (1, d), lambda i: (0, 0), "acc_all")
    return _rowcall("tail", body, (t // tr,),
                    [(x,) + rb, (r,) + rb, (target,) + rb, _full(gate), _full(final_w)],
                    [((t, d), F32) + rb + ("set",), ((t, d), BF16) + rb + ("set",), vec, vec,
                     ((1, LANES), F32, (1, LANES), lambda i: (0, 0), "acc_all")])


def _pad_to(a, rows, cols):
    return jnp.pad(a, ((0, rows - a.shape[0]), (0, cols - a.shape[1])))


def kernel(x, c, w_ada, b_ada, norm_w, w_in, gdn_conv_w, gdn_a_log, gdn_dt_bias, gdn_norm_w, w_proj_sb, w_proj_gdn, w_out, final_norm_w, loss_target, m_w_ada, m_b_ada, m_norm_w, m_w_in, m_gdn_conv_w, m_gdn_a_log, m_gdn_dt_bias, m_gdn_norm_w, m_w_proj_sb, m_w_proj_gdn, m_w_out, m_final_norm_w, v_w_ada, v_b_ada, v_norm_w, v_w_in, v_gdn_conv_w, v_gdn_a_log, v_gdn_dt_bias, v_gdn_norm_w, v_w_proj_sb, v_w_proj_gdn, v_w_out, v_final_norm_w):
    t, d = x.shape[1], x.shape[2]
    h = d // HEAD_DIM
    me = 4 * lax.axis_index("x") + 2 * lax.axis_index("y") + lax.axis_index("c")
    x2d = x[0]
    tgt = loss_target[0]
    ada_cols = w_ada.shape[2]
    in_cols = w_in.shape[2]
    rows_p = w_out.shape[1]

    w_in_all = _exchange("gather_w_in", w_in[0].astype(BF16), True)
    w_in_full = jnp.transpose(w_in_all, (1, 0, 2)).reshape(d, N_DEV * in_cols)
    w_main = jnp.concatenate([w_in_full[:, :8 * d], w_in_full[:, 8 * d + 2 * h:]], axis=1)
    w_g = _pad_to(w_in_full[:, 8 * d:8 * d + 2 * h], d, LANES)
    w_main_t = w_main.T
    w_g_t = w_g.T
    w_sq = jnp.stack([w_proj_sb[0], w_proj_gdn[0], w_out[0]]).astype(BF16)
    w_sq_all = _exchange("gather_w_sq", w_sq, True)
    w_sq_full = jnp.transpose(w_sq_all, (1, 0, 2, 3)).reshape(3, d, d)
    wp_sb, wp_gdn, wo = w_sq_full[0], w_sq_full[1], w_sq_full[2]
    conv_all = _exchange("gather_conv", _pad_to(gdn_conv_w[0], 8, gdn_conv_w.shape[2]), True)
    conv_w8 = jnp.transpose(conv_all, (1, 0, 2)).reshape(8, 3 * d)
    c_all = _exchange("gather_c", _pad_to(c, 8, d), True)[:, 0, :]

    sc_all = c_all * _sigmoid(c_all)
    mod_part = _matmul("ada_fwd", _pad_to(sc_all, 16, d).astype(BF16), w_ada[0].astype(BF16), F32)
    mod_part = mod_part[:N_DEV] + lax.dynamic_slice(b_ada, (0, me * ada_cols), (1, ada_cols))
    mod_rows = _exchange("a2a_mod", _pad_to(mod_part, 8, ada_cols).reshape(N_DEV, 1, ada_cols)
                         * jnp.ones((1, 8, 1), F32), False)
    mod = mod_rows[:, 0, :].reshape(1, 3 * d)
    shift, scale, gate = mod[:, :d], mod[:, d:2 * d], mod[:, 2 * d:]

    hmod = _norm_mod_fwd(x2d, shift, scale, norm_w)
    proj_a = _matmul("in_proj_a", hmod, w_main, BF16, n_cols=3 * d, col0=0)
    proj_b = _matmul("in_proj_b", hmod, w_main, F32, n_cols=7 * d, col0=3 * d)
    proj_c = _matmul("in_proj_c", hmod, w_g, F32)
    o_sb_raw, ctot = _sb_fwd(proj_a, h)
    o_sb = _sb_post_fwd(o_sb_raw, proj_b)
    ab = _pad_to(jnp.concatenate([gdn_a_log, gdn_dt_bias], axis=0), 8, LANES)
    gq, gk, gv, g_beta, g_cum = _gdn_pre_fwd(proj_b, proj_c, conv_w8, ab, h)
    o_gdn_raw, states, tms = _gdn_fwd(gq, gk, gv, g_beta, g_cum, h)
    o_gdn = _gdn_post_fwd(o_gdn_raw, proj_b, gdn_norm_w, h)
    p_sb = _matmul("proj_sb", o_sb, wp_sb, F32)
    p_gdn = _matmul("proj_gdn", o_gdn, wp_gdn, F32)
    y = _merge_fwd(proj_b, p_sb, p_gdn)
    r = _matmul("out_proj", y, wo, F32)
    dx2, dr, d_final_w, d_gate, loss_part = _tail(x2d, r, tgt, gate, final_norm_w.reshape(1, d))

    dy = _matmul("d_out_proj", dr, wo.T, F32)
    dw_out = _matmul("dw_out", y.T, dr, BF16)
    dp_sb, dp_gdn, dm_sb, dm_gdn = _merge_bwd(dy, proj_b, p_sb, p_gdn)
    do_sb = _matmul("d_proj_sb", dp_sb, wp_sb.T, F32)
    dw_p_sb = _matmul("dw_proj_sb", o_sb.T, dp_sb, BF16)
    do_gdn = _matmul("d_proj_gdn", dp_gdn, wp_gdn.T, F32)
    dw_p_gdn = _matmul("dw_proj_gdn", o_gdn.T, dp_gdn, BF16)
    do_sb_raw, d_sbz = _sb_post_bwd(do_sb, o_sb_raw, proj_b)
    d_sbq, d_sbk, d_sbv = _sb_bwd(proj_a, do_sb_raw, ctot, h)
    d_o_gdn, d_gz, d_gnw = _gdn_post_bwd(do_gdn, o_gdn_raw, proj_b, gdn_norm_w, h)
    dgq, dgk, dgv, dgbe, dgcum = _gdn_bwd(gq, gk, gv, g_beta, g_cum, d_o_gdn, states, tms, h)
    du_q, du_k, du_v, d_gates, d_ab = _gdn_pre_bwd(proj_b, proj_c, conv_w8, ab, dgq, dgk, dgv,
                                                  dgbe, dgcum, h)
    d_gq, d_gk, d_gv, dcw_q, dcw_k, dcw_v = _conv_bwd(proj_b, conv_w8, du_q, du_k, du_v, h)
    dproj = jnp.concatenate([d_sbq, d_sbk, d_sbv, d_sbz, d_gq, d_gk, d_gv, d_gz, dm_sb, dm_gdn],
                            axis=1)
    d_gates_b = d_gates.astype(BF16)
    dh_a = _matmul("d_in_proj", dproj, w_main_t, F32)
    dh_b = _matmul("d_in_proj_g", d_gates_b, w_g_t, F32)
    hmod_t = hmod.T
    dw_main = _matmul("dw_in", hmod_t, dproj, BF16)
    dw_g = _matmul("dw_in_g", hmod_t, d_gates_b, BF16)
    grad_x, d_shift, d_scale, d_norm_w = _norm_mod_bwd(dh_a, dh_b, x2d, dx2, scale, norm_w)

    dmod = jnp.concatenate([d_shift, d_scale, d_gate], axis=1)
    small = jnp.concatenate([dmod, d_norm_w, d_final_w, d_ab[:, :h], d_ab[:, h:2 * h], d_gnw,
                             loss_part], axis=1)
    n_small = small.shape[1]
    small_all = _exchange("gather_small", _pad_to(small, 8, n_small), True)[:, 0:1, :]
    small_w = jnp.concatenate([b_ada, norm_w, final_norm_w.reshape(1, d), gdn_a_log, gdn_dt_bias,
                               gdn_norm_w, jnp.zeros((1, LANES), F32)], axis=1)
    small_m = jnp.concatenate([m_b_ada, m_norm_w, m_final_norm_w.reshape(1, d), m_gdn_a_log,
                               m_gdn_dt_bias, m_gdn_norm_w, jnp.zeros((1, LANES), F32)], axis=1)
    small_v = jnp.concatenate([v_b_ada, v_norm_w, v_final_norm_w.reshape(1, d), v_gdn_a_log,
                               v_gdn_dt_bias, v_gdn_norm_w, jnp.ones((1, LANES), F32)], axis=1)
    s_g, s_d, s_m, s_v = _adamw("adamw_small", small_all, small_w, small_m, small_v)
    cuts = [3 * d, 4 * d, 5 * d, 5 * d + h, 5 * d + 2 * h, 5 * d + 2 * h + HEAD_DIM]

    def split_small(a):
        b, nw, fw, al, dtb, gn, _ = jnp.split(a, cuts, axis=1)
        return b, nw, fw.reshape(d), al, dtb, gn

    loss = s_g[0, cuts[-1]]

    dmod_all = small_all[:, 0, :3 * d]
    dmod_mine = lax.dynamic_slice(dmod_all, (0, me * ada_cols), (N_DEV, ada_cols))
    dw_ada = _matmul("dw_ada", _pad_to(sc_all.T, d, LANES).astype(BF16),
                     _pad_to(dmod_mine, LANES, ada_cols).astype(BF16), F32)
    ada = _adamw("adamw_ada", dw_ada[None], w_ada[0], m_w_ada[0], v_w_ada[0])

    dw_in_full = jnp.concatenate([dw_main[:, :8 * d], dw_g[:, :2 * h], dw_main[:, 8 * d:]], axis=1)
    dw_in_parts = _exchange("a2a_dw_in",
                            jnp.transpose(dw_in_full.reshape(d, N_DEV, in_cols), (1, 0, 2)), False)
    win = _adamw("adamw_w_in", dw_in_parts, w_in[0], m_w_in[0], v_w_in[0])
    dw_sq = jnp.stack([dw_p_sb, dw_p_gdn, dw_out]).reshape(3, N_DEV, rows_p, d)
    dw_sq_parts = _exchange("a2a_dw_sq", jnp.transpose(dw_sq, (1, 0, 2, 3)), False)
    sq = _adamw("adamw_sq", dw_sq_parts.reshape(N_DEV, 3 * rows_p, d),
                jnp.concatenate([w_proj_sb[0], w_proj_gdn[0], w_out[0]], axis=0),
                jnp.concatenate([m_w_proj_sb[0], m_w_proj_gdn[0], m_w_out[0]], axis=0),
                jnp.concatenate([v_w_proj_sb[0], v_w_proj_gdn[0], v_w_out[0]], axis=0))
    dcw = jnp.concatenate([dcw_q, dcw_k, dcw_v], axis=1)
    cw_cols = gdn_conv_w.shape[2]
    dcw_parts = _exchange("a2a_dconv",
                          jnp.transpose(dcw.reshape(8, N_DEV, cw_cols), (1, 0, 2)), False)
    cw = _adamw("adamw_conv", dcw_parts, _pad_to(gdn_conv_w[0], 8, cw_cols),
                _pad_to(m_gdn_conv_w[0], 8, cw_cols),
                jnp.pad(v_gdn_conv_w[0], ((0, 8 - GDN_CONV), (0, 0)), constant_values=1.0))

    outs = [loss, grad_x[None]]
    for k_out in range(4):
        b, nw, fw, al, dtb, gn = split_small((s_g, s_d, s_m, s_v)[k_out])
        sq3 = sq[k_out].reshape(3, 1, rows_p, d)
        outs += [ada[k_out][None], b, nw, win[k_out][None], cw[k_out][None, :GDN_CONV], al, dtb, gn,
                 sq3[0], sq3[1], sq3[2], fw]
    return tuple(outs)
```

```python
import jax
import jax.numpy as jnp
from jax import lax
from jax.experimental import pallas as pl
from jax.experimental.pallas import tpu as pltpu

F32 = jnp.float32
BF16 = jnp.bfloat16
N_DEV = 8
HEAD_DIM = 128
LANES = 128
GDN_CHUNK = 64
GDN_CONV = 4
NORM_EPS = 1e-6
L2_EPS = 1e-6
ADAM_LR = 0.001
ADAM_B1 = 0.9
ADAM_B2 = 0.999
ADAM_EPS = 1e-08
ADAM_WD = 0.01
ADAM_STEP = 10
VMEM_LIMIT = 56 * 1024 * 1024
MESH = pl.DeviceIdType.MESH
NT = (((1,), (1,)), ((), ()))
TN = (((0,), (0,)), ((), ()))


def _pick(n, pref):
    t = min(pref, n)
    while n % t:
        t //= 2
    return t


def _sigmoid(x):
    return 1.0 / (1.0 + jnp.exp(-x))


def _bdot(a, b, dims=None):
    a = a.astype(BF16)
    b = b.astype(BF16)
    if dims is None:
        return jnp.dot(a, b, preferred_element_type=F32)
    return lax.dot_general(a, b, dims, preferred_element_type=F32)


def _split2(x):
    hi = x.astype(BF16)
    lo = (x - hi.astype(F32)).astype(BF16)
    return hi, lo


def _split3(x):
    p1 = x.astype(BF16)
    r = x - p1.astype(F32)
    p2 = r.astype(BF16)
    p3 = (r - p2.astype(F32)).astype(BF16)
    return p1, p2, p3


def _dot2(x, c, dims=None):
    hi, lo = _split2(x)
    return _bdot(hi, c, dims) + _bdot(lo, c, dims)


def _cdot3(c, x):
    p1, p2, p3 = _split3(x)
    return _bdot(c, p1) + _bdot(c, p2) + _bdot(c, p3)


def _dot6(a, b, dims=None):
    a1, a2, a3 = _split3(a)
    b1, b2, b3 = _split3(b)
    small = (_bdot(a1, b3, dims) + _bdot(a3, b1, dims)) + _bdot(a2, b2, dims)
    return _bdot(a1, b1, dims) + ((_bdot(a1, b2, dims) + _bdot(a2, b1, dims)) + small)


def _full(a):
    nd = a.ndim
    return (a, a.shape, lambda *idx: (0,) * nd)


def _rowcall(name, body, grid, ins, outs):
    n_in = len(ins)
    modes = [o[4] for o in outs]
    n_ax = len(grid)

    def kern(*refs):
        idx = tuple(pl.program_id(a) for a in range(n_ax))
        vals = body(idx, *[r[...] for r in refs[:n_in]])
        first_all = idx[0] == 0
        for a in range(1, n_ax):
            first_all = jnp.logical_and(first_all, idx[a] == 0)
        first_inner = idx[-1] == 0
        for r, v, mode in zip(refs[n_in:], vals, modes):
            v = v.astype(r.dtype)
            if mode == "set":
                r[...] = v
            else:
                first = first_all if mode == "acc_all" else first_inner

                @pl.when(first)
                def _(r=r, v=v):
                    r[...] = v

                @pl.when(jnp.logical_not(first))
                def _(r=r, v=v):
                    r[...] += v

    return pl.pallas_call(
        kern, name=name, grid=grid,
        in_specs=[pl.BlockSpec(b, m) for (_, b, m) in ins],
        out_specs=[pl.BlockSpec(o[2], o[3]) for o in outs],
        out_shape=[jax.ShapeDtypeStruct(o[0], o[1]) for o in outs],
        compiler_params=pltpu.CompilerParams(
            dimension_semantics=("arbitrary",) * n_ax, vmem_limit_bytes=VMEM_LIMIT),
    )(*[a for (a, _, _) in ins])


EXCHANGE_SEMS = [pltpu.SemaphoreType.DMA((N_DEV - 1,)), pltpu.SemaphoreType.DMA((N_DEV - 1,)),
                 pltpu.SemaphoreType.DMA]


def _exchange_shape(x, gather):
    return jax.ShapeDtypeStruct((N_DEV,) + tuple(x.shape if gather else x.shape[1:]), x.dtype)


def _exchange_copies(x_ref, out_ref, send_sems, recv_sems, local_sem, gather, start, wait):
    xi, yi, ci = lax.axis_index("x"), lax.axis_index("y"), lax.axis_index("c")
    me = 4 * xi + 2 * yi + ci
    mine = pltpu.make_async_copy(x_ref if gather else x_ref.at[me], out_ref.at[me], local_sem)
    if start:
        mine.start()
    for k in range(1, N_DEV):
        kx, ky, kc = (k >> 2) & 1, (k >> 1) & 1, k & 1
        pid = me ^ k
        src = x_ref if gather else x_ref.at[pid]
        if start:
            pltpu.make_async_remote_copy(
                src_ref=src, dst_ref=out_ref.at[me], send_sem=send_sems.at[k - 1],
                recv_sem=recv_sems.at[k - 1], device_id=(xi ^ kx, yi ^ ky, ci ^ kc),
                device_id_type=MESH).start()
        if wait:
            pltpu.make_async_remote_copy(
                src_ref=src, dst_ref=out_ref.at[pid], send_sem=send_sems.at[k - 1],
                recv_sem=recv_sems.at[k - 1], device_id=(xi, yi, ci), device_id_type=MESH).wait()
    if wait:
        mine.wait()


def _matmul(name, a, b, out_dtype, n_cols=None, col0=0, tm=1024, tn=1024, tk=2048, xchg=None,
            gather=False):
    m, k = a.shape
    n = b.shape[1] if n_cols is None else n_cols
    tm = _pick(m, tm)
    tn = _pick(n, tn)
    while col0 % tn:
        tn //= 2
    tk = _pick(k, tk)
    nk = k // tk
    cb = col0 // tn
    grid = (m // tm, n // tn, nk)

    def kern(a_ref, b_ref, *rest):
        if xchg is None:
            o_ref, acc_ref = rest
        else:
            x_ref, o_ref, xo_ref, acc_ref, send_sems, recv_sems, local_sem = rest
            step = (pl.program_id(0) * grid[1] + pl.program_id(1)) * nk + pl.program_id(2)

            @pl.when(step == 0)
            def _():
                _exchange_copies(x_ref, xo_ref, send_sems, recv_sems, local_sem, gather, True, False)

        kk = pl.program_id(2)
        part = jnp.dot(a_ref[...], b_ref[...], preferred_element_type=F32)
        if nk == 1:
            o_ref[...] = part.astype(o_ref.dtype)
        else:
            @pl.when(kk == 0)
            def _():
                acc_ref[...] = part

            @pl.when(kk > 0)
            def _():
                acc_ref[...] += part

            @pl.when(kk == nk - 1)
            def _():
                o_ref[...] = acc_ref[...].astype(o_ref.dtype)

        if xchg is not None:
            @pl.when(step == grid[0] * grid[1] * nk - 1)
            def _():
                _exchange_copies(x_ref, xo_ref, send_sems, recv_sems, local_sem, gather, False, True)

    in_specs = [pl.BlockSpec((tm, tk), lambda i, j, kk: (i, kk)),
                pl.BlockSpec((tk, tn), lambda i, j, kk: (kk, j + cb))]
    out_specs = [pl.BlockSpec((tm, tn), lambda i, j, kk: (i, j))]
    out_shape = [jax.ShapeDtypeStruct((m, n), out_dtype)]
    scratch = [pltpu.VMEM((tm, tn), F32)]
    args = (a, b)
    if xchg is not None:
        in_specs.append(pl.BlockSpec(memory_space=pl.ANY))
        out_specs.append(pl.BlockSpec(memory_space=pl.ANY))
        out_shape.append(_exchange_shape(xchg, gather))
        scratch += EXCHANGE_SEMS
        args = (a, b, xchg)
    res = pl.pallas_call(
        kern, name=name, grid=grid, in_specs=in_specs, out_specs=out_specs, out_shape=out_shape,
        scratch_shapes=scratch,
        compiler_params=pltpu.CompilerParams(
            dimension_semantics=("arbitrary", "arbitrary", "arbitrary"),
            vmem_limit_bytes=VMEM_LIMIT),
    )(*args)
    return res[0] if xchg is None else (res[0], res[1])


def _exchange(name, x, gather):
    def body(x_ref, out_ref, send_sems, recv_sems, local_sem):
        _exchange_copies(x_ref, out_ref, send_sems, recv_sems, local_sem, gather, True, True)

    return pl.pallas_call(
        body, name=name,
        out_shape=_exchange_shape(x, gather),
        in_specs=[pl.BlockSpec(memory_space=pl.ANY)],
        out_specs=pl.BlockSpec(memory_space=pl.ANY),
        scratch_shapes=EXCHANGE_SEMS,
    )(x)


def _adamw(name, parts, w, m, v):
    p, r, c = parts.shape
    tr = r if r <= 64 else _pick(r, 64)

    def body(idx, parts_v, w_v, m_v, v_v):
        g = parts_v[0].astype(F32)
        for s in range(1, p):
            g = g + parts_v[s].astype(F32)
        m2 = ADAM_B1 * m_v + (1.0 - ADAM_B1) * g
        v2 = ADAM_B2 * v_v + (1.0 - ADAM_B2) * (g * g)
        m_hat = m2 / (1.0 - ADAM_B1 ** ADAM_STEP)
        v_hat = v2 / (1.0 - ADAM_B2 ** ADAM_STEP)
        delta = -ADAM_LR * (m_hat / (jnp.sqrt(v_hat) + ADAM_EPS) + ADAM_WD * w_v)
        return g, delta, m2, v2

    rb = ((tr, c), lambda i: (i, 0))
    return _rowcall(
        name, body, (r // tr,),
        [(parts, (p, tr, c), lambda i: (0, i, 0)), (w,) + rb, (m,) + rb, (v,) + rb],
        [((r, c), F32) + rb + ("set",)] * 4)


TR_WIDE = 256


def _norm_mod_fwd(x, shift, scale, norm_w):
    t, d = x.shape
    tr = _pick(t, TR_WIDE)

    def body(idx, x_v, sh, sc, nw):
        rstd = lax.rsqrt(jnp.mean(x_v * x_v, axis=1, keepdims=True) + NORM_EPS)
        return ((x_v * rstd) * nw * (1.0 + sc) + sh,)

    rb = ((tr, d), lambda i: (i, 0))
    return _rowcall("norm_mod_fwd", body, (t // tr,),
                    [(x,) + rb, _full(shift), _full(scale), _full(norm_w)],
                    [((t, d), BF16) + rb + ("set",)])[0]


def _norm_mod_bwd(dh_a, dh_b, x, dx2, scale, norm_w):
    t, d = x.shape
    tr = _pick(t, TR_WIDE)

    def body(idx, dha, dhb, x_v, dx2_v, sc, nw):
        dh = dha + dhb
        rstd = lax.rsqrt(jnp.mean(x_v * x_v, axis=1, keepdims=True) + NORM_EPS)
        xn = x_v * rstd
        m1 = 1.0 + sc
        dxn = dh * nw * m1
        dx = rstd * (dxn - xn * jnp.mean(dxn * xn, axis=1, keepdims=True))
        dhx = dh * xn
        return (dx2_v + dx,
                jnp.sum(dh, axis=0, keepdims=True),
                jnp.sum(dhx * nw, axis=0, keepdims=True),
                jnp.sum(dhx * m1, axis=0, keepdims=True))

    rb = ((tr, d), lambda i: (i, 0))
    vec = ((1, d), F32, (1, d), lambda i: (0, 0), "acc_all")
    return _rowcall("norm_mod_bwd", body, (t // tr,),
                    [(dh_a,) + rb, (dh_b,) + rb, (x,) + rb, (dx2,) + rb, _full(scale), _full(norm_w)],
                    [((t, d), F32) + rb + ("set",), vec, vec, vec])


SB_TILE = 512


SB_SCAN = 256


def _neg_log_not_beta(z):
    return jnp.maximum(z, 0.0) + jnp.log(1.0 + jnp.exp(-jnp.abs(z)))


def _key_scan(x, tri, later):
    n_blk = x.shape[1] // SB_SCAN
    parts = [x[:, g * SB_SCAN:(g + 1) * SB_SCAN] for g in range(n_blk)]
    sums = [jnp.sum(p, axis=1, keepdims=True) for p in parts]
    outs = []
    for g in range(n_blk):
        o = _dot2(parts[g], tri)
        others = range(g + 1, n_blk) if later else range(g)
        for g2 in others:
            o = o + sums[g2]
        outs.append(o)
    total = sums[0]
    for s in sums[1:]:
        total = total + s
    return (outs[0] if n_blk == 1 else jnp.concatenate(outs, axis=1)), total


def _scan_tri(kind):
    row = lax.broadcasted_iota(jnp.int32, (SB_SCAN, SB_SCAN), 0)
    col = lax.broadcasted_iota(jnp.int32, (SB_SCAN, SB_SCAN), 1)
    return {"after": row > col, "upto": row <= col, "before": row < col}[kind].astype(BF16)


def _sb_fwd(qkv, n_heads):
    t = qkv.shape[0]
    tq = _pick(t, SB_TILE)
    nq = t // tq
    scale = HEAD_DIM ** -0.5

    def kern(q_ref, k_ref, v_ref, o_ref, c_ref):
        i_blk = pl.program_id(1)
        row = lax.broadcasted_iota(jnp.int32, (tq, tq), 0)
        col = lax.broadcasted_iota(jnp.int32, (tq, tq), 1)
        causal = col < row
        after = _scan_tri("after")
        qb = q_ref[...]

        def tile(j_blk, c, acc, diag):
            r0 = pl.multiple_of(j_blk * tq, tq)
            kb = k_ref[pl.ds(r0, tq), :]
            vb = v_ref[pl.ds(r0, tq), :]
            z = lax.dot_general(qb, kb, NT, preferred_element_type=F32) * scale
            n = _neg_log_not_beta(z)
            if diag:
                n = jnp.where(causal, n, 0.0)
            later, total = _key_scan(n, after, True)
            a = jnp.exp(z - (n + later + c))
            if diag:
                a = jnp.where(causal, a, 0.0)
            acc = acc + jnp.dot(a.astype(BF16), vb, preferred_element_type=F32)
            return c + total, acc

        c, acc = tile(i_blk, jnp.zeros((tq, 1), F32), jnp.zeros((tq, HEAD_DIM), F32), True)
        c, acc = lax.fori_loop(0, i_blk, lambda jj, ca: tile(i_blk - 1 - jj, ca[0], ca[1], False),
                               (c, acc))
        o_ref[...] = acc
        c_ref[...] = jnp.broadcast_to(c, (tq, HEAD_DIM))

    h = n_heads
    return pl.pallas_call(
        kern, name="sb_fwd", grid=(h, nq),
        in_specs=[pl.BlockSpec((tq, HEAD_DIM), lambda hh, i: (i, hh)),
                  pl.BlockSpec((t, HEAD_DIM), lambda hh, i: (0, h + hh)),
                  pl.BlockSpec((t, HEAD_DIM), lambda hh, i: (0, 2 * h + hh))],
        out_specs=[pl.BlockSpec((tq, HEAD_DIM), lambda hh, i: (i, hh)),
                   pl.BlockSpec((tq, HEAD_DIM), lambda hh, i: (i, hh))],
        out_shape=[jax.ShapeDtypeStruct((t, h * HEAD_DIM), F32),
                   jax.ShapeDtypeStruct((t, h * HEAD_DIM), F32)],
        compiler_params=pltpu.CompilerParams(
            dimension_semantics=("arbitrary", "arbitrary"), vmem_limit_bytes=VMEM_LIMIT),
    )(qkv, qkv, qkv)


def _sb_bwd(qkv, do, ctot, n_heads):
    t = qkv.shape[0]
    tq = _pick(t, SB_TILE)
    nq = t // tq
    scale = HEAD_DIM ** -0.5

    def kern(q_ref, do_ref, c_ref, k_ref, v_ref, dq_ref, dk_ref, dv_ref, dk_acc, dv_acc):
        i_blk = pl.program_id(1)
        row = lax.broadcasted_iota(jnp.int32, (tq, tq), 0)
        col = lax.broadcasted_iota(jnp.int32, (tq, tq), 1)
        causal = col < row
        upto = _scan_tri("upto")
        before = _scan_tri("before")
        qb = q_ref[...]
        dob = do_ref[...]
        ctot_v = c_ref[...][:, 0:1]

        @pl.when(i_blk == 0)
        def _():
            dk_acc[...] = jnp.zeros_like(dk_acc)
            dv_acc[...] = jnp.zeros_like(dv_acc)

        def tile(j_blk, cl, pe, dq, diag):
            r0 = pl.multiple_of(j_blk * tq, tq)
            kb = k_ref[pl.ds(r0, tq), :]
            vb = v_ref[pl.ds(r0, tq), :]
            z = lax.dot_general(qb, kb, NT, preferred_element_type=F32) * scale
            n = _neg_log_not_beta(z)
            if diag:
                n = jnp.where(causal, n, 0.0)
            upto_s, n_total = _key_scan(n, upto, False)
            lb = z - n
            a = jnp.exp(lb - ((ctot_v - cl) - upto_s))
            if diag:
                a = jnp.where(causal, a, 0.0)
            da = lax.dot_general(dob, vb, NT, preferred_element_type=F32)
            e = da * a
            before_s, e_total = _key_scan(e, before, False)
            beta = jnp.exp(lb)
            dz = (e - beta * (e + (pe + before_s))) * scale
            if diag:
                dz = jnp.where(causal, dz, 0.0)
            dzb = dz.astype(BF16)
            dq = dq + jnp.dot(dzb, kb, preferred_element_type=F32)
            dk_acc[pl.ds(r0, tq), :] += lax.dot_general(dzb, qb, TN, preferred_element_type=F32)
            dv_acc[pl.ds(r0, tq), :] += lax.dot_general(a.astype(BF16), dob, TN,
                                                        preferred_element_type=F32)
            return cl + n_total, pe + e_total, dq

        zero = jnp.zeros((tq, 1), F32)
        cl, pe, dq = lax.fori_loop(0, i_blk, lambda j, s: tile(j, s[0], s[1], s[2], False),
                                   (zero, zero, jnp.zeros((tq, HEAD_DIM), F32)))
        cl, pe, dq = tile(i_blk, cl, pe, dq, True)
        dq_ref[...] = dq.astype(BF16)

        @pl.when(i_blk == nq - 1)
        def _():
            dk_ref[...] = dk_acc[...].astype(BF16)
            dv_ref[...] = dv_acc[...].astype(BF16)

    h = n_heads
    qspec = pl.BlockSpec((tq, HEAD_DIM), lambda hh, i: (i, hh))
    hspec = pl.BlockSpec((t, HEAD_DIM), lambda hh, i: (0, hh))
    out = jax.ShapeDtypeStruct((t, h * HEAD_DIM), BF16)
    return pl.pallas_call(
        kern, name="sb_bwd", grid=(h, nq),
        in_specs=[qspec, qspec, qspec,
                  pl.BlockSpec((t, HEAD_DIM), lambda hh, i: (0, h + hh)),
                  pl.BlockSpec((t, HEAD_DIM), lambda hh, i: (0, 2 * h + hh))],
        out_specs=[qspec, hspec, hspec],
        out_shape=[out, out, out],
        scratch_shapes=[pltpu.VMEM((t, HEAD_DIM), F32), pltpu.VMEM((t, HEAD_DIM), F32)],
        compiler_params=pltpu.CompilerParams(
            dimension_semantics=("arbitrary", "arbitrary"), vmem_limit_bytes=VMEM_LIMIT),
    )(qkv, do, ctot, qkv, qkv)


GDN_ROWS = 512
GDN_HEADS_PER_STEP = 2
TR_HEAD = 512


def _shift_rows(cur, halo, k, back):
    n = cur.shape[0]
    ext = jnp.concatenate([cur, halo], axis=0)
    return pltpu.roll(ext, k if back else n + 8 - k, 0)[:n]


def _conv_fwd(cur, halo, w):
    out = cur * w[GDN_CONV - 1:GDN_CONV, :]
    for i in range(GDN_CONV - 1):
        out = out + _shift_rows(cur, halo, GDN_CONV - 1 - i, True) * w[i:i + 1, :]
    return out


def _chunk_tri(n, upper):
    row = lax.broadcasted_iota(jnp.int32, (n, n), 0)
    col = lax.broadcasted_iota(jnp.int32, (n, n), 1)
    same = (row // GDN_CHUNK) == (col // GDN_CHUNK)
    tri = (col >= row) if upper else (col <= row)
    return jnp.logical_and(same, tri).astype(BF16)


def _lane_pick(x, lane):
    idx = lax.broadcasted_iota(jnp.int32, x.shape, 1)
    return jnp.sum(jnp.where(idx == lane, x, 0.0), axis=1, keepdims=True)


def _softplus(x):
    y = jnp.exp(-jnp.abs(x))
    u = 1.0 + y
    log1p = jnp.where(u == 1.0, y, jnp.log(u) * (y / jnp.where(u == 1.0, 1.0, u - 1.0)))
    return jnp.maximum(x, 0.0) + log1p


def _gdn_specs(t, tr, h, proj_seg0):
    def slab(seg):
        return ((tr, HEAD_DIM), lambda i, hh: (i, seg * h + hh))

    def halo_before(seg):
        return ((8, HEAD_DIM), lambda i, hh: (jnp.maximum(i * (tr // 8) - 1, 0), seg * h + hh))

    return slab, halo_before


def _gdn_pre_fwd(proj_b, proj_c, conv_w8, ab, n_heads):
    t = proj_b.shape[0]
    h = n_heads
    d = h * HEAD_DIM
    tr = _pick(t, TR_HEAD)
    slab, halo = _gdn_specs(t, tr, h, 1)
    scale = HEAD_DIM ** -0.5

    def body(idx, q_c, q_h, k_c, k_h, v_c, v_h, wq, wk, wv, gbga, ab_v):
        i, hh = idx
        live = (i > 0).astype(F32)
        outs = []
        for cur, hal, w, kind in ((q_c, q_h, wq, "q"), (k_c, k_h, wk, "k"), (v_c, v_h, wv, "v")):
            u = _conv_fwd(cur, hal * live, w)
            s = u * _sigmoid(u)
            if kind != "v":
                s = s * lax.rsqrt(jnp.sum(s * s, axis=1, keepdims=True) + L2_EPS)
            if kind == "q":
                s = s * scale
            outs.append(s)
        beta = _sigmoid(_lane_pick(gbga, hh))
        a_log = _lane_pick(ab_v[0:1, :], hh)
        dt = _lane_pick(ab_v[1:2, :], hh)
        g = -jnp.exp(a_log) * _softplus(_lane_pick(gbga, h + hh) + dt)
        g_rep = jnp.broadcast_to(g, (tr, HEAD_DIM))
        big_g = _cdot3(_chunk_tri(tr, False), g_rep)
        return outs + [jnp.broadcast_to(beta, (tr, HEAD_DIM)), big_g]

    wspec = lambda seg: ((8, HEAD_DIM), lambda i, hh: (0, seg * h + hh))
    out = ((t, d), F32, (tr, HEAD_DIM), lambda i, hh: (i, hh), "set")
    return _rowcall(
        "gdn_pre_fwd", body, (t // tr, h),
        [(proj_b,) + slab(1), (proj_b,) + halo(1), (proj_b,) + slab(2), (proj_b,) + halo(2),
         (proj_b,) + slab(3), (proj_b,) + halo(3),
         (conv_w8,) + wspec(0), (conv_w8,) + wspec(1), (conv_w8,) + wspec(2),
         (proj_c, (tr, LANES), lambda i, hh: (i, 0)), _full(ab)],
        [out] * 5)


def _gdn_consts():
    row = lax.broadcasted_iota(jnp.int32, (GDN_CHUNK, GDN_CHUNK), 0)
    col = lax.broadcasted_iota(jnp.int32, (GDN_CHUNK, GDN_CHUNK), 1)
    lane = lax.broadcasted_iota(jnp.int32, (GDN_CHUNK, HEAD_DIM), 1)
    return row > col, row >= col, (row == col).astype(F32), (lane == 0).astype(BF16)


def _gdn_local(q, k, v, be, ge, consts):
    lower, tril, eye, sel = consts
    kb_ = k * be
    vb_ = v * be
    e_g = jnp.exp(ge)
    kg = kb_ * e_g
    p1, p2, p3 = _split3(ge)
    g_i = _bdot(p1, sel, NT) + _bdot(p2, sel, NT) + _bdot(p3, sel, NT)
    g_j = _bdot(sel, p1, NT) + _bdot(sel, p2, NT) + _bdot(sel, p3, NT)
    dec = jnp.where(tril, jnp.exp(jnp.minimum(g_i - g_j, 0.0)), 0.0)
    kk = _bdot(kb_, k, NT)
    qk = _bdot(q, k, NT)
    g_last = jnp.min(ge, axis=0, keepdims=True)
    kdec_f = jnp.exp(g_last - ge)
    return dict(kb=kb_, vb=vb_, e_g=e_g, kg=kg, dec=dec, kk=kk, qk=qk, kdec_f=kdec_f,
                k_dec=k * kdec_f, q_dec=q * e_g, gamma=jnp.exp(g_last),
                intra=jnp.where(tril, qk * dec, 0.0))


def _wy_lower_t(k, be, ge, sel):
    row = lax.broadcasted_iota(jnp.int32, (GDN_CHUNK, GDN_CHUNK), 0)
    col = lax.broadcasted_iota(jnp.int32, (GDN_CHUNK, GDN_CHUNK), 1)
    p1, p2, p3 = _split3(ge)
    g_row = _bdot(p1, sel, NT) + _bdot(p2, sel, NT) + _bdot(p3, sel, NT)
    g_col = _bdot(sel, p1, NT) + _bdot(sel, p2, NT) + _bdot(sel, p3, NT)
    dec_t = jnp.exp(jnp.minimum(g_col - g_row, 0.0))
    return jnp.where(col > row, _bdot(k, k * be, NT) * dec_t, 0.0)


def _unit_lower_inverse(lw_t):
    n = lw_t.shape[0]
    n_slab = GDN_CHUNK // 8
    row = lax.broadcasted_iota(jnp.int32, (n, 8, GDN_CHUNK), 1)
    col = lax.broadcasted_iota(jnp.int32, (n, 8, GDN_CHUNK), 2)
    unit = lax.broadcasted_iota(jnp.int32, (1, 1, GDN_CHUNK), 2)
    lw = [lw_t[:, 8 * g:8 * g + 8, :] for g in range(n_slab)]
    inv = [(col == row + 8 * g).astype(F32) for g in range(n_slab)]
    for i in range(1, GDN_CHUNK):
        acc = lw[0][:, :, i:i + 1] * inv[0]
        for g in range(1, (i - 1) // 8 + 1):
            acc = acc + lw[g][:, :, i:i + 1] * inv[g]
        new_row = (unit == i).astype(F32) - jnp.sum(acc, axis=1, keepdims=True)
        inv[i // 8] = jnp.where(row == i % 8, new_row, inv[i // 8])
    return jnp.concatenate(inv, axis=1)


def _gdn_fwd(q, k, v, be, ge, n_heads):
    t = q.shape[0]
    h = n_heads
    hg = GDN_HEADS_PER_STEP
    tr = _pick(t, GDN_ROWS)
    nc = tr // GDN_CHUNK

    def kern(q_ref, k_ref, v_ref, b_ref, g_ref, o_ref, s_ref, tm_ref, state):
        consts = _gdn_consts()
        lower, tril, eye, sel = consts
        lanes = [pl.ds(hs * HEAD_DIM, HEAD_DIM) for hs in range(hg)]

        @pl.when(pl.program_id(1) == 0)
        def _():
            state[...] = jnp.zeros_like(state)

        lw_t = []
        for hs in range(hg):
            for ci in range(nc):
                rows = pl.ds(ci * GDN_CHUNK, GDN_CHUNK)
                lw_t.append(_wy_lower_t(k_ref[rows, lanes[hs]], b_ref[rows, lanes[hs]],
                                        g_ref[rows, lanes[hs]], sel))
        t_all = _unit_lower_inverse(jnp.stack(lw_t))
        for hs in range(hg):
            for ci in range(nc):
                tm_ref[hs, pl.ds(ci * GDN_CHUNK, GDN_CHUNK), :] = t_all[hs * nc + ci]

        def chunk(ci, carry):
            rows = pl.ds(pl.multiple_of(ci * GDN_CHUNK, GDN_CHUNK), GDN_CHUNK)
            results = []
            for hs in range(hg):
                ln = lanes[hs]
                loc = _gdn_local(q_ref[rows, ln], k_ref[rows, ln], v_ref[rows, ln], b_ref[rows, ln],
                                 g_ref[rows, ln], consts)
                t_mat = tm_ref[hs, rows, :]
                u = _bdot(t_mat, loc["vb"])
                w = _bdot(t_mat, loc["kg"])
                s = state[hs]
                v_new = u - _bdot(w, s)
                results.append((s, _bdot(loc["q_dec"], s) + _bdot(loc["intra"], v_new),
                                s * loc["gamma"] + _bdot(loc["k_dec"], v_new, TN)))
            for hs in range(hg):
                s_ref[hs, ci] = results[hs][0]
                o_ref[rows, lanes[hs]] = results[hs][1]
                state[hs] = results[hs][2]
            return carry

        lax.fori_loop(0, nc, chunk, 0)

    slab = pl.BlockSpec((tr, hg * HEAD_DIM), lambda hp, j: (j, hp))
    return pl.pallas_call(
        kern, name="gdn_fwd", grid=(h // hg, t // tr),
        in_specs=[slab] * 5,
        out_specs=[slab,
                   pl.BlockSpec((hg, nc, HEAD_DIM, HEAD_DIM), lambda hp, j: (hp, j, 0, 0)),
                   pl.BlockSpec((hg, tr, GDN_CHUNK), lambda hp, j: (hp, j, 0))],
        out_shape=[jax.ShapeDtypeStruct((t, h * HEAD_DIM), F32),
                   jax.ShapeDtypeStruct((h, t // GDN_CHUNK, HEAD_DIM, HEAD_DIM), F32),
                   jax.ShapeDtypeStruct((h, t, GDN_CHUNK), F32)],
        scratch_shapes=[pltpu.VMEM((hg, HEAD_DIM, HEAD_DIM), F32)],
        compiler_params=pltpu.CompilerParams(
            dimension_semantics=("arbitrary", "arbitrary"), vmem_limit_bytes=VMEM_LIMIT),
    )(q, k, v, be, ge)


def _gdn_bwd(q, k, v, be, ge, do, states, tms, n_heads):
    t = q.shape[0]
    h = n_heads
    hg = GDN_HEADS_PER_STEP
    tr = _pick(t, GDN_ROWS)
    nc = tr // GDN_CHUNK
    nj = t // tr

    def kern(q_ref, k_ref, v_ref, b_ref, g_ref, do_ref, s_ref, tm_ref,
             dq_ref, dk_ref, dv_ref, db_ref, dg_ref, dstate):
        consts = _gdn_consts()
        lower, tril, eye, sel = consts
        last_row = lax.broadcasted_iota(jnp.int32, (GDN_CHUNK, HEAD_DIM), 0) == GDN_CHUNK - 1

        @pl.when(pl.program_id(1) == 0)
        def _():
            dstate[...] = jnp.zeros_like(dstate)

        def lsum(x):
            return jnp.sum(x, axis=1, keepdims=True)

        def chunk(cc, carry):
            ci = nc - 1 - cc
            rows = pl.ds(pl.multiple_of(ci * GDN_CHUNK, GDN_CHUNK), GDN_CHUNK)
            results = [one_head(ci, hs) for hs in range(hg)]
            for hs in range(hg):
                ln = pl.ds(hs * HEAD_DIM, HEAD_DIM)
                dstate[hs] = results[hs][0]
                for ref, val in zip((dq_ref, dk_ref, dv_ref, db_ref, dg_ref), results[hs][1:]):
                    ref[rows, ln] = val
            return carry

        def one_head(ci, hs):
            rows = pl.ds(pl.multiple_of(ci * GDN_CHUNK, GDN_CHUNK), GDN_CHUNK)
            ln = pl.ds(hs * HEAD_DIM, HEAD_DIM)
            qv, kv, vv, bev = q_ref[rows, ln], k_ref[rows, ln], v_ref[rows, ln], b_ref[rows, ln]
            loc = _gdn_local(qv, kv, vv, bev, g_ref[rows, ln], consts)
            t_mat = tm_ref[hs, rows, :]
            s = s_ref[hs, ci]
            d_o = do_ref[rows, ln]
            d_s = dstate[hs]
            u = _bdot(t_mat, loc["vb"])
            w = _bdot(t_mat, loc["kg"])
            v_new = u - _bdot(w, s)
            dv_new = _bdot(loc["intra"], d_o, TN) + _bdot(loc["k_dec"], d_s)
            d_intra = jnp.where(tril, _bdot(d_o, v_new, NT), 0.0)
            dq_dec = _bdot(d_o, s, NT)
            dk_dec = _bdot(v_new, d_s, NT)
            dgamma = jnp.sum(lsum(d_s * s), axis=0, keepdims=True)
            dw = -_bdot(dv_new, s, NT)
            d_s_new = (_bdot(loc["q_dec"], d_o, TN) + loc["gamma"] * d_s
                       - _bdot(w, dv_new, TN))
            dtm = _bdot(dv_new, loc["vb"], NT) + _bdot(dw, loc["kg"], NT)
            dvb = _bdot(t_mat, dv_new, TN)
            dkg = _bdot(t_mat, dw, TN)
            dlw = jnp.where(lower, -_dot6(t_mat, _dot6(dtm, t_mat, NT), TN), 0.0)
            dkk = dlw * loc["dec"]
            dqk = d_intra * loc["dec"]
            ddec = dlw * loc["kk"] + d_intra * loc["qk"]
            dkb = _bdot(dkk, kv) + dkg * loc["e_g"]
            dk = (_bdot(dkk, loc["kb"], TN) + _bdot(dqk, qv, TN) + dk_dec * loc["kdec_f"]
                  + dkb * bev)
            dq = _bdot(dqk, kv) + dq_dec * loc["e_g"]
            dgd = ddec * loc["dec"]
            r_kdec = lsum(dk_dec * loc["k_dec"])
            col_sums = lsum(_dot2(dgd, sel, TN))
            d_big_g = (lsum(dgd) - col_sums
                       + lsum(dq_dec * loc["q_dec"]) - r_kdec + lsum(dkg * loc["kg"]))
            d_last = jnp.sum(r_kdec, axis=0, keepdims=True) + dgamma * loc["gamma"][:, 0:1]
            d_big_g = jnp.broadcast_to(d_big_g, (GDN_CHUNK, HEAD_DIM)) + jnp.where(last_row, d_last,
                                                                                  0.0)
            d_beta = jnp.broadcast_to(lsum(dkb * kv) + lsum(dvb * vv), (GDN_CHUNK, HEAD_DIM))
            return d_s_new, dq, dk, dvb * bev, d_beta, d_big_g

        lax.fori_loop(0, nc, chunk, 0)

    slab = pl.BlockSpec((tr, hg * HEAD_DIM), lambda hp, j: (nj - 1 - j, hp))
    out = jax.ShapeDtypeStruct((t, h * HEAD_DIM), F32)
    return pl.pallas_call(
        kern, name="gdn_bwd", grid=(h // hg, nj),
        in_specs=[slab] * 6 + [
            pl.BlockSpec((hg, nc, HEAD_DIM, HEAD_DIM), lambda hp, j: (hp, nj - 1 - j, 0, 0)),
            pl.BlockSpec((hg, tr, GDN_CHUNK), lambda hp, j: (hp, nj - 1 - j, 0))],
        out_specs=[slab] * 5,
        out_shape=[out] * 5,
        scratch_shapes=[pltpu.VMEM((hg, HEAD_DIM, HEAD_DIM), F32)],
        compiler_params=pltpu.CompilerParams(
            dimension_semantics=("arbitrary", "arbitrary"), vmem_limit_bytes=VMEM_LIMIT),
    )(q, k, v, be, ge, do, states, tms)


def _gdn_pre_bwd(proj_b, proj_c, conv_w8, ab, dq, dk, dv, dbe, dge, n_heads):
    t = proj_b.shape[0]
    h = n_heads
    d = h * HEAD_DIM
    tr = _pick(t, TR_HEAD)
    slab, halo = _gdn_specs(t, tr, h, 1)
    scale = HEAD_DIM ** -0.5

    def body(idx, q_c, q_h, k_c, k_h, v_c, v_h, wq, wk, wv, gbga, ab_v, dq_v, dk_v, dv_v, dbe_v,
             dge_v):
        i, hh = idx
        live = (i > 0).astype(F32)
        outs = []
        for cur, hal, w, dy, kind in ((q_c, q_h, wq, dq_v, "q"), (k_c, k_h, wk, dk_v, "k"),
                                      (v_c, v_h, wv, dv_v, "v")):
            u = _conv_fwd(cur, hal * live, w)
            sg = _sigmoid(u)
            if kind == "v":
                ds = dy
            else:
                s = u * sg
                r = lax.rsqrt(jnp.sum(s * s, axis=1, keepdims=True) + L2_EPS)
                y = s * r
                if kind == "q":
                    dy = dy * scale
                ds = r * (dy - y * jnp.sum(dy * y, axis=1, keepdims=True))
            outs.append(ds * (sg * (1.0 + u * (1.0 - sg))))
        lane = lax.broadcasted_iota(jnp.int32, (tr, LANES), 1)
        lane1 = lax.broadcasted_iota(jnp.int32, (1, LANES), 1)
        beta = _sigmoid(_lane_pick(gbga, hh))
        a_neg = -jnp.exp(_lane_pick(ab_v[0:1, :], hh))
        xg = _lane_pick(gbga, h + hh) + _lane_pick(ab_v[1:2, :], hh)
        g = a_neg * _softplus(xg)
        dgb = dbe_v * (beta * (1.0 - beta))
        dg = _cdot3(_chunk_tri(tr, True), dge_v)
        dga = dg * (a_neg * _sigmoid(xg))
        d_gates = jnp.where(lane == hh, dgb, 0.0) + jnp.where(lane == h + hh, dga, 0.0)
        d_ab = (jnp.where(lane1 == hh, jnp.sum(dg * g, axis=0, keepdims=True), 0.0)
                + jnp.where(lane1 == h + hh, jnp.sum(dga, axis=0, keepdims=True), 0.0))
        return outs + [d_gates, d_ab]

    wspec = lambda seg: ((8, HEAD_DIM), lambda i, hh: (0, seg * h + hh))
    hs = ((tr, HEAD_DIM), lambda i, hh: (i, hh))
    out = ((t, d), F32) + hs + ("set",)
    return _rowcall(
        "gdn_pre_bwd", body, (t // tr, h),
        [(proj_b,) + slab(1), (proj_b,) + halo(1), (proj_b,) + slab(2), (proj_b,) + halo(2),
         (proj_b,) + slab(3), (proj_b,) + halo(3),
         (conv_w8,) + wspec(0), (conv_w8,) + wspec(1), (conv_w8,) + wspec(2),
         (proj_c, (tr, LANES), lambda i, hh: (i, 0)), _full(ab),
         (dq,) + hs, (dk,) + hs, (dv,) + hs, (dbe,) + hs, (dge,) + hs],
        [out, out, out,
         ((t, LANES), F32, (tr, LANES), lambda i, hh: (i, 0), "acc_inner"),
         ((1, LANES), F32, (1, LANES), lambda i, hh: (0, 0), "acc_all")])


def _conv_bwd(proj_b, conv_w8, du_q, du_k, du_v, n_heads):
    t = proj_b.shape[0]
    h = n_heads
    d = h * HEAD_DIM
    tr = _pick(t, TR_HEAD)
    ni = t // tr

    def body(idx, q_c, q_h, k_c, k_h, v_c, v_h, wq, wk, wv, uq, uq_n, uk, uk_n, uv, uv_n):
        hh, i = idx
        live_b = (i > 0).astype(F32)
        live_a = (i < ni - 1).astype(F32)
        d_ins, d_ws = [], []
        for cur, hal, w, du, du_n in ((q_c, q_h, wq, uq, uq_n), (k_c, k_h, wk, uk, uk_n),
                                      (v_c, v_h, wv, uv, uv_n)):
            hal = hal * live_b
            du_n = du_n * live_a
            d_in = du * w[GDN_CONV - 1:GDN_CONV, :]
            rows = [jnp.sum(du * cur, axis=0, keepdims=True)]
            for i_tap in range(GDN_CONV - 2, -1, -1):
                kshift = GDN_CONV - 1 - i_tap
                d_in = d_in + _shift_rows(du, du_n, kshift, False) * w[i_tap:i_tap + 1, :]
                rows.insert(0, jnp.sum(du * _shift_rows(cur, hal, kshift, True), axis=0,
                                       keepdims=True))
            d_ins.append(d_in)
            tap = lax.broadcasted_iota(jnp.int32, (8, HEAD_DIM), 0)
            d_w = jnp.zeros((8, HEAD_DIM), F32)
            for i_tap in range(GDN_CONV):
                d_w = d_w + jnp.where(tap == i_tap, rows[i_tap], 0.0)
            d_ws.append(d_w)
        return d_ins + d_ws

    def slab(seg):
        return ((tr, HEAD_DIM), lambda hh, i: (i, seg * h + hh))

    def halo_b(seg):
        return ((8, HEAD_DIM), lambda hh, i: (jnp.maximum(i * (tr // 8) - 1, 0), seg * h + hh))

    hs = ((tr, HEAD_DIM), lambda hh, i: (i, hh))
    halo_a = ((8, HEAD_DIM), lambda hh, i: (jnp.minimum((i + 1) * (tr // 8), t // 8 - 1), hh))
    wspec = lambda seg: ((8, HEAD_DIM), lambda hh, i: (0, seg * h + hh))
    wout = ((8, d), F32, (8, HEAD_DIM), lambda hh, i: (0, hh), "acc_inner")
    out = ((t, d), BF16) + hs + ("set",)
    res = _rowcall(
        "conv_bwd", body, (h, ni),
        [(proj_b,) + slab(1), (proj_b,) + halo_b(1), (proj_b,) + slab(2), (proj_b,) + halo_b(2),
         (proj_b,) + slab(3), (proj_b,) + halo_b(3),
         (conv_w8,) + wspec(0), (conv_w8,) + wspec(1), (conv_w8,) + wspec(2),
         (du_q,) + hs, (du_q,) + halo_a, (du_k,) + hs, (du_k,) + halo_a, (du_v,) + hs,
         (du_v,) + halo_a],
        [out, out, out, wout, wout, wout])
    return res


def _gdn_post_fwd(o, proj_b, gnw, n_heads):
    t, d = o.shape
    h = n_heads
    tr = _pick(t, TR_HEAD)

    def body(idx, o_v, z, w):
        rstd = lax.rsqrt(jnp.mean(o_v * o_v, axis=1, keepdims=True) + NORM_EPS)
        return ((o_v * rstd) * w * (z * _sigmoid(z)),)

    hs = ((tr, HEAD_DIM), lambda i, hh: (i, hh))
    return _rowcall("gdn_post_fwd", body, (t // tr, h),
                    [(o,) + hs, (proj_b, (tr, HEAD_DIM), lambda i, hh: (i, 4 * h + hh)), _full(gnw)],
                    [((t, d), BF16) + hs + ("set",)])[0]


def _gdn_post_bwd(do_gdn, o, proj_b, gnw, n_heads):
    t, d = o.shape
    h = n_heads
    tr = _pick(t, TR_HEAD)

    def body(idx, dog, o_v, z, w):
        rstd = lax.rsqrt(jnp.mean(o_v * o_v, axis=1, keepdims=True) + NORM_EPS)
        n = o_v * rstd
        sg = _sigmoid(z)
        don = dog * (z * sg)
        dz = dog * (n * w) * (sg * (1.0 + z * (1.0 - sg)))
        dn = don * w
        d_o = rstd * (dn - n * jnp.mean(dn * n, axis=1, keepdims=True))
        return d_o, dz, jnp.sum(don * n, axis=0, keepdims=True)

    hs = ((tr, HEAD_DIM), lambda i, hh: (i, hh))
    return _rowcall("gdn_post_bwd", body, (t // tr, h),
                    [(do_gdn,) + hs, (o,) + hs,
                     (proj_b, (tr, HEAD_DIM), lambda i, hh: (i, 4 * h + hh)), _full(gnw)],
                    [((t, d), F32) + hs + ("set",), ((t, d), BF16) + hs + ("set",),
                     ((1, HEAD_DIM), F32, (1, HEAD_DIM), lambda i, hh: (0, 0), "acc_all")])


def _seg(arr, tr, d, seg):
    return (arr, (tr, d), lambda i: (i, seg))


def _sb_post_fwd(o_raw, proj_b):
    t, d = o_raw.shape
    tr = _pick(t, TR_WIDE)

    def body(idx, o_v, z):
        return (o_v * (z * _sigmoid(z)),)

    return _rowcall("sb_post_fwd", body, (t // tr,), [_seg(o_raw, tr, d, 0), _seg(proj_b, tr, d, 0)],
                    [((t, d), BF16, (tr, d), lambda i: (i, 0), "set")])[0]


def _sb_post_bwd(do_sb, o_raw, proj_b):
    t, d = o_raw.shape
    tr = _pick(t, TR_WIDE)

    def body(idx, dos, o_v, z):
        sg = _sigmoid(z)
        return dos * (z * sg), dos * o_v * (sg * (1.0 + z * (1.0 - sg)))

    out = ((t, d), BF16, (tr, d), lambda i: (i, 0), "set")
    return _rowcall("sb_post_bwd", body, (t // tr,),
                    [_seg(do_sb, tr, d, 0), _seg(o_raw, tr, d, 0), _seg(proj_b, tr, d, 0)],
                    [out, out])


def _merge_fwd(proj_b, p_sb, p_gdn):
    t, d = p_sb.shape
    tr = _pick(t, TR_WIDE)

    def body(idx, m_sb, m_gdn, ps, pg):
        return (_sigmoid(m_sb) * ps + _sigmoid(m_gdn) * pg,)

    return _rowcall("merge_fwd", body, (t // tr,),
                    [_seg(proj_b, tr, d, 5), _seg(proj_b, tr, d, 6), _seg(p_sb, tr, d, 0),
                     _seg(p_gdn, tr, d, 0)],
                    [((t, d), BF16, (tr, d), lambda i: (i, 0), "set")])[0]


def _merge_bwd(dy, proj_b, p_sb, p_gdn):
    t, d = p_sb.shape
    tr = _pick(t, TR_WIDE)

    def body(idx, dy_v, m_sb, m_gdn, ps, pg):
        s1 = _sigmoid(m_sb)
        s2 = _sigmoid(m_gdn)
        return s1 * dy_v, s2 * dy_v, dy_v * ps * (s1 * (1.0 - s1)), dy_v * pg * (s2 * (1.0 - s2))

    out = ((t, d), BF16, (tr, d), lambda i: (i, 0), "set")
    return _rowcall("merge_bwd", body, (t // tr,),
                    [_seg(dy, tr, d, 0), _seg(proj_b, tr, d, 5), _seg(proj_b, tr, d, 6),
                     _seg(p_sb, tr, d, 0), _seg(p_gdn, tr, d, 0)],
                    [out] * 4)


def _tail(x, r, target, gate, final_w):
    t, d = x.shape
    tr = _pick(t, TR_WIDE)

    def body(idx, x_v, r_v, tg, gt, fw):
        x2 = x_v + gt * r_v
        rstd = lax.rsqrt(jnp.mean(x2 * x2, axis=1, keepdims=True) + NORM_EPS)
        n = x2 * rstd
        diff = n * fw - tg
        loss = 0.5 * jnp.sum(jnp.mean(diff * diff, axis=1, keepdims=True), axis=0, keepdims=True)
        dout = diff * (1.0 / d)
        dn = dout * fw
        dx2 = rstd * (dn - n * jnp.mean(dn * n, axis=1, keepdims=True))
        return (dx2, gt * dx2, jnp.sum(dout * n, axis=0, keepdims=True),
                jnp.sum(dx2 * r_v, axis=0, keepdims=True), jnp.broadcast_to(loss, (1, LANES)))

    rb = ((tr, d), lambda i: (i, 0))
    vec = ((1, d), F32, (1, d), lambda i: (0, 0), "acc_all")
    return _rowcall("tail", body, (t // tr,),
                    [(x,) + rb, (r,) + rb, (target,) + rb, _full(gate), _full(final_w)],
                    [((t, d), F32) + rb + ("set",), ((t, d), BF16) + rb + ("set",), vec, vec,
                     ((1, LANES), F32, (1, LANES), lambda i: (0, 0), "acc_all")])


def _pad_to(a, rows, cols):
    return jnp.pad(a, ((0, rows - a.shape[0]), (0, cols - a.shape[1])))


def kernel(x, c, w_ada, b_ada, norm_w, w_in, gdn_conv_w, gdn_a_log, gdn_dt_bias, gdn_norm_w, w_proj_sb, w_proj_gdn, w_out, final_norm_w, loss_target, m_w_ada, m_b_ada, m_norm_w, m_w_in, m_gdn_conv_w, m_gdn_a_log, m_gdn_dt_bias, m_gdn_norm_w, m_w_proj_sb, m_w_proj_gdn, m_w_out, m_final_norm_w, v_w_ada, v_b_ada, v_norm_w, v_w_in, v_gdn_conv_w, v_gdn_a_log, v_gdn_dt_bias, v_gdn_norm_w, v_w_proj_sb, v_w_proj_gdn, v_w_out, v_final_norm_w):
    t, d = x.shape[1], x.shape[2]
    h = d // HEAD_DIM
    me = 4 * lax.axis_index("x") + 2 * lax.axis_index("y") + lax.axis_index("c")
    x2d = x[0]
    tgt = loss_target[0]
    ada_cols = w_ada.shape[2]
    in_cols = w_in.shape[2]
    rows_p = w_out.shape[1]

    w_in_all = _exchange("gather_w_in", w_in[0].astype(BF16), True)
    w_in_full = jnp.transpose(w_in_all, (1, 0, 2)).reshape(d, N_DEV * in_cols)
    w_main = jnp.concatenate([w_in_full[:, :8 * d], w_in_full[:, 8 * d + 2 * h:]], axis=1)
    w_g = _pad_to(w_in_full[:, 8 * d:8 * d + 2 * h], d, LANES)
    w_main_t = w_main.T
    w_g_t = w_g.T
    w_sq = jnp.stack([w_proj_sb[0], w_proj_gdn[0], w_out[0]]).astype(BF16)
    conv_all = _exchange("gather_conv", _pad_to(gdn_conv_w[0], 8, gdn_conv_w.shape[2]), True)
    conv_w8 = jnp.transpose(conv_all, (1, 0, 2)).reshape(8, 3 * d)
    c_all = _exchange("gather_c", _pad_to(c, 8, d), True)[:, 0, :]

    sc_all = c_all * _sigmoid(c_all)
    mod_part = _matmul("ada_fwd", _pad_to(sc_all, 16, d).astype(BF16), w_ada[0].astype(BF16), F32)
    mod_part = mod_part[:N_DEV] + lax.dynamic_slice(b_ada, (0, me * ada_cols), (1, ada_cols))
    mod_rows = _exchange("a2a_mod", _pad_to(mod_part, 8, ada_cols).reshape(N_DEV, 1, ada_cols)
                         * jnp.ones((1, 8, 1), F32), False)
    mod = mod_rows[:, 0, :].reshape(1, 3 * d)
    shift, scale, gate = mod[:, :d], mod[:, d:2 * d], mod[:, 2 * d:]

    hmod = _norm_mod_fwd(x2d, shift, scale, norm_w)
    proj_a = _matmul("in_proj_a", hmod, w_main, BF16, n_cols=3 * d, col0=0)
    proj_b, w_sq_all = _matmul("in_proj_b", hmod, w_main, F32, n_cols=7 * d, col0=3 * d,
                               xchg=w_sq, gather=True)
    w_sq_full = jnp.transpose(w_sq_all, (1, 0, 2, 3)).reshape(3, d, d)
    wp_sb, wp_gdn, wo = w_sq_full[0], w_sq_full[1], w_sq_full[2]
    proj_c = _matmul("in_proj_c", hmod, w_g, F32)
    o_sb_raw, ctot = _sb_fwd(proj_a, h)
    o_sb = _sb_post_fwd(o_sb_raw, proj_b)
    ab = _pad_to(jnp.concatenate([gdn_a_log, gdn_dt_bias], axis=0), 8, LANES)
    gq, gk, gv, g_beta, g_cum = _gdn_pre_fwd(proj_b, proj_c, conv_w8, ab, h)
    o_gdn_raw, states, tms = _gdn_fwd(gq, gk, gv, g_beta, g_cum, h)
    o_gdn = _gdn_post_fwd(o_gdn_raw, proj_b, gdn_norm_w, h)
    p_sb = _matmul("proj_sb", o_sb, wp_sb, F32)
    p_gdn = _matmul("proj_gdn", o_gdn, wp_gdn, F32)
    y = _merge_fwd(proj_b, p_sb, p_gdn)
    r = _matmul("out_proj", y, wo, F32)
    dx2, dr, d_final_w, d_gate, loss_part = _tail(x2d, r, tgt, gate, final_norm_w.reshape(1, d))

    dy = _matmul("d_out_proj", dr, wo.T, F32)
    dw_out = _matmul("dw_out", y.T, dr, BF16)
    dp_sb, dp_gdn, dm_sb, dm_gdn = _merge_bwd(dy, proj_b, p_sb, p_gdn)
    do_sb = _matmul("d_proj_sb", dp_sb, wp_sb.T, F32)
    dw_p_sb = _matmul("dw_proj_sb", o_sb.T, dp_sb, BF16)
    do_gdn = _matmul("d_proj_gdn", dp_gdn, wp_gdn.T, F32)
    dw_p_gdn = _matmul("dw_proj_gdn", o_gdn.T, dp_gdn, BF16)
    do_sb_raw, d_sbz = _sb_post_bwd(do_sb, o_sb_raw, proj_b)
    d_sbq, d_sbk, d_sbv = _sb_bwd(proj_a, do_sb_raw, ctot, h)
    d_o_gdn, d_gz, d_gnw = _gdn_post_bwd(do_gdn, o_gdn_raw, proj_b, gdn_norm_w, h)
    dgq, dgk, dgv, dgbe, dgcum = _gdn_bwd(gq, gk, gv, g_beta, g_cum, d_o_gdn, states, tms, h)
    du_q, du_k, du_v, d_gates, d_ab = _gdn_pre_bwd(proj_b, proj_c, conv_w8, ab, dgq, dgk, dgv,
                                                  dgbe, dgcum, h)
    d_gq, d_gk, d_gv, dcw_q, dcw_k, dcw_v = _conv_bwd(proj_b, conv_w8, du_q, du_k, du_v, h)
    dproj = jnp.concatenate([d_sbq, d_sbk, d_sbv, d_sbz, d_gq, d_gk, d_gv, d_gz, dm_sb, dm_gdn],
                            axis=1)
    d_gates_b = d_gates.astype(BF16)
    hmod_t = hmod.T
    dw_sq = jnp.stack([dw_p_sb, dw_p_gdn, dw_out]).reshape(3, N_DEV, rows_p, d)
    dw_main, dw_sq_parts = _matmul("dw_in", hmod_t, dproj, BF16,
                                   xchg=jnp.transpose(dw_sq, (1, 0, 2, 3)))
    dw_g = _matmul("dw_in_g", hmod_t, d_gates_b, BF16)
    dw_in_full = jnp.concatenate([dw_main[:, :8 * d], dw_g[:, :2 * h], dw_main[:, 8 * d:]], axis=1)
    dh_a, dw_in_parts = _matmul(
        "d_in_proj", dproj, w_main_t, F32,
        xchg=jnp.transpose(dw_in_full.reshape(d, N_DEV, in_cols), (1, 0, 2)))
    dh_b = _matmul("d_in_proj_g", d_gates_b, w_g_t, F32)
    grad_x, d_shift, d_scale, d_norm_w = _norm_mod_bwd(dh_a, dh_b, x2d, dx2, scale, norm_w)

    dmod = jnp.concatenate([d_shift, d_scale, d_gate], axis=1)
    small = jnp.concatenate([dmod, d_norm_w, d_final_w, d_ab[:, :h], d_ab[:, h:2 * h], d_gnw,
                             loss_part], axis=1)
    n_small = small.shape[1]
    small_all = _exchange("gather_small", _pad_to(small, 8, n_small), True)[:, 0:1, :]
    small_w = jnp.concatenate([b_ada, norm_w, final_norm_w.reshape(1, d), gdn_a_log, gdn_dt_bias,
                               gdn_norm_w, jnp.zeros((1, LANES), F32)], axis=1)
    small_m = jnp.concatenate([m_b_ada, m_norm_w, m_final_norm_w.reshape(1, d), m_gdn_a_log,
                               m_gdn_dt_bias, m_gdn_norm_w, jnp.zeros((1, LANES), F32)], axis=1)
    small_v = jnp.concatenate([v_b_ada, v_norm_w, v_final_norm_w.reshape(1, d), v_gdn_a_log,
                               v_gdn_dt_bias, v_gdn_norm_w, jnp.ones((1, LANES), F32)], axis=1)
    s_g, s_d, s_m, s_v = _adamw("adamw_small", small_all, small_w, small_m, small_v)
    cuts = [3 * d, 4 * d, 5 * d, 5 * d + h, 5 * d + 2 * h, 5 * d + 2 * h + HEAD_DIM]

    def split_small(a):
        b, nw, fw, al, dtb, gn, _ = jnp.split(a, cuts, axis=1)
        return b, nw, fw.reshape(d), al, dtb, gn

    loss = s_g[0, cuts[-1]]

    dmod_all = small_all[:, 0, :3 * d]
    dmod_mine = lax.dynamic_slice(dmod_all, (0, me * ada_cols), (N_DEV, ada_cols))
    dw_ada = _matmul("dw_ada", _pad_to(sc_all.T, d, LANES).astype(BF16),
                     _pad_to(dmod_mine, LANES, ada_cols).astype(BF16), F32)
    ada = _adamw("adamw_ada", dw_ada[None], w_ada[0], m_w_ada[0], v_w_ada[0])

    win = _adamw("adamw_w_in", dw_in_parts, w_in[0], m_w_in[0], v_w_in[0])
    sq = _adamw("adamw_sq", dw_sq_parts.reshape(N_DEV, 3 * rows_p, d),
                jnp.concatenate([w_proj_sb[0], w_proj_gdn[0], w_out[0]], axis=0),
                jnp.concatenate([m_w_proj_sb[0], m_w_proj_gdn[0], m_w_out[0]], axis=0),
                jnp.concatenate([v_w_proj_sb[0], v_w_proj_gdn[0], v_w_out[0]], axis=0))
    dcw = jnp.concatenate([dcw_q, dcw_k, dcw_v], axis=1)
    cw_cols = gdn_conv_w.shape[2]
    dcw_parts = _exchange("a2a_dconv",
                          jnp.transpose(dcw.reshape(8, N_DEV, cw_cols), (1, 0, 2)), False)
    cw = _adamw("adamw_conv", dcw_parts, _pad_to(gdn_conv_w[0], 8, cw_cols),
                _pad_to(m_gdn_conv_w[0], 8, cw_cols),
                jnp.pad(v_gdn_conv_w[0], ((0, 8 - GDN_CONV), (0, 0)), constant_values=1.0))

    outs = [loss, grad_x[None]]
    for k_out in range(4):
        b, nw, fw, al, dtb, gn = split_small((s_g, s_d, s_m, s_v)[k_out])
        sq3 = sq[k_out].reshape(3, 1, rows_p, d)
        outs += [ada[k_out][None], b, nw, win[k_out][None], cw[k_out][None, :GDN_CONV], al, dtb, gn,
                 sq3[0], sq3[1], sq3[2], fw]
    return tuple(outs)
```

```python
import jax
import jax.numpy as jnp
from jax import lax
from jax.experimental import pallas as pl
from jax.experimental.pallas import tpu as pltpu

F32 = jnp.float32
BF16 = jnp.bfloat16
N_DEV = 8
HEAD_DIM = 128
LANES = 128
GDN_CHUNK = 64
GDN_CONV = 4
NORM_EPS = 1e-6
L2_EPS = 1e-6
ADAM_LR = 0.001
ADAM_B1 = 0.9
ADAM_B2 = 0.999
ADAM_EPS = 1e-08
ADAM_WD = 0.01
ADAM_STEP = 10
VMEM_LIMIT = 56 * 1024 * 1024
MESH = pl.DeviceIdType.MESH
NT = (((1,), (1,)), ((), ()))
TN = (((0,), (0,)), ((), ()))


def _pick(n, pref):
    t = min(pref, n)
    while n % t:
        t //= 2
    return t


def _sigmoid(x):
    return 1.0 / (1.0 + jnp.exp(-x))


def _bdot(a, b, dims=None):
    a = a.astype(BF16)
    b = b.astype(BF16)
    if dims is None:
        return jnp.dot(a, b, preferred_element_type=F32)
    return lax.dot_general(a, b, dims, preferred_element_type=F32)


def _split2(x):
    hi = x.astype(BF16)
    lo = (x - hi.astype(F32)).astype(BF16)
    return hi, lo


def _split3(x):
    p1 = x.astype(BF16)
    r = x - p1.astype(F32)
    p2 = r.astype(BF16)
    p3 = (r - p2.astype(F32)).astype(BF16)
    return p1, p2, p3


def _dot2(x, c, dims=None):
    hi, lo = _split2(x)
    return _bdot(hi, c, dims) + _bdot(lo, c, dims)


def _cdot3(c, x):
    p1, p2, p3 = _split3(x)
    return _bdot(c, p1) + _bdot(c, p2) + _bdot(c, p3)


def _dot6(a, b, dims=None):
    a1, a2, a3 = _split3(a)
    b1, b2, b3 = _split3(b)
    small = (_bdot(a1, b3, dims) + _bdot(a3, b1, dims)) + _bdot(a2, b2, dims)
    return _bdot(a1, b1, dims) + ((_bdot(a1, b2, dims) + _bdot(a2, b1, dims)) + small)


def _full(a):
    nd = a.ndim
    return (a, a.shape, lambda *idx: (0,) * nd)


def _rowcall(name, body, grid, ins, outs):
    n_in = len(ins)
    modes = [o[4] for o in outs]
    n_ax = len(grid)

    def kern(*refs):
        idx = tuple(pl.program_id(a) for a in range(n_ax))
        vals = body(idx, *[r[...] for r in refs[:n_in]])
        first_all = idx[0] == 0
        for a in range(1, n_ax):
            first_all = jnp.logical_and(first_all, idx[a] == 0)
        first_inner = idx[-1] == 0
        for r, v, mode in zip(refs[n_in:], vals, modes):
            v = v.astype(r.dtype)
            if mode == "set":
                r[...] = v
            else:
                first = first_all if mode == "acc_all" else first_inner

                @pl.when(first)
                def _(r=r, v=v):
                    r[...] = v

                @pl.when(jnp.logical_not(first))
                def _(r=r, v=v):
                    r[...] += v

    return pl.pallas_call(
        kern, name=name, grid=grid,
        in_specs=[pl.BlockSpec(b, m) for (_, b, m) in ins],
        out_specs=[pl.BlockSpec(o[2], o[3]) for o in outs],
        out_shape=[jax.ShapeDtypeStruct(o[0], o[1]) for o in outs],
        compiler_params=pltpu.CompilerParams(
            dimension_semantics=("arbitrary",) * n_ax, vmem_limit_bytes=VMEM_LIMIT),
    )(*[a for (a, _, _) in ins])


EXCHANGE_SEMS = [pltpu.SemaphoreType.DMA((N_DEV - 1,)), pltpu.SemaphoreType.DMA((N_DEV - 1,)),
                 pltpu.SemaphoreType.DMA]


def _exchange_shape(x, gather):
    return jax.ShapeDtypeStruct((N_DEV,) + tuple(x.shape if gather else x.shape[1:]), x.dtype)


def _exchange_copies(x_ref, out_ref, send_sems, recv_sems, local_sem, gather, start, wait):
    xi, yi, ci = lax.axis_index("x"), lax.axis_index("y"), lax.axis_index("c")
    me = 4 * xi + 2 * yi + ci
    mine = pltpu.make_async_copy(x_ref if gather else x_ref.at[me], out_ref.at[me], local_sem)
    if start:
        mine.start()
    for k in range(1, N_DEV):
        kx, ky, kc = (k >> 2) & 1, (k >> 1) & 1, k & 1
        pid = me ^ k
        src = x_ref if gather else x_ref.at[pid]
        if start:
            pltpu.make_async_remote_copy(
                src_ref=src, dst_ref=out_ref.at[me], send_sem=send_sems.at[k - 1],
                recv_sem=recv_sems.at[k - 1], device_id=(xi ^ kx, yi ^ ky, ci ^ kc),
                device_id_type=MESH).start()
        if wait:
            pltpu.make_async_remote_copy(
                src_ref=src, dst_ref=out_ref.at[pid], send_sem=send_sems.at[k - 1],
                recv_sem=recv_sems.at[k - 1], device_id=(xi, yi, ci), device_id_type=MESH).wait()
    if wait:
        mine.wait()


def _matmul(name, a, b, out_dtype, n_cols=None, col0=0, tm=1024, tn=1024, tk=2048, xchg=None,
            gather=False):
    m, k = a.shape
    n = b.shape[1] if n_cols is None else n_cols
    tm = _pick(m, tm)
    tn = _pick(n, tn)
    while col0 % tn:
        tn //= 2
    tk = _pick(k, tk)
    nk = k // tk
    cb = col0 // tn
    grid = (m // tm, n // tn, nk)

    def kern(a_ref, b_ref, *rest):
        if xchg is None:
            o_ref, acc_ref = rest
        else:
            x_ref, o_ref, xo_ref, acc_ref, send_sems, recv_sems, local_sem = rest
            step = (pl.program_id(0) * grid[1] + pl.program_id(1)) * nk + pl.program_id(2)

            @pl.when(step == 0)
            def _():
                _exchange_copies(x_ref, xo_ref, send_sems, recv_sems, local_sem, gather, True, False)

        kk = pl.program_id(2)
        part = jnp.dot(a_ref[...], b_ref[...], preferred_element_type=F32)
        if nk == 1:
            o_ref[...] = part.astype(o_ref.dtype)
        else:
            @pl.when(kk == 0)
            def _():
                acc_ref[...] = part

            @pl.when(kk > 0)
            def _():
                acc_ref[...] += part

            @pl.when(kk == nk - 1)
            def _():
                o_ref[...] = acc_ref[...].astype(o_ref.dtype)

        if xchg is not None:
            @pl.when(step == grid[0] * grid[1] * nk - 1)
            def _():
                _exchange_copies(x_ref, xo_ref, send_sems, recv_sems, local_sem, gather, False, True)

    in_specs = [pl.BlockSpec((tm, tk), lambda i, j, kk: (i, kk)),
                pl.BlockSpec((tk, tn), lambda i, j, kk: (kk, j + cb))]
    out_specs = [pl.BlockSpec((tm, tn), lambda i, j, kk: (i, j))]
    out_shape = [jax.ShapeDtypeStruct((m, n), out_dtype)]
    scratch = [pltpu.VMEM((tm, tn), F32)]
    args = (a, b)
    if xchg is not None:
        in_specs.append(pl.BlockSpec(memory_space=pl.ANY))
        out_specs.append(pl.BlockSpec(memory_space=pl.ANY))
        out_shape.append(_exchange_shape(xchg, gather))
        scratch += EXCHANGE_SEMS
        args = (a, b, xchg)
    res = pl.pallas_call(
        kern, name=name, grid=grid, in_specs=in_specs, out_specs=out_specs, out_shape=out_shape,
        scratch_shapes=scratch,
        compiler_params=pltpu.CompilerParams(
            dimension_semantics=("arbitrary", "arbitrary", "arbitrary"),
            vmem_limit_bytes=VMEM_LIMIT),
    )(*args)
    return res[0] if xchg is None else (res[0], res[1])


def _exchange(name, x, gather):
    def body(x_ref, out_ref, send_sems, recv_sems, local_sem):
        _exchange_copies(x_ref, out_ref, send_sems, recv_sems, local_sem, gather, True, True)

    return pl.pallas_call(
        body, name=name,
        out_shape=_exchange_shape(x, gather),
        in_specs=[pl.BlockSpec(memory_space=pl.ANY)],
        out_specs=pl.BlockSpec(memory_space=pl.ANY),
        scratch_shapes=EXCHANGE_SEMS,
    )(x)


def _adamw(name, parts, w, m, v):
    p, r, c = parts.shape
    tr = r if r <= 64 else _pick(r, 64)

    def body(idx, parts_v, w_v, m_v, v_v):
        g = parts_v[0].astype(F32)
        for s in range(1, p):
            g = g + parts_v[s].astype(F32)
        m2 = ADAM_B1 * m_v + (1.0 - ADAM_B1) * g
        v2 = ADAM_B2 * v_v + (1.0 - ADAM_B2) * (g * g)
        m_hat = m2 / (1.0 - ADAM_B1 ** ADAM_STEP)
        v_hat = v2 / (1.0 - ADAM_B2 ** ADAM_STEP)
        delta = -ADAM_LR * (m_hat / (jnp.sqrt(v_hat) + ADAM_EPS) + ADAM_WD * w_v)
        return g, delta, m2, v2

    rb = ((tr, c), lambda i: (i, 0))
    return _rowcall(
        name, body, (r // tr,),
        [(parts, (p, tr, c), lambda i: (0, i, 0)), (w,) + rb, (m,) + rb, (v,) + rb],
        [((r, c), F32) + rb + ("set",)] * 4)


TR_WIDE = 256


def _norm_mod_fwd(x, shift, scale, norm_w):
    t, d = x.shape
    tr = _pick(t, TR_WIDE)

    def body(idx, x_v, sh, sc, nw):
        rstd = lax.rsqrt(jnp.mean(x_v * x_v, axis=1, keepdims=True) + NORM_EPS)
        return ((x_v * rstd) * nw * (1.0 + sc) + sh,)

    rb = ((tr, d), lambda i: (i, 0))
    return _rowcall("norm_mod_fwd", body, (t // tr,),
                    [(x,) + rb, _full(shift), _full(scale), _full(norm_w)],
                    [((t, d), BF16) + rb + ("set",)])[0]


def _norm_mod_bwd(dh_a, dh_b, x, dx2, scale, norm_w):
    t, d = x.shape
    tr = _pick(t, TR_WIDE)

    def body(idx, dha, dhb, x_v, dx2_v, sc, nw):
        dh = dha + dhb
        rstd = lax.rsqrt(jnp.mean(x_v * x_v, axis=1, keepdims=True) + NORM_EPS)
        xn = x_v * rstd
        m1 = 1.0 + sc
        dxn = dh * nw * m1
        dx = rstd * (dxn - xn * jnp.mean(dxn * xn, axis=1, keepdims=True))
        dhx = dh * xn
        return (dx2_v + dx,
                jnp.sum(dh, axis=0, keepdims=True),
                jnp.sum(dhx * nw, axis=0, keepdims=True),
                jnp.sum(dhx * m1, axis=0, keepdims=True))

    rb = ((tr, d), lambda i: (i, 0))
    vec = ((1, d), F32, (1, d), lambda i: (0, 0), "acc_all")
    return _rowcall("norm_mod_bwd", body, (t // tr,),
                    [(dh_a,) + rb, (dh_b,) + rb, (x,) + rb, (dx2,) + rb, _full(scale), _full(norm_w)],
                    [((t, d), F32) + rb + ("set",), vec, vec, vec])


SB_TILE = 512


SB_SCAN = 256
SB_DEAD = 105.0


def _neg_log_not_beta(z):
    return jnp.maximum(z, 0.0) + jnp.log(1.0 + jnp.exp(-jnp.abs(z)))


def _key_scan(x, tri, later):
    n_blk = x.shape[1] // SB_SCAN
    parts = [x[:, g * SB_SCAN:(g + 1) * SB_SCAN] for g in range(n_blk)]
    sums = [jnp.sum(p, axis=1, keepdims=True) for p in parts]
    outs = []
    for g in range(n_blk):
        o = _dot2(parts[g], tri)
        others = range(g + 1, n_blk) if later else range(g)
        for g2 in others:
            o = o + sums[g2]
        outs.append(o)
    total = sums[0]
    for s in sums[1:]:
        total = total + s
    return (outs[0] if n_blk == 1 else jnp.concatenate(outs, axis=1)), total


def _scan_tri(kind):
    row = lax.broadcasted_iota(jnp.int32, (SB_SCAN, SB_SCAN), 0)
    col = lax.broadcasted_iota(jnp.int32, (SB_SCAN, SB_SCAN), 1)
    return {"after": row > col, "upto": row <= col, "before": row < col}[kind].astype(BF16)


def _sb_fwd(qkv, n_heads):
    t = qkv.shape[0]
    tq = _pick(t, SB_TILE)
    nq = t // tq
    scale = HEAD_DIM ** -0.5

    def kern(q_ref, k_ref, v_ref, o_ref, c_ref, nd_ref):
        i_blk = pl.program_id(1)
        row = lax.broadcasted_iota(jnp.int32, (tq, tq), 0)
        col = lax.broadcasted_iota(jnp.int32, (tq, tq), 1)
        causal = col < row
        after = _scan_tri("after")
        qb = q_ref[...]

        def tile(j_blk, c, acc, diag):
            r0 = pl.multiple_of(j_blk * tq, tq)
            kb = k_ref[pl.ds(r0, tq), :]
            vb = v_ref[pl.ds(r0, tq), :]
            z = lax.dot_general(qb, kb, NT, preferred_element_type=F32) * scale
            n = _neg_log_not_beta(z)
            if diag:
                n = jnp.where(causal, n, 0.0)
            later, total = _key_scan(n, after, True)
            a = jnp.exp(z - (n + later + c))
            if diag:
                a = jnp.where(causal, a, 0.0)
            acc = acc + jnp.dot(a.astype(BF16), vb, preferred_element_type=F32)
            return c + total, acc

        c, acc = tile(i_blk, jnp.zeros((tq, 1), F32), jnp.zeros((tq, HEAD_DIM), F32), True)

        def more(st):
            return jnp.logical_and(st[0] < i_blk, jnp.min(st[1]) <= SB_DEAD)

        def step(st):
            c2, acc2 = tile(i_blk - 1 - st[0], st[1], st[2], False)
            return st[0] + 1, c2, acc2

        n_done, c, acc = lax.while_loop(more, step, (jnp.int32(0), c, acc))
        o_ref[...] = acc
        c_ref[...] = jnp.broadcast_to(c, (tq, HEAD_DIM))
        nd_ref[...] = jnp.full((8, LANES), n_done.astype(F32))

    h = n_heads
    return pl.pallas_call(
        kern, name="sb_fwd", grid=(h, nq),
        in_specs=[pl.BlockSpec((tq, HEAD_DIM), lambda hh, i: (i, hh)),
                  pl.BlockSpec((t, HEAD_DIM), lambda hh, i: (0, h + hh)),
                  pl.BlockSpec((t, HEAD_DIM), lambda hh, i: (0, 2 * h + hh))],
        out_specs=[pl.BlockSpec((tq, HEAD_DIM), lambda hh, i: (i, hh)),
                   pl.BlockSpec((tq, HEAD_DIM), lambda hh, i: (i, hh)),
                   pl.BlockSpec((None, None, 8, LANES), lambda hh, i: (hh, i, 0, 0))],
        out_shape=[jax.ShapeDtypeStruct((t, h * HEAD_DIM), F32),
                   jax.ShapeDtypeStruct((t, h * HEAD_DIM), F32),
                   jax.ShapeDtypeStruct((h, nq, 8, LANES), F32)],
        compiler_params=pltpu.CompilerParams(
            dimension_semantics=("arbitrary", "arbitrary"), vmem_limit_bytes=VMEM_LIMIT),
    )(qkv, qkv, qkv)


def _sb_bwd(qkv, do, ctot, n_visited, n_heads):
    t = qkv.shape[0]
    tq = _pick(t, SB_TILE)
    nq = t // tq
    scale = HEAD_DIM ** -0.5

    def kern(q_ref, do_ref, c_ref, nd_ref, k_ref, v_ref, dq_ref, dk_ref, dv_ref, dk_acc, dv_acc):
        i_blk = pl.program_id(1)
        row = lax.broadcasted_iota(jnp.int32, (tq, tq), 0)
        col = lax.broadcasted_iota(jnp.int32, (tq, tq), 1)
        causal = col < row
        upto = _scan_tri("upto")
        before = _scan_tri("before")
        qb = q_ref[...]
        dob = do_ref[...]
        ctot_v = c_ref[...][:, 0:1]

        @pl.when(i_blk == 0)
        def _():
            dk_acc[...] = jnp.zeros_like(dk_acc)
            dv_acc[...] = jnp.zeros_like(dv_acc)

        def tile(j_blk, cl, pe, dq, diag):
            r0 = pl.multiple_of(j_blk * tq, tq)
            kb = k_ref[pl.ds(r0, tq), :]
            vb = v_ref[pl.ds(r0, tq), :]
            z = lax.dot_general(qb, kb, NT, preferred_element_type=F32) * scale
            n = _neg_log_not_beta(z)
            if diag:
                n = jnp.where(causal, n, 0.0)
            upto_s, n_total = _key_scan(n, upto, False)
            lb = z - n
            a = jnp.exp(lb - ((ctot_v - cl) - upto_s))
            if diag:
                a = jnp.where(causal, a, 0.0)
            da = lax.dot_general(dob, vb, NT, preferred_element_type=F32)
            e = da * a
            before_s, e_total = _key_scan(e, before, False)
            beta = jnp.exp(lb)
            dz = (e - beta * (e + (pe + before_s))) * scale
            if diag:
                dz = jnp.where(causal, dz, 0.0)
            dzb = dz.astype(BF16)
            dq = dq + jnp.dot(dzb, kb, preferred_element_type=F32)
            dk_acc[pl.ds(r0, tq), :] += lax.dot_general(dzb, qb, TN, preferred_element_type=F32)
            dv_acc[pl.ds(r0, tq), :] += lax.dot_general(a.astype(BF16), dob, TN,
                                                        preferred_element_type=F32)
            return cl + n_total, pe + e_total, dq

        zero = jnp.zeros((tq, 1), F32)
        first = i_blk - jnp.max(nd_ref[...]).astype(jnp.int32)
        cl, pe, dq = lax.fori_loop(first, i_blk, lambda j, s: tile(j, s[0], s[1], s[2], False),
                                   (zero, zero, jnp.zeros((tq, HEAD_DIM), F32)))
        cl, pe, dq = tile(i_blk, cl, pe, dq, True)
        dq_ref[...] = dq.astype(BF16)

        @pl.when(i_blk == nq - 1)
        def _():
            dk_ref[...] = dk_acc[...].astype(BF16)
            dv_ref[...] = dv_acc[...].astype(BF16)

    h = n_heads
    qspec = pl.BlockSpec((tq, HEAD_DIM), lambda hh, i: (i, hh))
    hspec = pl.BlockSpec((t, HEAD_DIM), lambda hh, i: (0, hh))
    out = jax.ShapeDtypeStruct((t, h * HEAD_DIM), BF16)
    return pl.pallas_call(
        kern, name="sb_bwd", grid=(h, nq),
        in_specs=[qspec, qspec, qspec,
                  pl.BlockSpec((None, None, 8, LANES), lambda hh, i: (hh, i, 0, 0)),
                  pl.BlockSpec((t, HEAD_DIM), lambda hh, i: (0, h + hh)),
                  pl.BlockSpec((t, HEAD_DIM), lambda hh, i: (0, 2 * h + hh))],
        out_specs=[qspec, hspec, hspec],
        out_shape=[out, out, out],
        scratch_shapes=[pltpu.VMEM((t, HEAD_DIM), F32), pltpu.VMEM((t, HEAD_DIM), F32)],
        compiler_params=pltpu.CompilerParams(
            dimension_semantics=("arbitrary", "arbitrary"), vmem_limit_bytes=VMEM_LIMIT),
    )(qkv, do, ctot, n_visited, qkv, qkv)


GDN_ROWS = 512
GDN_HEADS_PER_STEP = 2
TR_HEAD = 512


def _shift_rows(cur, halo, k, back):
    n = cur.shape[0]
    ext = jnp.concatenate([cur, halo], axis=0)
    return pltpu.roll(ext, k if back else n + 8 - k, 0)[:n]


def _conv_fwd(cur, halo, w):
    out = cur * w[GDN_CONV - 1:GDN_CONV, :]
    for i in range(GDN_CONV - 1):
        out = out + _shift_rows(cur, halo, GDN_CONV - 1 - i, True) * w[i:i + 1, :]
    return out


def _chunk_tri(n, upper):
    row = lax.broadcasted_iota(jnp.int32, (n, n), 0)
    col = lax.broadcasted_iota(jnp.int32, (n, n), 1)
    same = (row // GDN_CHUNK) == (col // GDN_CHUNK)
    tri = (col >= row) if upper else (col <= row)
    return jnp.logical_and(same, tri).astype(BF16)


def _lane_pick(x, lane):
    idx = lax.broadcasted_iota(jnp.int32, x.shape, 1)
    return jnp.sum(jnp.where(idx == lane, x, 0.0), axis=1, keepdims=True)


def _softplus(x):
    y = jnp.exp(-jnp.abs(x))
    u = 1.0 + y
    log1p = jnp.where(u == 1.0, y, jnp.log(u) * (y / jnp.where(u == 1.0, 1.0, u - 1.0)))
    return jnp.maximum(x, 0.0) + log1p


def _gdn_specs(t, tr, h, proj_seg0):
    def slab(seg):
        return ((tr, HEAD_DIM), lambda i, hh: (i, seg * h + hh))

    def halo_before(seg):
        return ((8, HEAD_DIM), lambda i, hh: (jnp.maximum(i * (tr // 8) - 1, 0), seg * h + hh))

    return slab, halo_before


def _gdn_pre_fwd(proj_b, proj_c, conv_w8, ab, n_heads):
    t = proj_b.shape[0]
    h = n_heads
    d = h * HEAD_DIM
    tr = _pick(t, TR_HEAD)
    slab, halo = _gdn_specs(t, tr, h, 1)
    scale = HEAD_DIM ** -0.5

    def body(idx, q_c, q_h, k_c, k_h, v_c, v_h, wq, wk, wv, gbga, ab_v):
        i, hh = idx
        live = (i > 0).astype(F32)
        outs = []
        for cur, hal, w, kind in ((q_c, q_h, wq, "q"), (k_c, k_h, wk, "k"), (v_c, v_h, wv, "v")):
            u = _conv_fwd(cur, hal * live, w)
            s = u * _sigmoid(u)
            if kind != "v":
                s = s * lax.rsqrt(jnp.sum(s * s, axis=1, keepdims=True) + L2_EPS)
            if kind == "q":
                s = s * scale
            outs.append(s)
        beta = _sigmoid(_lane_pick(gbga, hh))
        a_log = _lane_pick(ab_v[0:1, :], hh)
        dt = _lane_pick(ab_v[1:2, :], hh)
        g = -jnp.exp(a_log) * _softplus(_lane_pick(gbga, h + hh) + dt)
        g_rep = jnp.broadcast_to(g, (tr, HEAD_DIM))
        big_g = _cdot3(_chunk_tri(tr, False), g_rep)
        return outs + [jnp.broadcast_to(beta, (tr, HEAD_DIM)), big_g]

    wspec = lambda seg: ((8, HEAD_DIM), lambda i, hh: (0, seg * h + hh))
    out = ((t, d), F32, (tr, HEAD_DIM), lambda i, hh: (i, hh), "set")
    return _rowcall(
        "gdn_pre_fwd", body, (t // tr, h),
        [(proj_b,) + slab(1), (proj_b,) + halo(1), (proj_b,) + slab(2), (proj_b,) + halo(2),
         (proj_b,) + slab(3), (proj_b,) + halo(3),
         (conv_w8,) + wspec(0), (conv_w8,) + wspec(1), (conv_w8,) + wspec(2),
         (proj_c, (tr, LANES), lambda i, hh: (i, 0)), _full(ab)],
        [out] * 5)


def _gdn_consts():
    row = lax.broadcasted_iota(jnp.int32, (GDN_CHUNK, GDN_CHUNK), 0)
    col = lax.broadcasted_iota(jnp.int32, (GDN_CHUNK, GDN_CHUNK), 1)
    lane = lax.broadcasted_iota(jnp.int32, (GDN_CHUNK, HEAD_DIM), 1)
    return row > col, row >= col, (row == col).astype(F32), (lane == 0).astype(BF16)


def _gdn_local(q, k, v, be, ge, consts):
    lower, tril, eye, sel = consts
    kb_ = k * be
    vb_ = v * be
    e_g = jnp.exp(ge)
    kg = kb_ * e_g
    p1, p2, p3 = _split3(ge)
    g_i = _bdot(p1, sel, NT) + _bdot(p2, sel, NT) + _bdot(p3, sel, NT)
    g_j = _bdot(sel, p1, NT) + _bdot(sel, p2, NT) + _bdot(sel, p3, NT)
    dec = jnp.where(tril, jnp.exp(jnp.minimum(g_i - g_j, 0.0)), 0.0)
    kk = _bdot(kb_, k, NT)
    qk = _bdot(q, k, NT)
    g_last = jnp.min(ge, axis=0, keepdims=True)
    kdec_f = jnp.exp(g_last - ge)
    return dict(kb=kb_, vb=vb_, e_g=e_g, kg=kg, dec=dec, kk=kk, qk=qk, kdec_f=kdec_f,
                k_dec=k * kdec_f, q_dec=q * e_g, gamma=jnp.exp(g_last),
                intra=jnp.where(tril, qk * dec, 0.0))


def _wy_lower_t(k, be, ge, sel):
    row = lax.broadcasted_iota(jnp.int32, (GDN_CHUNK, GDN_CHUNK), 0)
    col = lax.broadcasted_iota(jnp.int32, (GDN_CHUNK, GDN_CHUNK), 1)
    p1, p2, p3 = _split3(ge)
    g_row = _bdot(p1, sel, NT) + _bdot(p2, sel, NT) + _bdot(p3, sel, NT)
    g_col = _bdot(sel, p1, NT) + _bdot(sel, p2, NT) + _bdot(sel, p3, NT)
    dec_t = jnp.exp(jnp.minimum(g_col - g_row, 0.0))
    return jnp.where(col > row, _bdot(k, k * be, NT) * dec_t, 0.0)


def _unit_lower_inverse(lw_t):
    n = lw_t.shape[0]
    n_slab = GDN_CHUNK // 8
    row = lax.broadcasted_iota(jnp.int32, (n, 8, GDN_CHUNK), 1)
    col = lax.broadcasted_iota(jnp.int32, (n, 8, GDN_CHUNK), 2)
    unit = lax.broadcasted_iota(jnp.int32, (1, 1, GDN_CHUNK), 2)
    lw = [lw_t[:, 8 * g:8 * g + 8, :] for g in range(n_slab)]
    inv = [(col == row + 8 * g).astype(F32) for g in range(n_slab)]
    for i in range(1, GDN_CHUNK):
        acc = lw[0][:, :, i:i + 1] * inv[0]
        for g in range(1, (i - 1) // 8 + 1):
            acc = acc + lw[g][:, :, i:i + 1] * inv[g]
        new_row = (unit == i).astype(F32) - jnp.sum(acc, axis=1, keepdims=True)
        inv[i // 8] = jnp.where(row == i % 8, new_row, inv[i // 8])
    return jnp.concatenate(inv, axis=1)


def _gdn_fwd(q, k, v, be, ge, n_heads):
    t = q.shape[0]
    h = n_heads
    hg = GDN_HEADS_PER_STEP
    tr = _pick(t, GDN_ROWS)
    nc = tr // GDN_CHUNK

    def kern(q_ref, k_ref, v_ref, b_ref, g_ref, o_ref, s_ref, tm_ref, state):
        consts = _gdn_consts()
        lower, tril, eye, sel = consts
        lanes = [pl.ds(hs * HEAD_DIM, HEAD_DIM) for hs in range(hg)]

        @pl.when(pl.program_id(1) == 0)
        def _():
            state[...] = jnp.zeros_like(state)

        lw_t = []
        for hs in range(hg):
            for ci in range(nc):
                rows = pl.ds(ci * GDN_CHUNK, GDN_CHUNK)
                lw_t.append(_wy_lower_t(k_ref[rows, lanes[hs]], b_ref[rows, lanes[hs]],
                                        g_ref[rows, lanes[hs]], sel))
        t_all = _unit_lower_inverse(jnp.stack(lw_t))
        for hs in range(hg):
            for ci in range(nc):
                tm_ref[hs, pl.ds(ci * GDN_CHUNK, GDN_CHUNK), :] = t_all[hs * nc + ci]

        def chunk(ci, carry):
            rows = pl.ds(pl.multiple_of(ci * GDN_CHUNK, GDN_CHUNK), GDN_CHUNK)
            results = []
            for hs in range(hg):
                ln = lanes[hs]
                loc = _gdn_local(q_ref[rows, ln], k_ref[rows, ln], v_ref[rows, ln], b_ref[rows, ln],
                                 g_ref[rows, ln], consts)
                t_mat = tm_ref[hs, rows, :]
                u = _bdot(t_mat, loc["vb"])
                w = _bdot(t_mat, loc["kg"])
                s = state[hs]
                v_new = u - _bdot(w, s)
                results.append((s, _bdot(loc["q_dec"], s) + _bdot(loc["intra"], v_new),
                                s * loc["gamma"] + _bdot(loc["k_dec"], v_new, TN)))
            for hs in range(hg):
                s_ref[hs, ci] = results[hs][0]
                o_ref[rows, lanes[hs]] = results[hs][1]
                state[hs] = results[hs][2]
            return carry

        lax.fori_loop(0, nc, chunk, 0)

    slab = pl.BlockSpec((tr, hg * HEAD_DIM), lambda hp, j: (j, hp))
    return pl.pallas_call(
        kern, name="gdn_fwd", grid=(h // hg, t // tr),
        in_specs=[slab] * 5,
        out_specs=[slab,
                   pl.BlockSpec((hg, nc, HEAD_DIM, HEAD_DIM), lambda hp, j: (hp, j, 0, 0)),
                   pl.BlockSpec((hg, tr, GDN_CHUNK), lambda hp, j: (hp, j, 0))],
        out_shape=[jax.ShapeDtypeStruct((t, h * HEAD_DIM), F32),
                   jax.ShapeDtypeStruct((h, t // GDN_CHUNK, HEAD_DIM, HEAD_DIM), F32),
                   jax.ShapeDtypeStruct((h, t, GDN_CHUNK), F32)],
        scratch_shapes=[pltpu.VMEM((hg, HEAD_DIM, HEAD_DIM), F32)],
        compiler_params=pltpu.CompilerParams(
            dimension_semantics=("arbitrary", "arbitrary"), vmem_limit_bytes=VMEM_LIMIT),
    )(q, k, v, be, ge)


def _gdn_bwd(q, k, v, be, ge, do, states, tms, n_heads):
    t = q.shape[0]
    h = n_heads
    hg = GDN_HEADS_PER_STEP
    tr = _pick(t, GDN_ROWS)
    nc = tr // GDN_CHUNK
    nj = t // tr

    def kern(q_ref, k_ref, v_ref, b_ref, g_ref, do_ref, s_ref, tm_ref,
             dq_ref, dk_ref, dv_ref, db_ref, dg_ref, dstate):
        consts = _gdn_consts()
        lower, tril, eye, sel = consts
        last_row = lax.broadcasted_iota(jnp.int32, (GDN_CHUNK, HEAD_DIM), 0) == GDN_CHUNK - 1

        @pl.when(pl.program_id(1) == 0)
        def _():
            dstate[...] = jnp.zeros_like(dstate)

        def lsum(x):
            return jnp.sum(x, axis=1, keepdims=True)

        def chunk(cc, carry):
            ci = nc - 1 - cc
            rows = pl.ds(pl.multiple_of(ci * GDN_CHUNK, GDN_CHUNK), GDN_CHUNK)
            results = [one_head(ci, hs) for hs in range(hg)]
            for hs in range(hg):
                ln = pl.ds(hs * HEAD_DIM, HEAD_DIM)
                dstate[hs] = results[hs][0]
                for ref, val in zip((dq_ref, dk_ref, dv_ref, db_ref, dg_ref), results[hs][1:]):
                    ref[rows, ln] = val
            return carry

        def one_head(ci, hs):
            rows = pl.ds(pl.multiple_of(ci * GDN_CHUNK, GDN_CHUNK), GDN_CHUNK)
            ln = pl.ds(hs * HEAD_DIM, HEAD_DIM)
            qv, kv, vv, bev = q_ref[rows, ln], k_ref[rows, ln], v_ref[rows, ln], b_ref[rows, ln]
            loc = _gdn_local(qv, kv, vv, bev, g_ref[rows, ln], consts)
            t_mat = tm_ref[hs, rows, :]
            s = s_ref[hs, ci]
            d_o = do_ref[rows, ln]
            d_s = dstate[hs]
            u = _bdot(t_mat, loc["vb"])
            w = _bdot(t_mat, loc["kg"])
            v_new = u - _bdot(w, s)
            dv_new = _bdot(loc["intra"], d_o, TN) + _bdot(loc["k_dec"], d_s)
            d_intra = jnp.where(tril, _bdot(d_o, v_new, NT), 0.0)
            dq_dec = _bdot(d_o, s, NT)
            dk_dec = _bdot(v_new, d_s, NT)
            dgamma = jnp.sum(lsum(d_s * s), axis=0, keepdims=True)
            dw = -_bdot(dv_new, s, NT)
            d_s_new = (_bdot(loc["q_dec"], d_o, TN) + loc["gamma"] * d_s
                       - _bdot(w, dv_new, TN))
            dtm = _bdot(dv_new, loc["vb"], NT) + _bdot(dw, loc["kg"], NT)
            dvb = _bdot(t_mat, dv_new, TN)
            dkg = _bdot(t_mat, dw, TN)
            dlw = jnp.where(lower, -_dot6(t_mat, _dot6(dtm, t_mat, NT), TN), 0.0)
            dkk = dlw * loc["dec"]
            dqk = d_intra * loc["dec"]
            ddec = dlw * loc["kk"] + d_intra * loc["qk"]
            dkb = _bdot(dkk, kv) + dkg * loc["e_g"]
            dk = (_bdot(dkk, loc["kb"], TN) + _bdot(dqk, qv, TN) + dk_dec * loc["kdec_f"]
                  + dkb * bev)
            dq = _bdot(dqk, kv) + dq_dec * loc["e_g"]
            dgd = ddec * loc["dec"]
            r_kdec = lsum(dk_dec * loc["k_dec"])
            col_sums = lsum(_dot2(dgd, sel, TN))
            d_big_g = (lsum(dgd) - col_sums
                       + lsum(dq_dec * loc["q_dec"]) - r_kdec + lsum(dkg * loc["kg"]))
            d_last = jnp.sum(r_kdec, axis=0, keepdims=True) + dgamma * loc["gamma"][:, 0:1]
            d_big_g = jnp.broadcast_to(d_big_g, (GDN_CHUNK, HEAD_DIM)) + jnp.where(last_row, d_last,
                                                                                  0.0)
            d_beta = jnp.broadcast_to(lsum(dkb * kv) + lsum(dvb * vv), (GDN_CHUNK, HEAD_DIM))
            return d_s_new, dq, dk, dvb * bev, d_beta, d_big_g

        lax.fori_loop(0, nc, chunk, 0)

    slab = pl.BlockSpec((tr, hg * HEAD_DIM), lambda hp, j: (nj - 1 - j, hp))
    out = jax.ShapeDtypeStruct((t, h * HEAD_DIM), F32)
    return pl.pallas_call(
        kern, name="gdn_bwd", grid=(h // hg, nj),
        in_specs=[slab] * 6 + [
            pl.BlockSpec((hg, nc, HEAD_DIM, HEAD_DIM), lambda hp, j: (hp, nj - 1 - j, 0, 0)),
            pl.BlockSpec((hg, tr, GDN_CHUNK), lambda hp, j: (hp, nj - 1 - j, 0))],
        out_specs=[slab] * 5,
        out_shape=[out] * 5,
        scratch_shapes=[pltpu.VMEM((hg, HEAD_DIM, HEAD_DIM), F32)],
        compiler_params=pltpu.CompilerParams(
            dimension_semantics=("arbitrary", "arbitrary"), vmem_limit_bytes=VMEM_LIMIT),
    )(q, k, v, be, ge, do, states, tms)


def _gdn_pre_bwd(proj_b, proj_c, conv_w8, ab, dq, dk, dv, dbe, dge, n_heads):
    t = proj_b.shape[0]
    h = n_heads
    d = h * HEAD_DIM
    tr = _pick(t, TR_HEAD)
    slab, halo = _gdn_specs(t, tr, h, 1)
    scale = HEAD_DIM ** -0.5

    def body(idx, q_c, q_h, k_c, k_h, v_c, v_h, wq, wk, wv, gbga, ab_v, dq_v, dk_v, dv_v, dbe_v,
             dge_v):
        i, hh = idx
        live = (i > 0).astype(F32)
        outs = []
        for cur, hal, w, dy, kind in ((q_c, q_h, wq, dq_v, "q"), (k_c, k_h, wk, dk_v, "k"),
                                      (v_c, v_h, wv, dv_v, "v")):
            u = _conv_fwd(cur, hal * live, w)
            sg = _sigmoid(u)
            if kind == "v":
                ds = dy
            else:
                s = u * sg
                r = lax.rsqrt(jnp.sum(s * s, axis=1, keepdims=True) + L2_EPS)
                y = s * r
                if kind == "q":
                    dy = dy * scale
                ds = r * (dy - y * jnp.sum(dy * y, axis=1, keepdims=True))
            outs.append(ds * (sg * (1.0 + u * (1.0 - sg))))
        lane = lax.broadcasted_iota(jnp.int32, (tr, LANES), 1)
        lane1 = lax.broadcasted_iota(jnp.int32, (1, LANES), 1)
        beta = _sigmoid(_lane_pick(gbga, hh))
        a_neg = -jnp.exp(_lane_pick(ab_v[0:1, :], hh))
        xg = _lane_pick(gbga, h + hh) + _lane_pick(ab_v[1:2, :], hh)
        g = a_neg * _softplus(xg)
        dgb = dbe_v * (beta * (1.0 - beta))
        dg = _cdot3(_chunk_tri(tr, True), dge_v)
        dga = dg * (a_neg * _sigmoid(xg))
        d_gates = jnp.where(lane == hh, dgb, 0.0) + jnp.where(lane == h + hh, dga, 0.0)
        d_ab = (jnp.where(lane1 == hh, jnp.sum(dg * g, axis=0, keepdims=True), 0.0)
                + jnp.where(lane1 == h + hh, jnp.sum(dga, axis=0, keepdims=True), 0.0))
        return outs + [d_gates, d_ab]

    wspec = lambda seg: ((8, HEAD_DIM), lambda i, hh: (0, seg * h + hh))
    hs = ((tr, HEAD_DIM), lambda i, hh: (i, hh))
    out = ((t, d), F32) + hs + ("set",)
    return _rowcall(
        "gdn_pre_bwd", body, (t // tr, h),
        [(proj_b,) + slab(1), (proj_b,) + halo(1), (proj_b,) + slab(2), (proj_b,) + halo(2),
         (proj_b,) + slab(3), (proj_b,) + halo(3),
         (conv_w8,) + wspec(0), (conv_w8,) + wspec(1), (conv_w8,) + wspec(2),
         (proj_c, (tr, LANES), lambda i, hh: (i, 0)), _full(ab),
         (dq,) + hs, (dk,) + hs, (dv,) + hs, (dbe,) + hs, (dge,) + hs],
        [out, out, out,
         ((t, LANES), F32, (tr, LANES), lambda i, hh: (i, 0), "acc_inner"),
         ((1, LANES), F32, (1, LANES), lambda i, hh: (0, 0), "acc_all")])


def _conv_bwd(proj_b, conv_w8, du_q, du_k, du_v, n_heads):
    t = proj_b.shape[0]
    h = n_heads
    d = h * HEAD_DIM
    tr = _pick(t, TR_HEAD)
    ni = t // tr

    def body(idx, q_c, q_h, k_c, k_h, v_c, v_h, wq, wk, wv, uq, uq_n, uk, uk_n, uv, uv_n):
        hh, i = idx
        live_b = (i > 0).astype(F32)
        live_a = (i < ni - 1).astype(F32)
        d_ins, d_ws = [], []
        for cur, hal, w, du, du_n in ((q_c, q_h, wq, uq, uq_n), (k_c, k_h, wk, uk, uk_n),
                                      (v_c, v_h, wv, uv, uv_n)):
            hal = hal * live_b
            du_n = du_n * live_a
            d_in = du * w[GDN_CONV - 1:GDN_CONV, :]
            rows = [jnp.sum(du * cur, axis=0, keepdims=True)]
            for i_tap in range(GDN_CONV - 2, -1, -1):
                kshift = GDN_CONV - 1 - i_tap
                d_in = d_in + _shift_rows(du, du_n, kshift, False) * w[i_tap:i_tap + 1, :]
                rows.insert(0, jnp.sum(du * _shift_rows(cur, hal, kshift, True), axis=0,
                                       keepdims=True))
            d_ins.append(d_in)
            tap = lax.broadcasted_iota(jnp.int32, (8, HEAD_DIM), 0)
            d_w = jnp.zeros((8, HEAD_DIM), F32)
            for i_tap in range(GDN_CONV):
                d_w = d_w + jnp.where(tap == i_tap, rows[i_tap], 0.0)
            d_ws.append(d_w)
        return d_ins + d_ws

    def slab(seg):
        return ((tr, HEAD_DIM), lambda hh, i: (i, seg * h + hh))

    def halo_b(seg):
        return ((8, HEAD_DIM), lambda hh, i: (jnp.maximum(i * (tr // 8) - 1, 0), seg * h + hh))

    hs = ((tr, HEAD_DIM), lambda hh, i: (i, hh))
    halo_a = ((8, HEAD_DIM), lambda hh, i: (jnp.minimum((i + 1) * (tr // 8), t // 8 - 1), hh))
    wspec = lambda seg: ((8, HEAD_DIM), lambda hh, i: (0, seg * h + hh))
    wout = ((8, d), F32, (8, HEAD_DIM), lambda hh, i: (0, hh), "acc_inner")
    out = ((t, d), BF16) + hs + ("set",)
    res = _rowcall(
        "conv_bwd", body, (h, ni),
        [(proj_b,) + slab(1), (proj_b,) + halo_b(1), (proj_b,) + slab(2), (proj_b,) + halo_b(2),
         (proj_b,) + slab(3), (proj_b,) + halo_b(3),
         (conv_w8,) + wspec(0), (conv_w8,) + wspec(1), (conv_w8,) + wspec(2),
         (du_q,) + hs, (du_q,) + halo_a, (du_k,) + hs, (du_k,) + halo_a, (du_v,) + hs,
         (du_v,) + halo_a],
        [out, out, out, wout, wout, wout])
    return res


def _gdn_post_fwd(o, proj_b, gnw, n_heads):
    t, d = o.shape
    h = n_heads
    tr = _pick(t, TR_HEAD)

    def body(idx, o_v, z, w):
        rstd = lax.rsqrt(jnp.mean(o_v * o_v, axis=1, keepdims=True) + NORM_EPS)
        return ((o_v * rstd) * w * (z * _sigmoid(z)),)

    hs = ((tr, HEAD_DIM), lambda i, hh: (i, hh))
    return _rowcall("gdn_post_fwd", body, (t // tr, h),
                    [(o,) + hs, (proj_b, (tr, HEAD_DIM), lambda i, hh: (i, 4 * h + hh)), _full(gnw)],
                    [((t, d), BF16) + hs + ("set",)])[0]


def _gdn_post_bwd(do_gdn, o, proj_b, gnw, n_heads):
    t, d = o.shape
    h = n_heads
    tr = _pick(t, TR_HEAD)

    def body(idx, dog, o_v, z, w):
        rstd = lax.rsqrt(jnp.mean(o_v * o_v, axis=1, keepdims=True) + NORM_EPS)
        n = o_v * rstd
        sg = _sigmoid(z)
        don = dog * (z * sg)
        dz = dog * (n * w) * (sg * (1.0 + z * (1.0 - sg)))
        dn = don * w
        d_o = rstd * (dn - n * jnp.mean(dn * n, axis=1, keepdims=True))
        return d_o, dz, jnp.sum(don * n, axis=0, keepdims=True)

    hs = ((tr, HEAD_DIM), lambda i, hh: (i, hh))
    return _rowcall("gdn_post_bwd", body, (t // tr, h),
                    [(do_gdn,) + hs, (o,) + hs,
                     (proj_b, (tr, HEAD_DIM), lambda i, hh: (i, 4 * h + hh)), _full(gnw)],
                    [((t, d), F32) + hs + ("set",), ((t, d), BF16) + hs + ("set",),
                     ((1, HEAD_DIM), F32, (1, HEAD_DIM), lambda i, hh: (0, 0), "acc_all")])


def _seg(arr, tr, d, seg):
    return (arr, (tr, d), lambda i: (i, seg))


def _sb_post_fwd(o_raw, proj_b):
    t, d = o_raw.shape
    tr = _pick(t, TR_WIDE)

    def body(idx, o_v, z):
        return (o_v * (z * _sigmoid(z)),)

    return _rowcall("sb_post_fwd", body, (t // tr,), [_seg(o_raw, tr, d, 0), _seg(proj_b, tr, d, 0)],
                    [((t, d), BF16, (tr, d), lambda i: (i, 0), "set")])[0]


def _sb_post_bwd(do_sb, o_raw, proj_b):
    t, d = o_raw.shape
    tr = _pick(t, TR_WIDE)

    def body(idx, dos, o_v, z):
        sg = _sigmoid(z)
        return dos * (z * sg), dos * o_v * (sg * (1.0 + z * (1.0 - sg)))

    out = ((t, d), BF16, (tr, d), lambda i: (i, 0), "set")
    return _rowcall("sb_post_bwd", body, (t // tr,),
                    [_seg(do_sb, tr, d, 0), _seg(o_raw, tr, d, 0), _seg(proj_b, tr, d, 0)],
                    [out, out])


def _merge_fwd(proj_b, p_sb, p_gdn):
    t, d = p_sb.shape
    tr = _pick(t, TR_WIDE)

    def body(idx, m_sb, m_gdn, ps, pg):
        return (_sigmoid(m_sb) * ps + _sigmoid(m_gdn) * pg,)

    return _rowcall("merge_fwd", body, (t // tr,),
                    [_seg(proj_b, tr, d, 5), _seg(proj_b, tr, d, 6), _seg(p_sb, tr, d, 0),
                     _seg(p_gdn, tr, d, 0)],
                    [((t, d), BF16, (tr, d), lambda i: (i, 0), "set")])[0]


def _merge_bwd(dy, proj_b, p_sb, p_gdn):
    t, d = p_sb.shape
    tr = _pick(t, TR_WIDE)

    def body(idx, dy_v, m_sb, m_gdn, ps, pg):
        s1 = _sigmoid(m_sb)
        s2 = _sigmoid(m_gdn)
        return s1 * dy_v, s2 * dy_v, dy_v * ps * (s1 * (1.0 - s1)), dy_v * pg * (s2 * (1.0 - s2))

    out = ((t, d), BF16, (tr, d), lambda i: (i, 0), "set")
    return _rowcall("merge_bwd", body, (t // tr,),
                    [_seg(dy, tr, d, 0), _seg(proj_b, tr, d, 5), _seg(proj_b, tr, d, 6),
                     _seg(p_sb, tr, d, 0), _seg(p_gdn, tr, d, 0)],
                    [out] * 4)


def _tail(x, r, target, gate, final_w):
    t, d = x.shape
    tr = _pick(t, TR_WIDE)

    def body(idx, x_v, r_v, tg, gt, fw):
        x2 = x_v + gt * r_v
        rstd = lax.rsqrt(jnp.mean(x2 * x2, axis=1, keepdims=True) + NORM_EPS)
        n = x2 * rstd
        diff = n * fw - tg
        loss = 0.5 * jnp.sum(jnp.mean(diff * diff, axis=1, keepdims=True), axis=0, keepdims=True)
        dout = diff * (1.0 / d)
        dn = dout * fw
        dx2 = rstd * (dn - n * jnp.mean(dn * n, axis=1, keepdims=True))
        return (dx2, gt * dx2, jnp.sum(dout * n, axis=0, keepdims=True),
                jnp.sum(dx2 * r_v, axis=0, keepdims=True), jnp.broadcast_to(loss, (1, LANES)))

    rb = ((tr, d), lambda i: (i, 0))
    vec = ((1, d), F32, (1, d), lambda i: (0, 0), "acc_all")
    return _rowcall("tail", body, (t // tr,),
                    [(x,) + rb, (r,) + rb, (target,) + rb, _full(gate), _full(final_w)],
                    [((t, d), F32) + rb + ("set",), ((t, d), BF16) + rb + ("set",), vec, vec,
                     ((1, LANES), F32, (1, LANES), lambda i: (0, 0), "acc_all")])


def _pad_to(a, rows, cols):
    return jnp.pad(a, ((0, rows - a.shape[0]), (0, cols - a.shape[1])))


def kernel(x, c, w_ada, b_ada, norm_w, w_in, gdn_conv_w, gdn_a_log, gdn_dt_bias, gdn_norm_w, w_proj_sb, w_proj_gdn, w_out, final_norm_w, loss_target, m_w_ada, m_b_ada, m_norm_w, m_w_in, m_gdn_conv_w, m_gdn_a_log, m_gdn_dt_bias, m_gdn_norm_w, m_w_proj_sb, m_w_proj_gdn, m_w_out, m_final_norm_w, v_w_ada, v_b_ada, v_norm_w, v_w_in, v_gdn_conv_w, v_gdn_a_log, v_gdn_dt_bias, v_gdn_norm_w, v_w_proj_sb, v_w_proj_gdn, v_w_out, v_final_norm_w):
    t, d = x.shape[1], x.shape[2]
    h = d // HEAD_DIM
    me = 4 * lax.axis_index("x") + 2 * lax.axis_index("y") + lax.axis_index("c")
    x2d = x[0]
    tgt = loss_target[0]
    ada_cols = w_ada.shape[2]
    in_cols = w_in.shape[2]
    rows_p = w_out.shape[1]

    w_in_all = _exchange("gather_w_in", w_in[0].astype(BF16), True)
    w_in_full = jnp.transpose(w_in_all, (1, 0, 2)).reshape(d, N_DEV * in_cols)
    w_main = jnp.concatenate([w_in_full[:, :8 * d], w_in_full[:, 8 * d + 2 * h:]], axis=1)
    w_g = _pad_to(w_in_full[:, 8 * d:8 * d + 2 * h], d, LANES)
    w_main_t = w_main.T
    w_g_t = w_g.T
    w_sq = jnp.stack([w_proj_sb[0], w_proj_gdn[0], w_out[0]]).astype(BF16)
    conv_all = _exchange("gather_conv", _pad_to(gdn_conv_w[0], 8, gdn_conv_w.shape[2]), True)
    conv_w8 = jnp.transpose(conv_all, (1, 0, 2)).reshape(8, 3 * d)
    c_all = _exchange("gather_c", _pad_to(c, 8, d), True)[:, 0, :]

    sc_all = c_all * _sigmoid(c_all)
    mod_part = _matmul("ada_fwd", _pad_to(sc_all, 16, d).astype(BF16), w_ada[0].astype(BF16), F32)
    mod_part = mod_part[:N_DEV] + lax.dynamic_slice(b_ada, (0, me * ada_cols), (1, ada_cols))
    mod_rows = _exchange("a2a_mod", _pad_to(mod_part, 8, ada_cols).reshape(N_DEV, 1, ada_cols)
                         * jnp.ones((1, 8, 1), F32), False)
    mod = mod_rows[:, 0, :].reshape(1, 3 * d)
    shift, scale, gate = mod[:, :d], mod[:, d:2 * d], mod[:, 2 * d:]

    hmod = _norm_mod_fwd(x2d, shift, scale, norm_w)
    proj_a = _matmul("in_proj_a", hmod, w_main, BF16, n_cols=3 * d, col0=0)
    proj_b, w_sq_all = _matmul("in_proj_b", hmod, w_main, F32, n_cols=7 * d, col0=3 * d,
                               xchg=w_sq, gather=True)
    w_sq_full = jnp.transpose(w_sq_all, (1, 0, 2, 3)).reshape(3, d, d)
    wp_sb, wp_gdn, wo = w_sq_full[0], w_sq_full[1], w_sq_full[2]
    proj_c = _matmul("in_proj_c", hmod, w_g, F32)
    o_sb_raw, ctot, sb_visited = _sb_fwd(proj_a, h)
    o_sb = _sb_post_fwd(o_sb_raw, proj_b)
    ab = _pad_to(jnp.concatenate([gdn_a_log, gdn_dt_bias], axis=0), 8, LANES)
    gq, gk, gv, g_beta, g_cum = _gdn_pre_fwd(proj_b, proj_c, conv_w8, ab, h)
    o_gdn_raw, states, tms = _gdn_fwd(gq, gk, gv, g_beta, g_cum, h)
    o_gdn = _gdn_post_fwd(o_gdn_raw, proj_b, gdn_norm_w, h)
    p_sb = _matmul("proj_sb", o_sb, wp_sb, F32)
    p_gdn = _matmul("proj_gdn", o_gdn, wp_gdn, F32)
    y = _merge_fwd(proj_b, p_sb, p_gdn)
    r = _matmul("out_proj", y, wo, F32)
    dx2, dr, d_final_w, d_gate, loss_part = _tail(x2d, r, tgt, gate, final_norm_w.reshape(1, d))

    dy = _matmul("d_out_proj", dr, wo.T, F32)
    dw_out = _matmul("dw_out", y.T, dr, BF16)
    dp_sb, dp_gdn, dm_sb, dm_gdn = _merge_bwd(dy, proj_b, p_sb, p_gdn)
    do_sb = _matmul("d_proj_sb", dp_sb, wp_sb.T, F32)
    dw_p_sb = _matmul("dw_proj_sb", o_sb.T, dp_sb, BF16)
    do_gdn = _matmul("d_proj_gdn", dp_gdn, wp_gdn.T, F32)
    dw_p_gdn = _matmul("dw_proj_gdn", o_gdn.T, dp_gdn, BF16)
    do_sb_raw, d_sbz = _sb_post_bwd(do_sb, o_sb_raw, proj_b)
    d_sbq, d_sbk, d_sbv = _sb_bwd(proj_a, do_sb_raw, ctot, sb_visited, h)
    d_o_gdn, d_gz, d_gnw = _gdn_post_bwd(do_gdn, o_gdn_raw, proj_b, gdn_norm_w, h)
    dgq, dgk, dgv, dgbe, dgcum = _gdn_bwd(gq, gk, gv, g_beta, g_cum, d_o_gdn, states, tms, h)
    du_q, du_k, du_v, d_gates, d_ab = _gdn_pre_bwd(proj_b, proj_c, conv_w8, ab, dgq, dgk, dgv,
                                                  dgbe, dgcum, h)
    d_gq, d_gk, d_gv, dcw_q, dcw_k, dcw_v = _conv_bwd(proj_b, conv_w8, du_q, du_k, du_v, h)
    dproj = jnp.concatenate([d_sbq, d_sbk, d_sbv, d_sbz, d_gq, d_gk, d_gv, d_gz, dm_sb, dm_gdn],
                            axis=1)
    d_gates_b = d_gates.astype(BF16)
    hmod_t = hmod.T
    dw_sq = jnp.stack([dw_p_sb, dw_p_gdn, dw_out]).reshape(3, N_DEV, rows_p, d)
    dw_main, dw_sq_parts = _matmul("dw_in", hmod_t, dproj, BF16,
                                   xchg=jnp.transpose(dw_sq, (1, 0, 2, 3)))
    dw_g = _matmul("dw_in_g", hmod_t, d_gates_b, BF16)
    dw_in_full = jnp.concatenate([dw_main[:, :8 * d], dw_g[:, :2 * h], dw_main[:, 8 * d:]], axis=1)
    dh_a, dw_in_parts = _matmul(
        "d_in_proj", dproj, w_main_t, F32,
        xchg=jnp.transpose(dw_in_full.reshape(d, N_DEV, in_cols), (1, 0, 2)))
    dh_b = _matmul("d_in_proj_g", d_gates_b, w_g_t, F32)
    grad_x, d_shift, d_scale, d_norm_w = _norm_mod_bwd(dh_a, dh_b, x2d, dx2, scale, norm_w)

    dmod = jnp.concatenate([d_shift, d_scale, d_gate], axis=1)
    small = jnp.concatenate([dmod, d_norm_w, d_final_w, d_ab[:, :h], d_ab[:, h:2 * h], d_gnw,
                             loss_part], axis=1)
    n_small = small.shape[1]
    small_all = _exchange("gather_small", _pad_to(small, 8, n_small), True)[:, 0:1, :]
    small_w = jnp.concatenate([b_ada, norm_w, final_norm_w.reshape(1, d), gdn_a_log, gdn_dt_bias,
                               gdn_norm_w, jnp.zeros((1, LANES), F32)], axis=1)
    small_m = jnp.concatenate([m_b_ada, m_norm_w, m_final_norm_w.reshape(1, d), m_gdn_a_log,
                               m_gdn_dt_bias, m_gdn_norm_w, jnp.zeros((1, LANES), F32)], axis=1)
    small_v = jnp.concatenate([v_b_ada, v_norm_w, v_final_norm_w.reshape(1, d), v_gdn_a_log,
                               v_gdn_dt_bias, v_gdn_norm_w, jnp.ones((1, LANES), F32)], axis=1)
    s_g, s_d, s_m, s_v = _adamw("adamw_small", small_all, small_w, small_m, small_v)
    cuts = [3 * d, 4 * d, 5 * d, 5 * d + h, 5 * d + 2 * h, 5 * d + 2 * h + HEAD_DIM]

    def split_small(a):
        b, nw, fw, al, dtb, gn, _ = jnp.split(a, cuts, axis=1)
        return b, nw, fw.reshape(d), al, dtb, gn

    loss = s_g[0, cuts[-1]]

    dmod_all = small_all[:, 0, :3 * d]
    dmod_mine = lax.dynamic_slice(dmod_all, (0, me * ada_cols), (N_DEV, ada_cols))
    dw_ada = _matmul("dw_ada", _pad_to(sc_all.T, d, LANES).astype(BF16),
                     _pad_to(dmod_mine, LANES, ada_cols).astype(BF16), F32)
    ada = _adamw("adamw_ada", dw_ada[None], w_ada[0], m_w_ada[0], v_w_ada[0])

    win = _adamw("adamw_w_in", dw_in_parts, w_in[0], m_w_in[0], v_w_in[0])
    sq = _adamw("adamw_sq", dw_sq_parts.reshape(N_DEV, 3 * rows_p, d),
                jnp.concatenate([w_proj_sb[0], w_proj_gdn[0], w_out[0]], axis=0),
                jnp.concatenate([m_w_proj_sb[0], m_w_proj_gdn[0], m_w_out[0]], axis=0),
                jnp.concatenate([v_w_proj_sb[0], v_w_proj_gdn[0], v_w_out[0]], axis=0))
    dcw = jnp.concatenate([dcw_q, dcw_k, dcw_v], axis=1)
    cw_cols = gdn_conv_w.shape[2]
    dcw_parts = _exchange("a2a_dconv",
                          jnp.transpose(dcw.reshape(8, N_DEV, cw_cols), (1, 0, 2)), False)
    cw = _adamw("adamw_conv", dcw_parts, _pad_to(gdn_conv_w[0], 8, cw_cols),
                _pad_to(m_gdn_conv_w[0], 8, cw_cols),
                jnp.pad(v_gdn_conv_w[0], ((0, 8 - GDN_CONV), (0, 0)), constant_values=1.0))

    outs = [loss, grad_x[None]]
    for k_out in range(4):
        b, nw, fw, al, dtb, gn = split_small((s_g, s_d, s_m, s_v)[k_out])
        sq3 = sq[k_out].reshape(3, 1, rows_p, d)
        outs += [ada[k_out][None], b, nw, win[k_out][None], cw[k_out][None, :GDN_CONV], al, dtb, gn,
                 sq3[0], sq3[1], sq3[2], fw]
    return tuple(outs)
```

```python
import jax
import jax.numpy as jnp
from jax import lax
from jax.experimental import pallas as pl
from jax.experimental.pallas import tpu as pltpu

F32 = jnp.float32
BF16 = jnp.bfloat16
N_DEV = 8
HEAD_DIM = 128
LANES = 128
GDN_CHUNK = 64
GDN_CONV = 4
NORM_EPS = 1e-6
L2_EPS = 1e-6
ADAM_LR = 0.001
ADAM_B1 = 0.9
ADAM_B2 = 0.999
ADAM_EPS = 1e-08
ADAM_WD = 0.01
ADAM_STEP = 10
VMEM_LIMIT = 56 * 1024 * 1024
MESH = pl.DeviceIdType.MESH
NT = (((1,), (1,)), ((), ()))
TN = (((0,), (0,)), ((), ()))


def _pick(n, pref):
    t = min(pref, n)
    while n % t:
        t //= 2
    return t


def _sigmoid(x):
    return 1.0 / (1.0 + jnp.exp(-x))


def _bdot(a, b, dims=None):
    a = a.astype(BF16)
    b = b.astype(BF16)
    if dims is None:
        return jnp.dot(a, b, preferred_element_type=F32)
    return lax.dot_general(a, b, dims, preferred_element_type=F32)


def _split2(x):
    hi = x.astype(BF16)
    lo = (x - hi.astype(F32)).astype(BF16)
    return hi, lo


def _split3(x):
    p1 = x.astype(BF16)
    r = x - p1.astype(F32)
    p2 = r.astype(BF16)
    p3 = (r - p2.astype(F32)).astype(BF16)
    return p1, p2, p3


def _dot2(x, c, dims=None):
    hi, lo = _split2(x)
    return _bdot(hi, c, dims) + _bdot(lo, c, dims)


def _cdot3(c, x):
    p1, p2, p3 = _split3(x)
    return _bdot(c, p1) + _bdot(c, p2) + _bdot(c, p3)


def _dot3(a, b, dims=None):
    ah, al = _split2(a)
    bh, bl = _split2(b)
    return _bdot(ah, bh, dims) + (_bdot(ah, bl, dims) + _bdot(al, bh, dims))


def _dot6(a, b, dims=None):
    a1, a2, a3 = _split3(a)
    b1, b2, b3 = _split3(b)
    small = (_bdot(a1, b3, dims) + _bdot(a3, b1, dims)) + _bdot(a2, b2, dims)
    return _bdot(a1, b1, dims) + ((_bdot(a1, b2, dims) + _bdot(a2, b1, dims)) + small)


def _full(a):
    nd = a.ndim
    return (a, a.shape, lambda *idx: (0,) * nd)


def _rowcall(name, body, grid, ins, outs):
    n_in = len(ins)
    modes = [o[4] for o in outs]
    n_ax = len(grid)

    def kern(*refs):
        idx = tuple(pl.program_id(a) for a in range(n_ax))
        vals = body(idx, *[r[...] for r in refs[:n_in]])
        first_all = idx[0] == 0
        for a in range(1, n_ax):
            first_all = jnp.logical_and(first_all, idx[a] == 0)
        first_inner = idx[-1] == 0
        for r, v, mode in zip(refs[n_in:], vals, modes):
            v = v.astype(r.dtype)
            if mode == "set":
                r[...] = v
            else:
                first = first_all if mode == "acc_all" else first_inner

                @pl.when(first)
                def _(r=r, v=v):
                    r[...] = v

                @pl.when(jnp.logical_not(first))
                def _(r=r, v=v):
                    r[...] += v

    return pl.pallas_call(
        kern, name=name, grid=grid,
        in_specs=[pl.BlockSpec(b, m) for (_, b, m) in ins],
        out_specs=[pl.BlockSpec(o[2], o[3]) for o in outs],
        out_shape=[jax.ShapeDtypeStruct(o[0], o[1]) for o in outs],
        compiler_params=pltpu.CompilerParams(
            dimension_semantics=("arbitrary",) * n_ax, vmem_limit_bytes=VMEM_LIMIT),
    )(*[a for (a, _, _) in ins])


EXCHANGE_SEMS = [pltpu.SemaphoreType.DMA((N_DEV - 1,)), pltpu.SemaphoreType.DMA((N_DEV - 1,)),
                 pltpu.SemaphoreType.DMA]


def _exchange_shape(x, gather):
    return jax.ShapeDtypeStruct((N_DEV,) + tuple(x.shape if gather else x.shape[1:]), x.dtype)


def _exchange_copies(x_ref, out_ref, send_sems, recv_sems, local_sem, gather, start, wait):
    xi, yi, ci = lax.axis_index("x"), lax.axis_index("y"), lax.axis_index("c")
    me = 4 * xi + 2 * yi + ci
    mine = pltpu.make_async_copy(x_ref if gather else x_ref.at[me], out_ref.at[me], local_sem)
    if start:
        mine.start()
    for k in range(1, N_DEV):
        kx, ky, kc = (k >> 2) & 1, (k >> 1) & 1, k & 1
        pid = me ^ k
        src = x_ref if gather else x_ref.at[pid]
        if start:
            pltpu.make_async_remote_copy(
                src_ref=src, dst_ref=out_ref.at[me], send_sem=send_sems.at[k - 1],
                recv_sem=recv_sems.at[k - 1], device_id=(xi ^ kx, yi ^ ky, ci ^ kc),
                device_id_type=MESH).start()
        if wait:
            pltpu.make_async_remote_copy(
                src_ref=src, dst_ref=out_ref.at[pid], send_sem=send_sems.at[k - 1],
                recv_sem=recv_sems.at[k - 1], device_id=(xi, yi, ci), device_id_type=MESH).wait()
    if wait:
        mine.wait()


def _matmul(name, a, b, out_dtype, n_cols=None, col0=0, tm=1024, tn=1024, tk=2048, xchg=None,
            gather=False):
    m, k = a.shape
    n = b.shape[1] if n_cols is None else n_cols
    tm = _pick(m, tm)
    tn = _pick(n, tn)
    while col0 % tn:
        tn //= 2
    tk = _pick(k, tk)
    nk = k // tk
    cb = col0 // tn
    grid = (m // tm, n // tn, nk)

    def kern(a_ref, b_ref, *rest):
        if xchg is None:
            o_ref, acc_ref = rest
        else:
            x_ref, o_ref, xo_ref, acc_ref, send_sems, recv_sems, local_sem = rest
            step = (pl.program_id(0) * grid[1] + pl.program_id(1)) * nk + pl.program_id(2)

            @pl.when(step == 0)
            def _():
                _exchange_copies(x_ref, xo_ref, send_sems, recv_sems, local_sem, gather, True, False)

        kk = pl.program_id(2)
        part = jnp.dot(a_ref[...], b_ref[...], preferred_element_type=F32)
        if nk == 1:
            o_ref[...] = part.astype(o_ref.dtype)
        else:
            @pl.when(kk == 0)
            def _():
                acc_ref[...] = part

            @pl.when(kk > 0)
            def _():
                acc_ref[...] += part

            @pl.when(kk == nk - 1)
            def _():
                o_ref[...] = acc_ref[...].astype(o_ref.dtype)

        if xchg is not None:
            @pl.when(step == grid[0] * grid[1] * nk - 1)
            def _():
                _exchange_copies(x_ref, xo_ref, send_sems, recv_sems, local_sem, gather, False, True)

    in_specs = [pl.BlockSpec((tm, tk), lambda i, j, kk: (i, kk)),
                pl.BlockSpec((tk, tn), lambda i, j, kk: (kk, j + cb))]
    out_specs = [pl.BlockSpec((tm, tn), lambda i, j, kk: (i, j))]
    out_shape = [jax.ShapeDtypeStruct((m, n), out_dtype)]
    scratch = [pltpu.VMEM((tm, tn), F32)]
    args = (a, b)
    if xchg is not None:
        in_specs.append(pl.BlockSpec(memory_space=pl.ANY))
        out_specs.append(pl.BlockSpec(memory_space=pl.ANY))
        out_shape.append(_exchange_shape(xchg, gather))
        scratch += EXCHANGE_SEMS
        args = (a, b, xchg)
    res = pl.pallas_call(
        kern, name=name, grid=grid, in_specs=in_specs, out_specs=out_specs, out_shape=out_shape,
        scratch_shapes=scratch,
        compiler_params=pltpu.CompilerParams(
            dimension_semantics=("arbitrary", "arbitrary", "arbitrary"),
            vmem_limit_bytes=VMEM_LIMIT),
    )(*args)
    return res[0] if xchg is None else (res[0], res[1])


def _exchange(name, x, gather):
    def body(x_ref, out_ref, send_sems, recv_sems, local_sem):
        _exchange_copies(x_ref, out_ref, send_sems, recv_sems, local_sem, gather, True, True)

    return pl.pallas_call(
        body, name=name,
        out_shape=_exchange_shape(x, gather),
        in_specs=[pl.BlockSpec(memory_space=pl.ANY)],
        out_specs=pl.BlockSpec(memory_space=pl.ANY),
        scratch_shapes=EXCHANGE_SEMS,
    )(x)


def _adamw(name, parts, w, m, v):
    p, r, c = parts.shape
    tr = r if r <= 64 else _pick(r, 64)

    def body(idx, parts_v, w_v, m_v, v_v):
        g = parts_v[0].astype(F32)
        for s in range(1, p):
            g = g + parts_v[s].astype(F32)
        m2 = ADAM_B1 * m_v + (1.0 - ADAM_B1) * g
        v2 = ADAM_B2 * v_v + (1.0 - ADAM_B2) * (g * g)
        m_hat = m2 / (1.0 - ADAM_B1 ** ADAM_STEP)
        v_hat = v2 / (1.0 - ADAM_B2 ** ADAM_STEP)
        delta = -ADAM_LR * (m_hat / (jnp.sqrt(v_hat) + ADAM_EPS) + ADAM_WD * w_v)
        return g, delta, m2, v2

    rb = ((tr, c), lambda i: (i, 0))
    return _rowcall(
        name, body, (r // tr,),
        [(parts, (p, tr, c), lambda i: (0, i, 0)), (w,) + rb, (m,) + rb, (v,) + rb],
        [((r, c), F32) + rb + ("set",)] * 4)


TR_WIDE = 256


def _norm_mod_fwd(x, shift, scale, norm_w):
    t, d = x.shape
    tr = _pick(t, TR_WIDE)

    def body(idx, x_v, sh, sc, nw):
        rstd = lax.rsqrt(jnp.mean(x_v * x_v, axis=1, keepdims=True) + NORM_EPS)
        return ((x_v * rstd) * nw * (1.0 + sc) + sh,)

    rb = ((tr, d), lambda i: (i, 0))
    return _rowcall("norm_mod_fwd", body, (t // tr,),
                    [(x,) + rb, _full(shift), _full(scale), _full(norm_w)],
                    [((t, d), BF16) + rb + ("set",)])[0]


def _norm_mod_bwd(dh_a, dh_b, x, dx2, scale, norm_w):
    t, d = x.shape
    tr = _pick(t, TR_WIDE)

    def body(idx, dha, dhb, x_v, dx2_v, sc, nw):
        dh = dha + dhb
        rstd = lax.rsqrt(jnp.mean(x_v * x_v, axis=1, keepdims=True) + NORM_EPS)
        xn = x_v * rstd
        m1 = 1.0 + sc
        dxn = dh * nw * m1
        dx = rstd * (dxn - xn * jnp.mean(dxn * xn, axis=1, keepdims=True))
        dhx = dh * xn
        return (dx2_v + dx,
                jnp.sum(dh, axis=0, keepdims=True),
                jnp.sum(dhx * nw, axis=0, keepdims=True),
                jnp.sum(dhx * m1, axis=0, keepdims=True))

    rb = ((tr, d), lambda i: (i, 0))
    vec = ((1, d), F32, (1, d), lambda i: (0, 0), "acc_all")
    return _rowcall("norm_mod_bwd", body, (t // tr,),
                    [(dh_a,) + rb, (dh_b,) + rb, (x,) + rb, (dx2,) + rb, _full(scale), _full(norm_w)],
                    [((t, d), F32) + rb + ("set",), vec, vec, vec])


SB_TILE = 512


SB_SCAN = 256
SB_DEAD = 105.0


def _neg_log_not_beta(z):
    return jnp.maximum(z, 0.0) + jnp.log(1.0 + jnp.exp(-jnp.abs(z)))


def _key_scan(x, tri, later):
    n_blk = x.shape[1] // SB_SCAN
    parts = [x[:, g * SB_SCAN:(g + 1) * SB_SCAN] for g in range(n_blk)]
    sums = [jnp.sum(p, axis=1, keepdims=True) for p in parts]
    outs = []
    for g in range(n_blk):
        o = _dot2(parts[g], tri)
        others = range(g + 1, n_blk) if later else range(g)
        for g2 in others:
            o = o + sums[g2]
        outs.append(o)
    total = sums[0]
    for s in sums[1:]:
        total = total + s
    return (outs[0] if n_blk == 1 else jnp.concatenate(outs, axis=1)), total


def _scan_tri(kind):
    row = lax.broadcasted_iota(jnp.int32, (SB_SCAN, SB_SCAN), 0)
    col = lax.broadcasted_iota(jnp.int32, (SB_SCAN, SB_SCAN), 1)
    return {"after": row > col, "upto": row <= col, "before": row < col}[kind].astype(BF16)


def _sb_fwd(qkv, n_heads):
    t = qkv.shape[0]
    tq = _pick(t, SB_TILE)
    nq = t // tq
    scale = HEAD_DIM ** -0.5

    def kern(q_ref, k_ref, v_ref, o_ref, c_ref, nd_ref):
        i_blk = pl.program_id(1)
        row = lax.broadcasted_iota(jnp.int32, (tq, tq), 0)
        col = lax.broadcasted_iota(jnp.int32, (tq, tq), 1)
        causal = col < row
        after = _scan_tri("after")
        qb = q_ref[...]

        def tile(j_blk, c, acc, diag):
            r0 = pl.multiple_of(j_blk * tq, tq)
            kb = k_ref[pl.ds(r0, tq), :]
            vb = v_ref[pl.ds(r0, tq), :]
            z = lax.dot_general(qb, kb, NT, preferred_element_type=F32) * scale
            n = _neg_log_not_beta(z)
            if diag:
                n = jnp.where(causal, n, 0.0)
            later, total = _key_scan(n, after, True)
            a = jnp.exp(z - (n + later + c))
            if diag:
                a = jnp.where(causal, a, 0.0)
            acc = acc + jnp.dot(a.astype(BF16), vb, preferred_element_type=F32)
            return c + total, acc

        c, acc = tile(i_blk, jnp.zeros((tq, 1), F32), jnp.zeros((tq, HEAD_DIM), F32), True)

        def more(st):
            return jnp.logical_and(st[0] < i_blk, jnp.min(st[1]) <= SB_DEAD)

        def step(st):
            c2, acc2 = tile(i_blk - 1 - st[0], st[1], st[2], False)
            return st[0] + 1, c2, acc2

        n_done, c, acc = lax.while_loop(more, step, (jnp.int32(0), c, acc))
        o_ref[...] = acc
        c_ref[...] = jnp.broadcast_to(c, (tq, HEAD_DIM))
        nd_ref[...] = jnp.full((8, LANES), n_done.astype(F32))

    h = n_heads
    return pl.pallas_call(
        kern, name="sb_fwd", grid=(h, nq),
        in_specs=[pl.BlockSpec((tq, HEAD_DIM), lambda hh, i: (i, hh)),
                  pl.BlockSpec((t, HEAD_DIM), lambda hh, i: (0, h + hh)),
                  pl.BlockSpec((t, HEAD_DIM), lambda hh, i: (0, 2 * h + hh))],
        out_specs=[pl.BlockSpec((tq, HEAD_DIM), lambda hh, i: (i, hh)),
                   pl.BlockSpec((tq, HEAD_DIM), lambda hh, i: (i, hh)),
                   pl.BlockSpec((None, None, 8, LANES), lambda hh, i: (hh, i, 0, 0))],
        out_shape=[jax.ShapeDtypeStruct((t, h * HEAD_DIM), F32),
                   jax.ShapeDtypeStruct((t, h * HEAD_DIM), F32),
                   jax.ShapeDtypeStruct((h, nq, 8, LANES), F32)],
        compiler_params=pltpu.CompilerParams(
            dimension_semantics=("arbitrary", "arbitrary"), vmem_limit_bytes=VMEM_LIMIT),
    )(qkv, qkv, qkv)


def _sb_bwd(qkv, do, ctot, n_visited, n_heads):
    t = qkv.shape[0]
    tq = _pick(t, SB_TILE)
    nq = t // tq
    scale = HEAD_DIM ** -0.5

    def kern(q_ref, do_ref, c_ref, nd_ref, k_ref, v_ref, dq_ref, dk_ref, dv_ref, dk_acc, dv_acc):
        i_blk = pl.program_id(1)
        row = lax.broadcasted_iota(jnp.int32, (tq, tq), 0)
        col = lax.broadcasted_iota(jnp.int32, (tq, tq), 1)
        causal = col < row
        upto = _scan_tri("upto")
        before = _scan_tri("before")
        qb = q_ref[...]
        dob = do_ref[...]
        ctot_v = c_ref[...][:, 0:1]

        @pl.when(i_blk == 0)
        def _():
            dk_acc[...] = jnp.zeros_like(dk_acc)
            dv_acc[...] = jnp.zeros_like(dv_acc)

        def tile(j_blk, cl, pe, dq, diag):
            r0 = pl.multiple_of(j_blk * tq, tq)
            kb = k_ref[pl.ds(r0, tq), :]
            vb = v_ref[pl.ds(r0, tq), :]
            z = lax.dot_general(qb, kb, NT, preferred_element_type=F32) * scale
            n = _neg_log_not_beta(z)
            if diag:
                n = jnp.where(causal, n, 0.0)
            upto_s, n_total = _key_scan(n, upto, False)
            lb = z - n
            a = jnp.exp(lb - ((ctot_v - cl) - upto_s))
            if diag:
                a = jnp.where(causal, a, 0.0)
            da = lax.dot_general(dob, vb, NT, preferred_element_type=F32)
            e = da * a
            before_s, e_total = _key_scan(e, before, False)
            beta = jnp.exp(lb)
            dz = (e - beta * (e + (pe + before_s))) * scale
            if diag:
                dz = jnp.where(causal, dz, 0.0)
            dzb = dz.astype(BF16)
            dq = dq + jnp.dot(dzb, kb, preferred_element_type=F32)
            dk_acc[pl.ds(r0, tq), :] += lax.dot_general(dzb, qb, TN, preferred_element_type=F32)
            dv_acc[pl.ds(r0, tq), :] += lax.dot_general(a.astype(BF16), dob, TN,
                                                        preferred_element_type=F32)
            return cl + n_total, pe + e_total, dq

        zero = jnp.zeros((tq, 1), F32)
        first = i_blk - jnp.max(nd_ref[...]).astype(jnp.int32)
        cl, pe, dq = lax.fori_loop(first, i_blk, lambda j, s: tile(j, s[0], s[1], s[2], False),
                                   (zero, zero, jnp.zeros((tq, HEAD_DIM), F32)))
        cl, pe, dq = tile(i_blk, cl, pe, dq, True)
        dq_ref[...] = dq.astype(BF16)

        @pl.when(i_blk == nq - 1)
        def _():
            dk_ref[...] = dk_acc[...].astype(BF16)
            dv_ref[...] = dv_acc[...].astype(BF16)

    h = n_heads
    qspec = pl.BlockSpec((tq, HEAD_DIM), lambda hh, i: (i, hh))
    hspec = pl.BlockSpec((t, HEAD_DIM), lambda hh, i: (0, hh))
    out = jax.ShapeDtypeStruct((t, h * HEAD_DIM), BF16)
    return pl.pallas_call(
        kern, name="sb_bwd", grid=(h, nq),
        in_specs=[qspec, qspec, qspec,
                  pl.BlockSpec((None, None, 8, LANES), lambda hh, i: (hh, i, 0, 0)),
                  pl.BlockSpec((t, HEAD_DIM), lambda hh, i: (0, h + hh)),
                  pl.BlockSpec((t, HEAD_DIM), lambda hh, i: (0, 2 * h + hh))],
        out_specs=[qspec, hspec, hspec],
        out_shape=[out, out, out],
        scratch_shapes=[pltpu.VMEM((t, HEAD_DIM), F32), pltpu.VMEM((t, HEAD_DIM), F32)],
        compiler_params=pltpu.CompilerParams(
            dimension_semantics=("arbitrary", "arbitrary"), vmem_limit_bytes=VMEM_LIMIT),
    )(qkv, do, ctot, n_visited, qkv, qkv)


GDN_ROWS = 512
GDN_HEADS_PER_STEP = 2
TR_HEAD = 512


def _shift_rows(cur, halo, k, back):
    n = cur.shape[0]
    ext = jnp.concatenate([cur, halo], axis=0)
    return pltpu.roll(ext, k if back else n + 8 - k, 0)[:n]


def _conv_fwd(cur, halo, w):
    out = cur * w[GDN_CONV - 1:GDN_CONV, :]
    for i in range(GDN_CONV - 1):
        out = out + _shift_rows(cur, halo, GDN_CONV - 1 - i, True) * w[i:i + 1, :]
    return out


def _chunk_tri(n, upper):
    row = lax.broadcasted_iota(jnp.int32, (n, n), 0)
    col = lax.broadcasted_iota(jnp.int32, (n, n), 1)
    same = (row // GDN_CHUNK) == (col // GDN_CHUNK)
    tri = (col >= row) if upper else (col <= row)
    return jnp.logical_and(same, tri).astype(BF16)


def _lane_pick(x, lane):
    idx = lax.broadcasted_iota(jnp.int32, x.shape, 1)
    return jnp.sum(jnp.where(idx == lane, x, 0.0), axis=1, keepdims=True)


def _softplus(x):
    y = jnp.exp(-jnp.abs(x))
    u = 1.0 + y
    log1p = jnp.where(u == 1.0, y, jnp.log(u) * (y / jnp.where(u == 1.0, 1.0, u - 1.0)))
    return jnp.maximum(x, 0.0) + log1p


def _gdn_specs(t, tr, h, proj_seg0):
    def slab(seg):
        return ((tr, HEAD_DIM), lambda i, hh: (i, seg * h + hh))

    def halo_before(seg):
        return ((8, HEAD_DIM), lambda i, hh: (jnp.maximum(i * (tr // 8) - 1, 0), seg * h + hh))

    return slab, halo_before


def _gdn_pre_fwd(proj_b, proj_c, conv_w8, ab, n_heads):
    t = proj_b.shape[0]
    h = n_heads
    d = h * HEAD_DIM
    tr = _pick(t, TR_HEAD)
    slab, halo = _gdn_specs(t, tr, h, 1)
    scale = HEAD_DIM ** -0.5

    def body(idx, q_c, q_h, k_c, k_h, v_c, v_h, wq, wk, wv, gbga, ab_v):
        i, hh = idx
        live = (i > 0).astype(F32)
        outs = []
        for cur, hal, w, kind in ((q_c, q_h, wq, "q"), (k_c, k_h, wk, "k"), (v_c, v_h, wv, "v")):
            u = _conv_fwd(cur, hal * live, w)
            s = u * _sigmoid(u)
            if kind != "v":
                s = s * lax.rsqrt(jnp.sum(s * s, axis=1, keepdims=True) + L2_EPS)
            if kind == "q":
                s = s * scale
            outs.append(s)
        beta = _sigmoid(_lane_pick(gbga, hh))
        a_log = _lane_pick(ab_v[0:1, :], hh)
        dt = _lane_pick(ab_v[1:2, :], hh)
        g = -jnp.exp(a_log) * _softplus(_lane_pick(gbga, h + hh) + dt)
        g_rep = jnp.broadcast_to(g, (tr, HEAD_DIM))
        big_g = _cdot3(_chunk_tri(tr, False), g_rep)
        return outs + [jnp.broadcast_to(beta, (tr, HEAD_DIM)), big_g]

    wspec = lambda seg: ((8, HEAD_DIM), lambda i, hh: (0, seg * h + hh))
    out = ((t, d), F32, (tr, HEAD_DIM), lambda i, hh: (i, hh), "set")
    return _rowcall(
        "gdn_pre_fwd", body, (t // tr, h),
        [(proj_b,) + slab(1), (proj_b,) + halo(1), (proj_b,) + slab(2), (proj_b,) + halo(2),
         (proj_b,) + slab(3), (proj_b,) + halo(3),
         (conv_w8,) + wspec(0), (conv_w8,) + wspec(1), (conv_w8,) + wspec(2),
         (proj_c, (tr, LANES), lambda i, hh: (i, 0)), _full(ab)],
        [out] * 5)


def _gdn_consts():
    row = lax.broadcasted_iota(jnp.int32, (GDN_CHUNK, GDN_CHUNK), 0)
    col = lax.broadcasted_iota(jnp.int32, (GDN_CHUNK, GDN_CHUNK), 1)
    lane = lax.broadcasted_iota(jnp.int32, (GDN_CHUNK, HEAD_DIM), 1)
    return row > col, row >= col, (row == col).astype(F32), (lane == 0).astype(BF16)


def _gdn_local(q, k, v, be, ge, g_lanes, consts):
    lower, tril, eye, sel = consts
    kb_ = k * be
    vb_ = v * be
    e_g = jnp.exp(ge)
    kg = kb_ * e_g
    g_i = ge[:, :GDN_CHUNK]
    g_j = jnp.broadcast_to(g_lanes, (GDN_CHUNK, GDN_CHUNK))
    dec = jnp.where(tril, jnp.exp(jnp.minimum(g_i - g_j, 0.0)), 0.0)
    kk = _bdot(kb_, k, NT)
    qk = _bdot(q, k, NT)
    g_last = jnp.min(ge, axis=0, keepdims=True)
    kdec_f = jnp.exp(g_last - ge)
    return dict(kb=kb_, vb=vb_, e_g=e_g, kg=kg, dec=dec, kk=kk, qk=qk, kdec_f=kdec_f,
                k_dec=k * kdec_f, q_dec=q * e_g, gamma=jnp.exp(g_last),
                intra=jnp.where(tril, qk * dec, 0.0))


def _wy_lower_t(k, be, ge, g_lanes):
    row = lax.broadcasted_iota(jnp.int32, (GDN_CHUNK, GDN_CHUNK), 0)
    col = lax.broadcasted_iota(jnp.int32, (GDN_CHUNK, GDN_CHUNK), 1)
    g_row = ge[:, :GDN_CHUNK]
    g_col = jnp.broadcast_to(g_lanes, (GDN_CHUNK, GDN_CHUNK))
    dec_t = jnp.exp(jnp.minimum(g_col - g_row, 0.0))
    return jnp.where(col > row, _bdot(k, k * be, NT) * dec_t, 0.0)


def _unit_lower_inverse(lw_t):
    n = lw_t.shape[0]
    n_slab = GDN_CHUNK // 8
    row = lax.broadcasted_iota(jnp.int32, (n, 8, GDN_CHUNK), 1)
    col = lax.broadcasted_iota(jnp.int32, (n, 8, GDN_CHUNK), 2)
    unit = lax.broadcasted_iota(jnp.int32, (1, 1, GDN_CHUNK), 2)
    lw = [lw_t[:, 8 * g:8 * g + 8, :] for g in range(n_slab)]
    inv = [(col == row + 8 * g).astype(F32) for g in range(n_slab)]
    for i in range(1, GDN_CHUNK):
        acc = lw[0][:, :, i:i + 1] * inv[0]
        for g in range(1, (i - 1) // 8 + 1):
            acc = acc + lw[g][:, :, i:i + 1] * inv[g]
        new_row = (unit == i).astype(F32) - jnp.sum(acc, axis=1, keepdims=True)
        inv[i // 8] = jnp.where(row == i % 8, new_row, inv[i // 8])
    return jnp.concatenate(inv, axis=1)


def _gdn_fwd(q, k, v, be, ge, ge_t, n_heads):
    t = q.shape[0]
    h = n_heads
    hg = GDN_HEADS_PER_STEP
    tr = _pick(t, GDN_ROWS)
    nc = tr // GDN_CHUNK

    def kern(q_ref, k_ref, v_ref, b_ref, g_ref, gt_ref, o_ref, s_ref, tm_ref, state):
        consts = _gdn_consts()
        lower, tril, eye, sel = consts
        lanes = [pl.ds(hs * HEAD_DIM, HEAD_DIM) for hs in range(hg)]

        @pl.when(pl.program_id(1) == 0)
        def _():
            state[...] = jnp.zeros_like(state)

        lw_t = []
        for hs in range(hg):
            for ci in range(nc):
                rows = pl.ds(ci * GDN_CHUNK, GDN_CHUNK)
                lw_t.append(_wy_lower_t(k_ref[rows, lanes[hs]], b_ref[rows, lanes[hs]],
                                        g_ref[rows, lanes[hs]], gt_ref[hs, 0:1, rows]))
        t_all = _unit_lower_inverse(jnp.stack(lw_t))
        for hs in range(hg):
            for ci in range(nc):
                tm_ref[hs, pl.ds(ci * GDN_CHUNK, GDN_CHUNK), :] = t_all[hs * nc + ci]

        local = {}
        for ci in range(nc):
            rows = pl.ds(ci * GDN_CHUNK, GDN_CHUNK)
            for hs in range(hg):
                ln = lanes[hs]
                loc = _gdn_local(q_ref[rows, ln], k_ref[rows, ln], v_ref[rows, ln], b_ref[rows, ln],
                                 g_ref[rows, ln], gt_ref[hs, 0:1, rows], consts)
                t_mat = t_all[hs * nc + ci]
                local[ci, hs] = (_bdot(t_mat, loc["vb"]), _bdot(t_mat, loc["kg"]).astype(BF16),
                                 loc["q_dec"].astype(BF16), loc["intra"].astype(BF16),
                                 loc["k_dec"].astype(BF16), loc["gamma"])
        states = [state[hs] for hs in range(hg)]
        for ci in range(nc):
            rows = pl.ds(ci * GDN_CHUNK, GDN_CHUNK)
            for hs in range(hg):
                u, w, q_dec, intra, k_dec, gamma = local[ci, hs]
                s = states[hs]
                s_ref[hs, ci] = s
                v_new = u - _bdot(w, s)
                o_ref[rows, lanes[hs]] = _bdot(q_dec, s) + _bdot(intra, v_new)
                states[hs] = s * gamma + _bdot(k_dec, v_new, TN)
        for hs in range(hg):
            state[hs] = states[hs]

    slab = pl.BlockSpec((tr, hg * HEAD_DIM), lambda hp, j: (j, hp))
    return pl.pallas_call(
        kern, name="gdn_fwd", grid=(h // hg, t // tr),
        in_specs=[slab] * 5 + [pl.BlockSpec((hg, 8, tr), lambda hp, j: (hp, 0, j))],
        out_specs=[slab,
                   pl.BlockSpec((hg, nc, HEAD_DIM, HEAD_DIM), lambda hp, j: (hp, j, 0, 0)),
                   pl.BlockSpec((hg, tr, GDN_CHUNK), lambda hp, j: (hp, j, 0))],
        out_shape=[jax.ShapeDtypeStruct((t, h * HEAD_DIM), F32),
                   jax.ShapeDtypeStruct((h, t // GDN_CHUNK, HEAD_DIM, HEAD_DIM), F32),
                   jax.ShapeDtypeStruct((h, t, GDN_CHUNK), F32)],
        scratch_shapes=[pltpu.VMEM((hg, HEAD_DIM, HEAD_DIM), F32)],
        compiler_params=pltpu.CompilerParams(
            dimension_semantics=("arbitrary", "arbitrary"), vmem_limit_bytes=VMEM_LIMIT),
    )(q, k, v, be, ge, ge_t)


def _gdn_bwd(q, k, v, be, ge, ge_t, do, states, tms, n_heads):
    t = q.shape[0]
    h = n_heads
    hg = GDN_HEADS_PER_STEP
    tr = _pick(t, GDN_ROWS)
    nc = tr // GDN_CHUNK
    nj = t // tr

    def kern(q_ref, k_ref, v_ref, b_ref, g_ref, gt_ref, do_ref, s_ref, tm_ref,
             dq_ref, dk_ref, dv_ref, db_ref, dg_ref, dstate):
        consts = _gdn_consts()
        lower, tril, eye, sel = consts
        last_row = lax.broadcasted_iota(jnp.int32, (GDN_CHUNK, HEAD_DIM), 0) == GDN_CHUNK - 1

        @pl.when(pl.program_id(1) == 0)
        def _():
            dstate[...] = jnp.zeros_like(dstate)

        def lsum(x):
            return jnp.sum(x, axis=1, keepdims=True)

        def before(ci, hs):
            rows = pl.ds(ci * GDN_CHUNK, GDN_CHUNK)
            ln = pl.ds(hs * HEAD_DIM, HEAD_DIM)
            qv, kv, vv, bev = q_ref[rows, ln], k_ref[rows, ln], v_ref[rows, ln], b_ref[rows, ln]
            loc = _gdn_local(qv, kv, vv, bev, g_ref[rows, ln], gt_ref[hs, 0:1, rows], consts)
            t_mat = tm_ref[hs, rows, :]
            s = s_ref[hs, ci]
            d_o = do_ref[rows, ln]
            w = _bdot(t_mat, loc["kg"])
            v_new = _bdot(t_mat, loc["vb"]) - _bdot(w, s)
            return dict(loc=loc, qv=qv, kv=kv, vv=vv, bev=bev, t_mat=t_mat, s=s, w=w, v_new=v_new,
                        dv_new0=_bdot(loc["intra"], d_o, TN), ds0=_bdot(loc["q_dec"], d_o, TN),
                        d_intra=jnp.where(tril, _bdot(d_o, v_new, NT), 0.0),
                        dq_dec=_bdot(d_o, s, NT))

        def recur(c, d_s):
            loc = c["loc"]
            dv_new = c["dv_new0"] + _bdot(loc["k_dec"], d_s)
            c.update(dv_new=dv_new, dk_dec=_bdot(c["v_new"], d_s, NT),
                     dgamma=jnp.sum(lsum(d_s * c["s"]), axis=0, keepdims=True))
            return c["ds0"] + loc["gamma"] * d_s - _bdot(c["w"], dv_new, TN)

        def after(c):
            loc, qv, kv, vv, bev, t_mat, s = (c[n] for n in ("loc", "qv", "kv", "vv", "bev", "t_mat",
                                                             "s"))
            dv_new, dk_dec, dgamma, d_intra, dq_dec = (c[n] for n in ("dv_new", "dk_dec", "dgamma",
                                                                      "d_intra", "dq_dec"))
            dw = -_bdot(dv_new, s, NT)
            dtm = _bdot(dv_new, loc["vb"], NT) + _bdot(dw, loc["kg"], NT)
            dvb = _bdot(t_mat, dv_new, TN)
            dkg = _bdot(t_mat, dw, TN)
            dlw = jnp.where(lower, -_dot3(t_mat, _dot3(dtm, t_mat, NT), TN), 0.0)
            dkk = dlw * loc["dec"]
            dqk = d_intra * loc["dec"]
            ddec = dlw * loc["kk"] + d_intra * loc["qk"]
            dkb = _bdot(dkk, kv) + dkg * loc["e_g"]
            dk = (_bdot(dkk, loc["kb"], TN) + _bdot(dqk, qv, TN) + dk_dec * loc["kdec_f"]
                  + dkb * bev)
            dq = _bdot(dqk, kv) + dq_dec * loc["e_g"]
            dgd = ddec * loc["dec"]
            r_kdec = lsum(dk_dec * loc["k_dec"])
            col_sums = lsum(_dot2(dgd, sel, TN))
            d_big_g = (lsum(dgd) - col_sums
                       + lsum(dq_dec * loc["q_dec"]) - r_kdec + lsum(dkg * loc["kg"]))
            d_last = jnp.sum(r_kdec, axis=0, keepdims=True) + dgamma * loc["gamma"][:, 0:1]
            d_big_g = jnp.broadcast_to(d_big_g, (GDN_CHUNK, HEAD_DIM)) + jnp.where(last_row, d_last,
                                                                                  0.0)
            d_beta = jnp.broadcast_to(lsum(dkb * kv) + lsum(dvb * vv), (GDN_CHUNK, HEAD_DIM))
            return dq, dk, dvb * bev, d_beta, d_big_g

        work = {(ci, hs): before(ci, hs) for ci in range(nc) for hs in range(hg)}
        d_states = [dstate[hs] for hs in range(hg)]
        for ci in range(nc - 1, -1, -1):
            for hs in range(hg):
                d_states[hs] = recur(work[ci, hs], d_states[hs])
        for hs in range(hg):
            dstate[hs] = d_states[hs]
        for ci in range(nc):
            rows = pl.ds(ci * GDN_CHUNK, GDN_CHUNK)
            for hs in range(hg):
                ln = pl.ds(hs * HEAD_DIM, HEAD_DIM)
                for ref, val in zip((dq_ref, dk_ref, dv_ref, db_ref, dg_ref), after(work[ci, hs])):
                    ref[rows, ln] = val

    slab = pl.BlockSpec((tr, hg * HEAD_DIM), lambda hp, j: (nj - 1 - j, hp))
    out = jax.ShapeDtypeStruct((t, h * HEAD_DIM), F32)
    return pl.pallas_call(
        kern, name="gdn_bwd", grid=(h // hg, nj),
        in_specs=[slab] * 5 + [pl.BlockSpec((hg, 8, tr), lambda hp, j: (hp, 0, nj - 1 - j)), slab] + [
            pl.BlockSpec((hg, nc, HEAD_DIM, HEAD_DIM), lambda hp, j: (hp, nj - 1 - j, 0, 0)),
            pl.BlockSpec((hg, tr, GDN_CHUNK), lambda hp, j: (hp, nj - 1 - j, 0))],
        out_specs=[slab] * 5,
        out_shape=[out] * 5,
        scratch_shapes=[pltpu.VMEM((hg, HEAD_DIM, HEAD_DIM), F32)],
        compiler_params=pltpu.CompilerParams(
            dimension_semantics=("arbitrary", "arbitrary"), vmem_limit_bytes=VMEM_LIMIT),
    )(q, k, v, be, ge, ge_t, do, states, tms)


def _gdn_pre_bwd(proj_b, proj_c, conv_w8, ab, dq, dk, dv, dbe, dge, n_heads):
    t = proj_b.shape[0]
    h = n_heads
    d = h * HEAD_DIM
    tr = _pick(t, TR_HEAD)
    slab, halo = _gdn_specs(t, tr, h, 1)
    scale = HEAD_DIM ** -0.5

    def body(idx, q_c, q_h, k_c, k_h, v_c, v_h, wq, wk, wv, gbga, ab_v, dq_v, dk_v, dv_v, dbe_v,
             dge_v):
        i, hh = idx
        live = (i > 0).astype(F32)
        outs = []
        for cur, hal, w, dy, kind in ((q_c, q_h, wq, dq_v, "q"), (k_c, k_h, wk, dk_v, "k"),
                                      (v_c, v_h, wv, dv_v, "v")):
            u = _conv_fwd(cur, hal * live, w)
            sg = _sigmoid(u)
            if kind == "v":
                ds = dy
            else:
                s = u * sg
                r = lax.rsqrt(jnp.sum(s * s, axis=1, keepdims=True) + L2_EPS)
                y = s * r
                if kind == "q":
                    dy = dy * scale
                ds = r * (dy - y * jnp.sum(dy * y, axis=1, keepdims=True))
            outs.append(ds * (sg * (1.0 + u * (1.0 - sg))))
        lane = lax.broadcasted_iota(jnp.int32, (tr, LANES), 1)
        lane1 = lax.broadcasted_iota(jnp.int32, (1, LANES), 1)
        beta = _sigmoid(_lane_pick(gbga, hh))
        a_neg = -jnp.exp(_lane_pick(ab_v[0:1, :], hh))
        xg = _lane_pick(gbga, h + hh) + _lane_pick(ab_v[1:2, :], hh)
        g = a_neg * _softplus(xg)
        dgb = dbe_v * (beta * (1.0 - beta))
        dg = _cdot3(_chunk_tri(tr, True), dge_v)
        dga = dg * (a_neg * _sigmoid(xg))
        d_gates = jnp.where(lane == hh, dgb, 0.0) + jnp.where(lane == h + hh, dga, 0.0)
        d_ab = (jnp.where(lane1 == hh, jnp.sum(dg * g, axis=0, keepdims=True), 0.0)
                + jnp.where(lane1 == h + hh, jnp.sum(dga, axis=0, keepdims=True), 0.0))
        return outs + [d_gates, d_ab]

    wspec = lambda seg: ((8, HEAD_DIM), lambda i, hh: (0, seg * h + hh))
    hs = ((tr, HEAD_DIM), lambda i, hh: (i, hh))
    out = ((t, d), F32) + hs + ("set",)
    return _rowcall(
        "gdn_pre_bwd", body, (t // tr, h),
        [(proj_b,) + slab(1), (proj_b,) + halo(1), (proj_b,) + slab(2), (proj_b,) + halo(2),
         (proj_b,) + slab(3), (proj_b,) + halo(3),
         (conv_w8,) + wspec(0), (conv_w8,) + wspec(1), (conv_w8,) + wspec(2),
         (proj_c, (tr, LANES), lambda i, hh: (i, 0)), _full(ab),
         (dq,) + hs, (dk,) + hs, (dv,) + hs, (dbe,) + hs, (dge,) + hs],
        [out, out, out,
         ((t, LANES), F32, (tr, LANES), lambda i, hh: (i, 0), "acc_inner"),
         ((1, LANES), F32, (1, LANES), lambda i, hh: (0, 0), "acc_all")])


def _conv_bwd(proj_b, conv_w8, du_q, du_k, du_v, n_heads):
    t = proj_b.shape[0]
    h = n_heads
    d = h * HEAD_DIM
    tr = _pick(t, TR_HEAD)
    ni = t // tr

    def body(idx, q_c, q_h, k_c, k_h, v_c, v_h, wq, wk, wv, uq, uq_n, uk, uk_n, uv, uv_n):
        hh, i = idx
        live_b = (i > 0).astype(F32)
        live_a = (i < ni - 1).astype(F32)
        d_ins, d_ws = [], []
        for cur, hal, w, du, du_n in ((q_c, q_h, wq, uq, uq_n), (k_c, k_h, wk, uk, uk_n),
                                      (v_c, v_h, wv, uv, uv_n)):
            hal = hal * live_b
            du_n = du_n * live_a
            d_in = du * w[GDN_CONV - 1:GDN_CONV, :]
            rows = [jnp.sum(du * cur, axis=0, keepdims=True)]
            for i_tap in range(GDN_CONV - 2, -1, -1):
                kshift = GDN_CONV - 1 - i_tap
                d_in = d_in + _shift_rows(du, du_n, kshift, False) * w[i_tap:i_tap + 1, :]
                rows.insert(0, jnp.sum(du * _shift_rows(cur, hal, kshift, True), axis=0,
                                       keepdims=True))
            d_ins.append(d_in)
            tap = lax.broadcasted_iota(jnp.int32, (8, HEAD_DIM), 0)
            d_w = jnp.zeros((8, HEAD_DIM), F32)
            for i_tap in range(GDN_CONV):
                d_w = d_w + jnp.where(tap == i_tap, rows[i_tap], 0.0)
            d_ws.append(d_w)
        return d_ins + d_ws

    def slab(seg):
        return ((tr, HEAD_DIM), lambda hh, i: (i, seg * h + hh))

    def halo_b(seg):
        return ((8, HEAD_DIM), lambda hh, i: (jnp.maximum(i * (tr // 8) - 1, 0), seg * h + hh))

    hs = ((tr, HEAD_DIM), lambda hh, i: (i, hh))
    halo_a = ((8, HEAD_DIM), lambda hh, i: (jnp.minimum((i + 1) * (tr // 8), t // 8 - 1), hh))
    wspec = lambda seg: ((8, HEAD_DIM), lambda hh, i: (0, seg * h + hh))
    wout = ((8, d), F32, (8, HEAD_DIM), lambda hh, i: (0, hh), "acc_inner")
    out = ((t, d), BF16) + hs + ("set",)
    res = _rowcall(
        "conv_bwd", body, (h, ni),
        [(proj_b,) + slab(1), (proj_b,) + halo_b(1), (proj_b,) + slab(2), (proj_b,) + halo_b(2),
         (proj_b,) + slab(3), (proj_b,) + halo_b(3),
         (conv_w8,) + wspec(0), (conv_w8,) + wspec(1), (conv_w8,) + wspec(2),
         (du_q,) + hs, (du_q,) + halo_a, (du_k,) + hs, (du_k,) + halo_a, (du_v,) + hs,
         (du_v,) + halo_a],
        [out, out, out, wout, wout, wout])
    return res


def _gdn_post_fwd(o, proj_b, gnw, n_heads):
    t, d = o.shape
    h = n_heads
    tr = _pick(t, TR_HEAD)

    def body(idx, o_v, z, w):
        rstd = lax.rsqrt(jnp.mean(o_v * o_v, axis=1, keepdims=True) + NORM_EPS)
        return ((o_v * rstd) * w * (z * _sigmoid(z)),)

    hs = ((tr, HEAD_DIM), lambda i, hh: (i, hh))
    return _rowcall("gdn_post_fwd", body, (t // tr, h),
                    [(o,) + hs, (proj_b, (tr, HEAD_DIM), lambda i, hh: (i, 4 * h + hh)), _full(gnw)],
                    [((t, d), BF16) + hs + ("set",)])[0]


def _gdn_post_bwd(do_gdn, o, proj_b, gnw, n_heads):
    t, d = o.shape
    h = n_heads
    tr = _pick(t, TR_HEAD)

    def body(idx, dog, o_v, z, w):
        rstd = lax.rsqrt(jnp.mean(o_v * o_v, axis=1, keepdims=True) + NORM_EPS)
        n = o_v * rstd
        sg = _sigmoid(z)
        don = dog * (z * sg)
        dz = dog * (n * w) * (sg * (1.0 + z * (1.0 - sg)))
        dn = don * w
        d_o = rstd * (dn - n * jnp.mean(dn * n, axis=1, keepdims=True))
        return d_o, dz, jnp.sum(don * n, axis=0, keepdims=True)

    hs = ((tr, HEAD_DIM), lambda i, hh: (i, hh))
    return _rowcall("gdn_post_bwd", body, (t // tr, h),
                    [(do_gdn,) + hs, (o,) + hs,
                     (proj_b, (tr, HEAD_DIM), lambda i, hh: (i, 4 * h + hh)), _full(gnw)],
                    [((t, d), F32) + hs + ("set",), ((t, d), BF16) + hs + ("set",),
                     ((1, HEAD_DIM), F32, (1, HEAD_DIM), lambda i, hh: (0, 0), "acc_all")])


def _seg(arr, tr, d, seg):
    return (arr, (tr, d), lambda i: (i, seg))


def _sb_post_fwd(o_raw, proj_b):
    t, d = o_raw.shape
    tr = _pick(t, TR_WIDE)

    def body(idx, o_v, z):
        return (o_v * (z * _sigmoid(z)),)

    return _rowcall("sb_post_fwd", body, (t // tr,), [_seg(o_raw, tr, d, 0), _seg(proj_b, tr, d, 0)],
                    [((t, d), BF16, (tr, d), lambda i: (i, 0), "set")])[0]


def _sb_post_bwd(do_sb, o_raw, proj_b):
    t, d = o_raw.shape
    tr = _pick(t, TR_WIDE)

    def body(idx, dos, o_v, z):
        sg = _sigmoid(z)
        return dos * (z * sg), dos * o_v * (sg * (1.0 + z * (1.0 - sg)))

    out = ((t, d), BF16, (tr, d), lambda i: (i, 0), "set")
    return _rowcall("sb_post_bwd", body, (t // tr,),
                    [_seg(do_sb, tr, d, 0), _seg(o_raw, tr, d, 0), _seg(proj_b, tr, d, 0)],
                    [out, out])


def _merge_fwd(proj_b, p_sb, p_gdn):
    t, d = p_sb.shape
    tr = _pick(t, TR_WIDE)

    def body(idx, m_sb, m_gdn, ps, pg):
        return (_sigmoid(m_sb) * ps + _sigmoid(m_gdn) * pg,)

    return _rowcall("merge_fwd", body, (t // tr,),
                    [_seg(proj_b, tr, d, 5), _seg(proj_b, tr, d, 6), _seg(p_sb, tr, d, 0),
                     _seg(p_gdn, tr, d, 0)],
                    [((t, d), BF16, (tr, d), lambda i: (i, 0), "set")])[0]


def _merge_bwd(dy, proj_b, p_sb, p_gdn):
    t, d = p_sb.shape
    tr = _pick(t, TR_WIDE)

    def body(idx, dy_v, m_sb, m_gdn, ps, pg):
        s1 = _sigmoid(m_sb)
        s2 = _sigmoid(m_gdn)
        return s1 * dy_v, s2 * dy_v, dy_v * ps * (s1 * (1.0 - s1)), dy_v * pg * (s2 * (1.0 - s2))

    out = ((t, d), BF16, (tr, d), lambda i: (i, 0), "set")
    return _rowcall("merge_bwd", body, (t // tr,),
                    [_seg(dy, tr, d, 0), _seg(proj_b, tr, d, 5), _seg(proj_b, tr, d, 6),
                     _seg(p_sb, tr, d, 0), _seg(p_gdn, tr, d, 0)],
                    [out] * 4)


def _tail(x, r, target, gate, final_w):
    t, d = x.shape
    tr = _pick(t, TR_WIDE)

    def body(idx, x_v, r_v, tg, gt, fw):
        x2 = x_v + gt * r_v
        rstd = lax.rsqrt(jnp.mean(x2 * x2, axis=1, keepdims=True) + NORM_EPS)
        n = x2 * rstd
        diff = n * fw - tg
        loss = 0.5 * jnp.sum(jnp.mean(diff * diff, axis=1, keepdims=True), axis=0, keepdims=True)
        dout = diff * (1.0 / d)
        dn = dout * fw
        dx2 = rstd * (dn - n * jnp.mean(dn * n, axis=1, keepdims=True))
        return (dx2, gt * dx2, jnp.sum(dout * n, axis=0, keepdims=True),
                jnp.sum(dx2 * r_v, axis=0, keepdims=True), jnp.broadcast_to(loss, (1, LANES)))

    rb = ((tr, d), lambda i: (i, 0))
    vec = ((1, d), F32, (1, d), lambda i: (0, 0), "acc_all")
    return _rowcall("tail", body, (t // tr,),
                    [(x,) + rb, (r,) + rb, (target,) + rb, _full(gate), _full(final_w)],
                    [((t, d), F32) + rb + ("set",), ((t, d), BF16) + rb + ("set",), vec, vec,
                     ((1, LANES), F32, (1, LANES), lambda i: (0, 0), "acc_all")])


def _pad_to(a, rows, cols):
    return jnp.pad(a, ((0, rows - a.shape[0]), (0, cols - a.shape[1])))


def kernel(x, c, w_ada, b_ada, norm_w, w_in, gdn_conv_w, gdn_a_log, gdn_dt_bias, gdn_norm_w, w_proj_sb, w_proj_gdn, w_out, final_norm_w, loss_target, m_w_ada, m_b_ada, m_norm_w, m_w_in, m_gdn_conv_w, m_gdn_a_log, m_gdn_dt_bias, m_gdn_norm_w, m_w_proj_sb, m_w_proj_gdn, m_w_out, m_final_norm_w, v_w_ada, v_b_ada, v_norm_w, v_w_in, v_gdn_conv_w, v_gdn_a_log, v_gdn_dt_bias, v_gdn_norm_w, v_w_proj_sb, v_w_proj_gdn, v_w_out, v_final_norm_w):
    t, d = x.shape[1], x.shape[2]
    h = d // HEAD_DIM
    me = 4 * lax.axis_index("x") + 2 * lax.axis_index("y") + lax.axis_index("c")
    x2d = x[0]
    tgt = loss_target[0]
    ada_cols = w_ada.shape[2]
    in_cols = w_in.shape[2]
    rows_p = w_out.shape[1]

    w_in_all = _exchange("gather_w_in", w_in[0].astype(BF16), True)
    w_in_full = jnp.transpose(w_in_all, (1, 0, 2)).reshape(d, N_DEV * in_cols)
    w_main = jnp.concatenate([w_in_full[:, :8 * d], w_in_full[:, 8 * d + 2 * h:]], axis=1)
    w_g = _pad_to(w_in_full[:, 8 * d:8 * d + 2 * h], d, LANES)
    w_main_t = w_main.T
    w_g_t = w_g.T
    w_sq = jnp.stack([w_proj_sb[0], w_proj_gdn[0], w_out[0]]).astype(BF16)
    conv_all = _exchange("gather_conv", _pad_to(gdn_conv_w[0], 8, gdn_conv_w.shape[2]), True)
    conv_w8 = jnp.transpose(conv_all, (1, 0, 2)).reshape(8, 3 * d)
    c_all = _exchange("gather_c", _pad_to(c, 8, d), True)[:, 0, :]

    sc_all = c_all * _sigmoid(c_all)
    mod_part = _matmul("ada_fwd", _pad_to(sc_all, 16, d).astype(BF16), w_ada[0].astype(BF16), F32)
    mod_part = mod_part[:N_DEV] + lax.dynamic_slice(b_ada, (0, me * ada_cols), (1, ada_cols))
    mod_rows = _exchange("a2a_mod", _pad_to(mod_part, 8, ada_cols).reshape(N_DEV, 1, ada_cols)
                         * jnp.ones((1, 8, 1), F32), False)
    mod = mod_rows[:, 0, :].reshape(1, 3 * d)
    shift, scale, gate = mod[:, :d], mod[:, d:2 * d], mod[:, 2 * d:]

    hmod = _norm_mod_fwd(x2d, shift, scale, norm_w)
    proj_a = _matmul("in_proj_a", hmod, w_main, BF16, n_cols=3 * d, col0=0)
    proj_b, w_sq_all = _matmul("in_proj_b", hmod, w_main, F32, n_cols=7 * d, col0=3 * d,
                               xchg=w_sq, gather=True)
    w_sq_full = jnp.transpose(w_sq_all, (1, 0, 2, 3)).reshape(3, d, d)
    wp_sb, wp_gdn, wo = w_sq_full[0], w_sq_full[1], w_sq_full[2]
    proj_c = _matmul("in_proj_c", hmod, w_g, F32)
    o_sb_raw, ctot, sb_visited = _sb_fwd(proj_a, h)
    o_sb = _sb_post_fwd(o_sb_raw, proj_b)
    ab = _pad_to(jnp.concatenate([gdn_a_log, gdn_dt_bias], axis=0), 8, LANES)
    gq, gk, gv, g_beta, g_cum = _gdn_pre_fwd(proj_b, proj_c, conv_w8, ab, h)
    g_cum_t = jnp.broadcast_to(g_cum[:, ::HEAD_DIM].T[:, None, :], (h, 8, t))
    o_gdn_raw, states, tms = _gdn_fwd(gq, gk, gv, g_beta, g_cum, g_cum_t, h)
    o_gdn = _gdn_post_fwd(o_gdn_raw, proj_b, gdn_norm_w, h)
    p_sb = _matmul("proj_sb", o_sb, wp_sb, F32)
    p_gdn = _matmul("proj_gdn", o_gdn, wp_gdn, F32)
    y = _merge_fwd(proj_b, p_sb, p_gdn)
    r = _matmul("out_proj", y, wo, F32)
    dx2, dr, d_final_w, d_gate, loss_part = _tail(x2d, r, tgt, gate, final_norm_w.reshape(1, d))

    dy = _matmul("d_out_proj", dr, wo.T, F32)
    dw_out = _matmul("dw_out", y.T, dr, BF16)
    dp_sb, dp_gdn, dm_sb, dm_gdn = _merge_bwd(dy, proj_b, p_sb, p_gdn)
    do_sb = _matmul("d_proj_sb", dp_sb, wp_sb.T, F32)
    dw_p_sb = _matmul("dw_proj_sb", o_sb.T, dp_sb, BF16)
    do_gdn = _matmul("d_proj_gdn", dp_gdn, wp_gdn.T, F32)
    dw_p_gdn = _matmul("dw_proj_gdn", o_gdn.T, dp_gdn, BF16)
    do_sb_raw, d_sbz = _sb_post_bwd(do_sb, o_sb_raw, proj_b)
    d_sbq, d_sbk, d_sbv = _sb_bwd(proj_a, do_sb_raw, ctot, sb_visited, h)
    d_o_gdn, d_gz, d_gnw = _gdn_post_bwd(do_gdn, o_gdn_raw, proj_b, gdn_norm_w, h)
    dgq, dgk, dgv, dgbe, dgcum = _gdn_bwd(gq, gk, gv, g_beta, g_cum, g_cum_t, d_o_gdn, states, tms, h)
    du_q, du_k, du_v, d_gates, d_ab = _gdn_pre_bwd(proj_b, proj_c, conv_w8, ab, dgq, dgk, dgv,
                                                  dgbe, dgcum, h)
    d_gq, d_gk, d_gv, dcw_q, dcw_k, dcw_v = _conv_bwd(proj_b, conv_w8, du_q, du_k, du_v, h)
    dproj = jnp.concatenate([d_sbq, d_sbk, d_sbv, d_sbz, d_gq, d_gk, d_gv, d_gz, dm_sb, dm_gdn],
                            axis=1)
    d_gates_b = d_gates.astype(BF16)
    hmod_t = hmod.T
    dw_sq = jnp.stack([dw_p_sb, dw_p_gdn, dw_out]).reshape(3, N_DEV, rows_p, d)
    dw_main, dw_sq_parts = _matmul("dw_in", hmod_t, dproj, BF16,
                                   xchg=jnp.transpose(dw_sq, (1, 0, 2, 3)))
    dw_g = _matmul("dw_in_g", hmod_t, d_gates_b, BF16)
    dw_in_full = jnp.concatenate([dw_main[:, :8 * d], dw_g[:, :2 * h], dw_main[:, 8 * d:]], axis=1)
    dh_a, dw_in_parts = _matmul(
        "d_in_proj", dproj, w_main_t, F32,
        xchg=jnp.transpose(dw_in_full.reshape(d, N_DEV, in_cols), (1, 0, 2)))
    dh_b = _matmul("d_in_proj_g", d_gates_b, w_g_t, F32)
    grad_x, d_shift, d_scale, d_norm_w = _norm_mod_bwd(dh_a, dh_b, x2d, dx2, scale, norm_w)

    dmod = jnp.concatenate([d_shift, d_scale, d_gate], axis=1)
    small = jnp.concatenate([dmod, d_norm_w, d_final_w, d_ab[:, :h], d_ab[:, h:2 * h], d_gnw,
                             loss_part], axis=1)
    n_small = small.shape[1]
    small_all = _exchange("gather_small", _pad_to(small, 8, n_small), True)[:, 0:1, :]
    small_w = jnp.concatenate([b_ada, norm_w, final_norm_w.reshape(1, d), gdn_a_log, gdn_dt_bias,
                               gdn_norm_w, jnp.zeros((1, LANES), F32)], axis=1)
    small_m = jnp.concatenate([m_b_ada, m_norm_w, m_final_norm_w.reshape(1, d), m_gdn_a_log,
                               m_gdn_dt_bias, m_gdn_norm_w, jnp.zeros((1, LANES), F32)], axis=1)
    small_v = jnp.concatenate([v_b_ada, v_norm_w, v_final_norm_w.reshape(1, d), v_gdn_a_log,
                               v_gdn_dt_bias, v_gdn_norm_w, jnp.ones((1, LANES), F32)], axis=1)
    s_g, s_d, s_m, s_v = _adamw("adamw_small", small_all, small_w, small_m, small_v)
    cuts = [3 * d, 4 * d, 5 * d, 5 * d + h, 5 * d + 2 * h, 5 * d + 2 * h + HEAD_DIM]

    def split_small(a):
        b, nw, fw, al, dtb, gn, _ = jnp.split(a, cuts, axis=1)
        return b, nw, fw.reshape(d), al, dtb, gn

    loss = s_g[0, cuts[-1]]

    dmod_all = small_all[:, 0, :3 * d]
    dmod_mine = lax.dynamic_slice(dmod_all, (0, me * ada_cols), (N_DEV, ada_cols))
    dw_ada = _matmul("dw_ada", _pad_to(sc_all.T, d, LANES).astype(BF16),
                     _pad_to(dmod_mine, LANES, ada_cols).astype(BF16), F32)
    ada = _adamw("adamw_ada", dw_ada[None], w_ada[0], m_w_ada[0], v_w_ada[0])

    win = _adamw("adamw_w_in", dw_in_parts, w_in[0], m_w_in[0], v_w_in[0])
    sq = _adamw("adamw_sq", dw_sq_parts.reshape(N_DEV, 3 * rows_p, d),
                jnp.concatenate([w_proj_sb[0], w_proj_gdn[0], w_out[0]], axis=0),
                jnp.concatenate([m_w_proj_sb[0], m_w_proj_gdn[0], m_w_out[0]], axis=0),
                jnp.concatenate([v_w_proj_sb[0], v_w_proj_gdn[0], v_w_out[0]], axis=0))
    dcw = jnp.concatenate([dcw_q, dcw_k, dcw_v], axis=1)
    cw_cols = gdn_conv_w.shape[2]
    dcw_parts = _exchange("a2a_dconv",
                          jnp.transpose(dcw.reshape(8, N_DEV, cw_cols), (1, 0, 2)), False)
    cw = _adamw("adamw_conv", dcw_parts, _pad_to(gdn_conv_w[0], 8, cw_cols),
                _pad_to(m_gdn_conv_w[0], 8, cw_cols),
                jnp.pad(v_gdn_conv_w[0], ((0, 8 - GDN_CONV), (0, 0)), constant_values=1.0))

    outs = [loss, grad_x[None]]
    for k_out in range(4):
        b, nw, fw, al, dtb, gn = split_small((s_g, s_d, s_m, s_v)[k_out])
        sq3 = sq[k_out].reshape(3, 1, rows_p, d)
        outs += [ada[k_out][None], b, nw, win[k_out][None], cw[k_out][None, :GDN_CONV], al, dtb, gn,
                 sq3[0], sq3[1], sq3[2], fw]
    return tuple(outs)
```

```python
import jax
import jax.numpy as jnp
from jax import lax
from jax.experimental import pallas as pl
from jax.experimental.pallas import tpu as pltpu

F32 = jnp.float32
BF16 = jnp.bfloat16
N_DEV = 8
HEAD_DIM = 128
LANES = 128
GDN_CHUNK = 64
GDN_CONV = 4
NORM_EPS = 1e-6
L2_EPS = 1e-6
ADAM_LR = 0.001
ADAM_B1 = 0.9
ADAM_B2 = 0.999
ADAM_EPS = 1e-08
ADAM_WD = 0.01
ADAM_STEP = 10
VMEM_LIMIT = 56 * 1024 * 1024
MESH = pl.DeviceIdType.MESH
NT = (((1,), (1,)), ((), ()))
TN = (((0,), (0,)), ((), ()))


def _pick(n, pref):
    t = min(pref, n)
    while n % t:
        t //= 2
    return t


def _sigmoid(x):
    return 1.0 / (1.0 + jnp.exp(-x))


def _bdot(a, b, dims=None):
    a = a.astype(BF16)
    b = b.astype(BF16)
    if dims is None:
        return jnp.dot(a, b, preferred_element_type=F32)
    return lax.dot_general(a, b, dims, preferred_element_type=F32)


def _split2(x):
    hi = x.astype(BF16)
    lo = (x - hi.astype(F32)).astype(BF16)
    return hi, lo


def _split3(x):
    p1 = x.astype(BF16)
    r = x - p1.astype(F32)
    p2 = r.astype(BF16)
    p3 = (r - p2.astype(F32)).astype(BF16)
    return p1, p2, p3


def _dot2(x, c, dims=None):
    hi, lo = _split2(x)
    return _bdot(hi, c, dims) + _bdot(lo, c, dims)


def _cdot3(c, x):
    p1, p2, p3 = _split3(x)
    return _bdot(c, p1) + _bdot(c, p2) + _bdot(c, p3)


def _dot3(a, b, dims=None):
    ah, al = _split2(a)
    bh, bl = _split2(b)
    return _bdot(ah, bh, dims) + (_bdot(ah, bl, dims) + _bdot(al, bh, dims))


def _dot6(a, b, dims=None):
    a1, a2, a3 = _split3(a)
    b1, b2, b3 = _split3(b)
    small = (_bdot(a1, b3, dims) + _bdot(a3, b1, dims)) + _bdot(a2, b2, dims)
    return _bdot(a1, b1, dims) + ((_bdot(a1, b2, dims) + _bdot(a2, b1, dims)) + small)


def _full(a):
    nd = a.ndim
    return (a, a.shape, lambda *idx: (0,) * nd)


def _rowcall(name, body, grid, ins, outs):
    n_in = len(ins)
    modes = [o[4] for o in outs]
    n_ax = len(grid)

    def kern(*refs):
        idx = tuple(pl.program_id(a) for a in range(n_ax))
        vals = body(idx, *[r[...] for r in refs[:n_in]])
        first_all = idx[0] == 0
        for a in range(1, n_ax):
            first_all = jnp.logical_and(first_all, idx[a] == 0)
        first_inner = idx[-1] == 0
        for r, v, mode in zip(refs[n_in:], vals, modes):
            v = v.astype(r.dtype)
            if mode == "set":
                r[...] = v
            else:
                first = first_all if mode == "acc_all" else first_inner

                @pl.when(first)
                def _(r=r, v=v):
                    r[...] = v

                @pl.when(jnp.logical_not(first))
                def _(r=r, v=v):
                    r[...] += v

    return pl.pallas_call(
        kern, name=name, grid=grid,
        in_specs=[pl.BlockSpec(b, m) for (_, b, m) in ins],
        out_specs=[pl.BlockSpec(o[2], o[3]) for o in outs],
        out_shape=[jax.ShapeDtypeStruct(o[0], o[1]) for o in outs],
        compiler_params=pltpu.CompilerParams(
            dimension_semantics=("arbitrary",) * n_ax, vmem_limit_bytes=VMEM_LIMIT),
    )(*[a for (a, _, _) in ins])


EXCHANGE_SEMS = [pltpu.SemaphoreType.DMA((N_DEV - 1,)), pltpu.SemaphoreType.DMA((N_DEV - 1,)),
                 pltpu.SemaphoreType.DMA]


def _exchange_shape(x, gather):
    return jax.ShapeDtypeStruct((N_DEV,) + tuple(x.shape if gather else x.shape[1:]), x.dtype)


def _exchange_copies(x_ref, out_ref, send_sems, recv_sems, local_sem, gather, start, wait):
    xi, yi, ci = lax.axis_index("x"), lax.axis_index("y"), lax.axis_index("c")
    me = 4 * xi + 2 * yi + ci
    mine = pltpu.make_async_copy(x_ref if gather else x_ref.at[me], out_ref.at[me], local_sem)
    if start:
        mine.start()
    for k in range(1, N_DEV):
        kx, ky, kc = (k >> 2) & 1, (k >> 1) & 1, k & 1
        pid = me ^ k
        src = x_ref if gather else x_ref.at[pid]
        if start:
            pltpu.make_async_remote_copy(
                src_ref=src, dst_ref=out_ref.at[me], send_sem=send_sems.at[k - 1],
                recv_sem=recv_sems.at[k - 1], device_id=(xi ^ kx, yi ^ ky, ci ^ kc),
                device_id_type=MESH).start()
        if wait:
            pltpu.make_async_remote_copy(
                src_ref=src, dst_ref=out_ref.at[pid], send_sem=send_sems.at[k - 1],
                recv_sem=recv_sems.at[k - 1], device_id=(xi, yi, ci), device_id_type=MESH).wait()
    if wait:
        mine.wait()


def _matmul(name, a, b, out_dtype, n_cols=None, col0=0, tm=1024, tn=1024, tk=2048, xchg=None,
            gather=False):
    m, k = a.shape
    n = b.shape[1] if n_cols is None else n_cols
    tm = _pick(m, tm)
    tn = _pick(n, tn)
    while col0 % tn:
        tn //= 2
    tk = _pick(k, tk)
    nk = k // tk
    cb = col0 // tn
    grid = (m // tm, n // tn, nk)

    def kern(a_ref, b_ref, *rest):
        if xchg is None:
            o_ref, acc_ref = rest
        else:
            x_ref, o_ref, xo_ref, acc_ref, send_sems, recv_sems, local_sem = rest
            step = (pl.program_id(0) * grid[1] + pl.program_id(1)) * nk + pl.program_id(2)

            @pl.when(step == 0)
            def _():
                _exchange_copies(x_ref, xo_ref, send_sems, recv_sems, local_sem, gather, True, False)

        kk = pl.program_id(2)
        part = jnp.dot(a_ref[...], b_ref[...], preferred_element_type=F32)
        if nk == 1:
            o_ref[...] = part.astype(o_ref.dtype)
        else:
            @pl.when(kk == 0)
            def _():
                acc_ref[...] = part

            @pl.when(kk > 0)
            def _():
                acc_ref[...] += part

            @pl.when(kk == nk - 1)
            def _():
                o_ref[...] = acc_ref[...].astype(o_ref.dtype)

        if xchg is not None:
            @pl.when(step == grid[0] * grid[1] * nk - 1)
            def _():
                _exchange_copies(x_ref, xo_ref, send_sems, recv_sems, local_sem, gather, False, True)

    in_specs = [pl.BlockSpec((tm, tk), lambda i, j, kk: (i, kk)),
                pl.BlockSpec((tk, tn), lambda i, j, kk: (kk, j + cb))]
    out_specs = [pl.BlockSpec((tm, tn), lambda i, j, kk: (i, j))]
    out_shape = [jax.ShapeDtypeStruct((m, n), out_dtype)]
    scratch = [pltpu.VMEM((tm, tn), F32)]
    args = (a, b)
    if xchg is not None:
        in_specs.append(pl.BlockSpec(memory_space=pl.ANY))
        out_specs.append(pl.BlockSpec(memory_space=pl.ANY))
        out_shape.append(_exchange_shape(xchg, gather))
        scratch += EXCHANGE_SEMS
        args = (a, b, xchg)
    res = pl.pallas_call(
        kern, name=name, grid=grid, in_specs=in_specs, out_specs=out_specs, out_shape=out_shape,
        scratch_shapes=scratch,
        compiler_params=pltpu.CompilerParams(
            dimension_semantics=("arbitrary", "arbitrary", "arbitrary"),
            vmem_limit_bytes=VMEM_LIMIT),
    )(*args)
    return res[0] if xchg is None else (res[0], res[1])


def _exchange(name, x, gather):
    def body(x_ref, out_ref, send_sems, recv_sems, local_sem):
        _exchange_copies(x_ref, out_ref, send_sems, recv_sems, local_sem, gather, True, True)

    return pl.pallas_call(
        body, name=name,
        out_shape=_exchange_shape(x, gather),
        in_specs=[pl.BlockSpec(memory_space=pl.ANY)],
        out_specs=pl.BlockSpec(memory_space=pl.ANY),
        scratch_shapes=EXCHANGE_SEMS,
    )(x)


def _gather_two_level(name, x):
    def body(x_ref, out_ref, send_sems, recv_sems, local_sem):
        xi, yi, ci = lax.axis_index("x"), lax.axis_index("y"), lax.axis_index("c")
        me, sibling = (xi, yi, ci), (xi, yi, 1 - ci)
        chips = [(1 - xi, yi), (xi, 1 - yi), (1 - xi, 1 - yi)]

        def slot(px, py, pc):
            return out_ref.at[4 * px + 2 * py + pc]

        def copy(k, block, to, src=None):
            return pltpu.make_async_remote_copy(
                src_ref=slot(*block) if src is None else src, dst_ref=slot(*block),
                send_sem=send_sems.at[k], recv_sem=recv_sems.at[k], device_id=to,
                device_id_type=MESH)

        mine = pltpu.make_async_copy(x_ref, slot(*me), local_sem)
        mine.start()
        first = [copy(0, me, sibling, src=x_ref)]
        first += [copy(1 + j, me, (*chip, ci), src=x_ref) for j, chip in enumerate(chips)]
        for cp in first:
            cp.start()
        passed = [copy(4 + j, (*chip, ci), sibling) for j, chip in enumerate(chips)]
        for j, chip in enumerate(chips):
            copy(1 + j, (*chip, ci), me).wait_recv()
            passed[j].start()
        copy(0, sibling, me).wait_recv()
        for j, chip in enumerate(chips):
            copy(4 + j, (*chip, 1 - ci), me).wait_recv()
        for cp in first + passed:
            cp.wait_send()
        mine.wait()

    return pl.pallas_call(
        body, name=name,
        out_shape=_exchange_shape(x, True),
        in_specs=[pl.BlockSpec(memory_space=pl.ANY)],
        out_specs=pl.BlockSpec(memory_space=pl.ANY),
        scratch_shapes=EXCHANGE_SEMS,
    )(x)


def _adamw(name, parts, w, m, v):
    p, r, c = parts.shape
    tr = r if r <= 64 else _pick(r, 64)

    def body(idx, parts_v, w_v, m_v, v_v):
        g = parts_v[0].astype(F32)
        for s in range(1, p):
            g = g + parts_v[s].astype(F32)
        m2 = ADAM_B1 * m_v + (1.0 - ADAM_B1) * g
        v2 = ADAM_B2 * v_v + (1.0 - ADAM_B2) * (g * g)
        m_hat = m2 / (1.0 - ADAM_B1 ** ADAM_STEP)
        v_hat = v2 / (1.0 - ADAM_B2 ** ADAM_STEP)
        delta = -ADAM_LR * (m_hat / (jnp.sqrt(v_hat) + ADAM_EPS) + ADAM_WD * w_v)
        return g, delta, m2, v2

    rb = ((tr, c), lambda i: (i, 0))
    return _rowcall(
        name, body, (r // tr,),
        [(parts, (p, tr, c), lambda i: (0, i, 0)), (w,) + rb, (m,) + rb, (v,) + rb],
        [((r, c), F32) + rb + ("set",)] * 4)


TR_WIDE = 256


def _norm_mod_fwd(x, shift, scale, norm_w):
    t, d = x.shape
    tr = _pick(t, TR_WIDE)

    def body(idx, x_v, sh, sc, nw):
        rstd = lax.rsqrt(jnp.mean(x_v * x_v, axis=1, keepdims=True) + NORM_EPS)
        return ((x_v * rstd) * nw * (1.0 + sc) + sh,)

    rb = ((tr, d), lambda i: (i, 0))
    return _rowcall("norm_mod_fwd", body, (t // tr,),
                    [(x,) + rb, _full(shift), _full(scale), _full(norm_w)],
                    [((t, d), BF16) + rb + ("set",)])[0]


def _norm_mod_bwd(dh_a, dh_b, x, dx2, scale, norm_w):
    t, d = x.shape
    tr = _pick(t, TR_WIDE)

    def body(idx, dha, dhb, x_v, dx2_v, sc, nw):
        dh = dha + dhb
        rstd = lax.rsqrt(jnp.mean(x_v * x_v, axis=1, keepdims=True) + NORM_EPS)
        xn = x_v * rstd
        m1 = 1.0 + sc
        dxn = dh * nw * m1
        dx = rstd * (dxn - xn * jnp.mean(dxn * xn, axis=1, keepdims=True))
        dhx = dh * xn
        return (dx2_v + dx,
                jnp.sum(dh, axis=0, keepdims=True),
                jnp.sum(dhx * nw, axis=0, keepdims=True),
                jnp.sum(dhx * m1, axis=0, keepdims=True))

    rb = ((tr, d), lambda i: (i, 0))
    vec = ((1, d), F32, (1, d), lambda i: (0, 0), "acc_all")
    return _rowcall("norm_mod_bwd", body, (t // tr,),
                    [(dh_a,) + rb, (dh_b,) + rb, (x,) + rb, (dx2,) + rb, _full(scale), _full(norm_w)],
                    [((t, d), F32) + rb + ("set",), vec, vec, vec])


SB_TILE = 512


SB_SCAN = 256
SB_DEAD = 105.0


def _neg_log_not_beta(z):
    return jnp.maximum(z, 0.0) + jnp.log(1.0 + jnp.exp(-jnp.abs(z)))


def _key_scan(x, tri, later):
    n_blk = x.shape[1] // SB_SCAN
    parts = [x[:, g * SB_SCAN:(g + 1) * SB_SCAN] for g in range(n_blk)]
    sums = [jnp.sum(p, axis=1, keepdims=True) for p in parts]
    outs = []
    for g in range(n_blk):
        o = _dot2(parts[g], tri)
        others = range(g + 1, n_blk) if later else range(g)
        for g2 in others:
            o = o + sums[g2]
        outs.append(o)
    total = sums[0]
    for s in sums[1:]:
        total = total + s
    return (outs[0] if n_blk == 1 else jnp.concatenate(outs, axis=1)), total


def _scan_tri(kind):
    row = lax.broadcasted_iota(jnp.int32, (SB_SCAN, SB_SCAN), 0)
    col = lax.broadcasted_iota(jnp.int32, (SB_SCAN, SB_SCAN), 1)
    return {"after": row > col, "upto": row <= col, "before": row < col}[kind].astype(BF16)


def _sb_fwd(qkv, n_heads):
    t = qkv.shape[0]
    tq = _pick(t, SB_TILE)
    nq = t // tq
    scale = HEAD_DIM ** -0.5

    def kern(q_ref, k_ref, v_ref, o_ref, c_ref, nd_ref):
        i_blk = pl.program_id(1)
        row = lax.broadcasted_iota(jnp.int32, (tq, tq), 0)
        col = lax.broadcasted_iota(jnp.int32, (tq, tq), 1)
        causal = col < row
        after = _scan_tri("after")
        qb = q_ref[...]

        def tile(j_blk, c, acc, diag):
            r0 = pl.multiple_of(j_blk * tq, tq)
            kb = k_ref[pl.ds(r0, tq), :]
            vb = v_ref[pl.ds(r0, tq), :]
            z = lax.dot_general(qb, kb, NT, preferred_element_type=F32) * scale
            n = _neg_log_not_beta(z)
            if diag:
                n = jnp.where(causal, n, 0.0)
            later, total = _key_scan(n, after, True)
            a = jnp.exp(z - (n + later + c))
            if diag:
                a = jnp.where(causal, a, 0.0)
            acc = acc + jnp.dot(a.astype(BF16), vb, preferred_element_type=F32)
            return c + total, acc

        c, acc = tile(i_blk, jnp.zeros((tq, 1), F32), jnp.zeros((tq, HEAD_DIM), F32), True)

        def more(st):
            return jnp.logical_and(st[0] < i_blk, jnp.min(st[1]) <= SB_DEAD)

        def step(st):
            c2, acc2 = tile(i_blk - 1 - st[0], st[1], st[2], False)
            return st[0] + 1, c2, acc2

        n_done, c, acc = lax.while_loop(more, step, (jnp.int32(0), c, acc))
        o_ref[...] = acc
        c_ref[...] = jnp.broadcast_to(c, (tq, HEAD_DIM))
        nd_ref[...] = jnp.full((8, LANES), n_done.astype(F32))

    h = n_heads
    return pl.pallas_call(
        kern, name="sb_fwd", grid=(h, nq),
        in_specs=[pl.BlockSpec((tq, HEAD_DIM), lambda hh, i: (i, hh)),
                  pl.BlockSpec((t, HEAD_DIM), lambda hh, i: (0, h + hh)),
                  pl.BlockSpec((t, HEAD_DIM), lambda hh, i: (0, 2 * h + hh))],
        out_specs=[pl.BlockSpec((tq, HEAD_DIM), lambda hh, i: (i, hh)),
                   pl.BlockSpec((tq, HEAD_DIM), lambda hh, i: (i, hh)),
                   pl.BlockSpec((None, None, 8, LANES), lambda hh, i: (hh, i, 0, 0))],
        out_shape=[jax.ShapeDtypeStruct((t, h * HEAD_DIM), F32),
                   jax.ShapeDtypeStruct((t, h * HEAD_DIM), F32),
                   jax.ShapeDtypeStruct((h, nq, 8, LANES), F32)],
        compiler_params=pltpu.CompilerParams(
            dimension_semantics=("arbitrary", "arbitrary"), vmem_limit_bytes=VMEM_LIMIT),
    )(qkv, qkv, qkv)


def _sb_bwd(qkv, do, ctot, n_visited, n_heads):
    t = qkv.shape[0]
    tq = _pick(t, SB_TILE)
    nq = t // tq
    scale = HEAD_DIM ** -0.5

    def kern(q_ref, do_ref, c_ref, nd_ref, k_ref, v_ref, dq_ref, dk_ref, dv_ref, dk_acc, dv_acc):
        i_blk = pl.program_id(1)
        row = lax.broadcasted_iota(jnp.int32, (tq, tq), 0)
        col = lax.broadcasted_iota(jnp.int32, (tq, tq), 1)
        causal = col < row
        upto = _scan_tri("upto")
        before = _scan_tri("before")
        qb = q_ref[...]
        dob = do_ref[...]
        ctot_v = c_ref[...][:, 0:1]

        @pl.when(i_blk == 0)
        def _():
            dk_acc[...] = jnp.zeros_like(dk_acc)
            dv_acc[...] = jnp.zeros_like(dv_acc)

        def tile(j_blk, cl, pe, dq, diag):
            r0 = pl.multiple_of(j_blk * tq, tq)
            kb = k_ref[pl.ds(r0, tq), :]
            vb = v_ref[pl.ds(r0, tq), :]
            z = lax.dot_general(qb, kb, NT, preferred_element_type=F32) * scale
            n = _neg_log_not_beta(z)
            if diag:
                n = jnp.where(causal, n, 0.0)
            upto_s, n_total = _key_scan(n, upto, False)
            lb = z - n
            a = jnp.exp(lb - ((ctot_v - cl) - upto_s))
            if diag:
                a = jnp.where(causal, a, 0.0)
            da = lax.dot_general(dob, vb, NT, preferred_element_type=F32)
            e = da * a
            before_s, e_total = _key_scan(e, before, False)
            beta = jnp.exp(lb)
            dz = (e - beta * (e + (pe + before_s))) * scale
            if diag:
                dz = jnp.where(causal, dz, 0.0)
            dzb = dz.astype(BF16)
            dq = dq + jnp.dot(dzb, kb, preferred_element_type=F32)
            dk_acc[pl.ds(r0, tq), :] += lax.dot_general(dzb, qb, TN, preferred_element_type=F32)
            dv_acc[pl.ds(r0, tq), :] += lax.dot_general(a.astype(BF16), dob, TN,
                                                        preferred_element_type=F32)
            return cl + n_total, pe + e_total, dq

        zero = jnp.zeros((tq, 1), F32)
        first = i_blk - jnp.max(nd_ref[...]).astype(jnp.int32)
        cl, pe, dq = lax.fori_loop(first, i_blk, lambda j, s: tile(j, s[0], s[1], s[2], False),
                                   (zero, zero, jnp.zeros((tq, HEAD_DIM), F32)))
        cl, pe, dq = tile(i_blk, cl, pe, dq, True)
        dq_ref[...] = dq.astype(BF16)

        @pl.when(i_blk == nq - 1)
        def _():
            dk_ref[...] = dk_acc[...].astype(BF16)
            dv_ref[...] = dv_acc[...].astype(BF16)

    h = n_heads
    qspec = pl.BlockSpec((tq, HEAD_DIM), lambda hh, i: (i, hh))
    hspec = pl.BlockSpec((t, HEAD_DIM), lambda hh, i: (0, hh))
    out = jax.ShapeDtypeStruct((t, h * HEAD_DIM), BF16)
    return pl.pallas_call(
        kern, name="sb_bwd", grid=(h, nq),
        in_specs=[qspec, qspec, qspec,
                  pl.BlockSpec((None, None, 8, LANES), lambda hh, i: (hh, i, 0, 0)),
                  pl.BlockSpec((t, HEAD_DIM), lambda hh, i: (0, h + hh)),
                  pl.BlockSpec((t, HEAD_DIM), lambda hh, i: (0, 2 * h + hh))],
        out_specs=[qspec, hspec, hspec],
        out_shape=[out, out, out],
        scratch_shapes=[pltpu.VMEM((t, HEAD_DIM), F32), pltpu.VMEM((t, HEAD_DIM), F32)],
        compiler_params=pltpu.CompilerParams(
            dimension_semantics=("arbitrary", "arbitrary"), vmem_limit_bytes=VMEM_LIMIT),
    )(qkv, do, ctot, n_visited, qkv, qkv)


GDN_ROWS = 512
GDN_HEADS_PER_STEP = 2
TR_HEAD = 512


def _shift_rows(cur, halo, k, back):
    n = cur.shape[0]
    ext = jnp.concatenate([cur, halo], axis=0)
    return pltpu.roll(ext, k if back else n + 8 - k, 0)[:n]


def _conv_fwd(cur, halo, w):
    out = cur * w[GDN_CONV - 1:GDN_CONV, :]
    for i in range(GDN_CONV - 1):
        out = out + _shift_rows(cur, halo, GDN_CONV - 1 - i, True) * w[i:i + 1, :]
    return out


def _chunk_tri(n, upper):
    row = lax.broadcasted_iota(jnp.int32, (n, n), 0)
    col = lax.broadcasted_iota(jnp.int32, (n, n), 1)
    same = (row // GDN_CHUNK) == (col // GDN_CHUNK)
    tri = (col >= row) if upper else (col <= row)
    return jnp.logical_and(same, tri).astype(BF16)


def _lane_pick(x, lane):
    idx = lax.broadcasted_iota(jnp.int32, x.shape, 1)
    return jnp.sum(jnp.where(idx == lane, x, 0.0), axis=1, keepdims=True)


def _softplus(x):
    y = jnp.exp(-jnp.abs(x))
    u = 1.0 + y
    log1p = jnp.where(u == 1.0, y, jnp.log(u) * (y / jnp.where(u == 1.0, 1.0, u - 1.0)))
    return jnp.maximum(x, 0.0) + log1p


def _gdn_specs(t, tr, h, proj_seg0):
    def slab(seg):
        return ((tr, HEAD_DIM), lambda i, hh: (i, seg * h + hh))

    def halo_before(seg):
        return ((8, HEAD_DIM), lambda i, hh: (jnp.maximum(i * (tr // 8) - 1, 0), seg * h + hh))

    return slab, halo_before


def _gdn_pre_fwd(proj_b, proj_c, conv_w8, ab, n_heads):
    t = proj_b.shape[0]
    h = n_heads
    d = h * HEAD_DIM
    tr = _pick(t, TR_HEAD)
    slab, halo = _gdn_specs(t, tr, h, 1)
    scale = HEAD_DIM ** -0.5

    def body(idx, q_c, q_h, k_c, k_h, v_c, v_h, wq, wk, wv, gbga, ab_v):
        i, hh = idx
        live = (i > 0).astype(F32)
        outs = []
        for cur, hal, w, kind in ((q_c, q_h, wq, "q"), (k_c, k_h, wk, "k"), (v_c, v_h, wv, "v")):
            u = _conv_fwd(cur, hal * live, w)
            s = u * _sigmoid(u)
            if kind != "v":
                s = s * lax.rsqrt(jnp.sum(s * s, axis=1, keepdims=True) + L2_EPS)
            if kind == "q":
                s = s * scale
            outs.append(s)
        beta = _sigmoid(_lane_pick(gbga, hh))
        a_log = _lane_pick(ab_v[0:1, :], hh)
        dt = _lane_pick(ab_v[1:2, :], hh)
        g = -jnp.exp(a_log) * _softplus(_lane_pick(gbga, h + hh) + dt)
        g_rep = jnp.broadcast_to(g, (tr, HEAD_DIM))
        big_g = _cdot3(_chunk_tri(tr, False), g_rep)
        return outs + [jnp.broadcast_to(beta, (tr, HEAD_DIM)), big_g]

    wspec = lambda seg: ((8, HEAD_DIM), lambda i, hh: (0, seg * h + hh))
    out = ((t, d), F32, (tr, HEAD_DIM), lambda i, hh: (i, hh), "set")
    return _rowcall(
        "gdn_pre_fwd", body, (t // tr, h),
        [(proj_b,) + slab(1), (proj_b,) + halo(1), (proj_b,) + slab(2), (proj_b,) + halo(2),
         (proj_b,) + slab(3), (proj_b,) + halo(3),
         (conv_w8,) + wspec(0), (conv_w8,) + wspec(1), (conv_w8,) + wspec(2),
         (proj_c, (tr, LANES), lambda i, hh: (i, 0)), _full(ab)],
        [out] * 5)


def _gdn_consts():
    row = lax.broadcasted_iota(jnp.int32, (GDN_CHUNK, GDN_CHUNK), 0)
    col = lax.broadcasted_iota(jnp.int32, (GDN_CHUNK, GDN_CHUNK), 1)
    lane = lax.broadcasted_iota(jnp.int32, (GDN_CHUNK, HEAD_DIM), 1)
    return row > col, row >= col, (row == col).astype(F32), (lane == 0).astype(BF16)


def _gdn_local(q, k, v, be, ge, g_lanes, consts):
    lower, tril, eye, sel = consts
    kb_ = k * be
    vb_ = v * be
    e_g = jnp.exp(ge)
    kg = kb_ * e_g
    g_i = ge[:, :GDN_CHUNK]
    g_j = jnp.broadcast_to(g_lanes, (GDN_CHUNK, GDN_CHUNK))
    dec = jnp.where(tril, jnp.exp(jnp.minimum(g_i - g_j, 0.0)), 0.0)
    kk = _bdot(kb_, k, NT)
    qk = _bdot(q, k, NT)
    g_last = jnp.min(ge, axis=0, keepdims=True)
    kdec_f = jnp.exp(g_last - ge)
    return dict(kb=kb_, vb=vb_, e_g=e_g, kg=kg, dec=dec, kk=kk, qk=qk, kdec_f=kdec_f,
                k_dec=k * kdec_f, q_dec=q * e_g, gamma=jnp.exp(g_last),
                intra=jnp.where(tril, qk * dec, 0.0))


def _wy_lower_t(k, be, ge, g_lanes):
    row = lax.broadcasted_iota(jnp.int32, (GDN_CHUNK, GDN_CHUNK), 0)
    col = lax.broadcasted_iota(jnp.int32, (GDN_CHUNK, GDN_CHUNK), 1)
    g_row = ge[:, :GDN_CHUNK]
    g_col = jnp.broadcast_to(g_lanes, (GDN_CHUNK, GDN_CHUNK))
    dec_t = jnp.exp(jnp.minimum(g_col - g_row, 0.0))
    return jnp.where(col > row, _bdot(k, k * be, NT) * dec_t, 0.0)


def _unit_lower_inverse(lw_t):
    n = lw_t.shape[0]
    n_slab = GDN_CHUNK // 8
    row = lax.broadcasted_iota(jnp.int32, (n, 8, GDN_CHUNK), 1)
    col = lax.broadcasted_iota(jnp.int32, (n, 8, GDN_CHUNK), 2)
    unit = lax.broadcasted_iota(jnp.int32, (1, 1, GDN_CHUNK), 2)
    lw = [lw_t[:, 8 * g:8 * g + 8, :] for g in range(n_slab)]
    inv = [(col == row + 8 * g).astype(F32) for g in range(n_slab)]
    for i in range(1, GDN_CHUNK):
        acc = lw[0][:, :, i:i + 1] * inv[0]
        for g in range(1, (i - 1) // 8 + 1):
            acc = acc + lw[g][:, :, i:i + 1] * inv[g]
        new_row = (unit == i).astype(F32) - jnp.sum(acc, axis=1, keepdims=True)
        inv[i // 8] = jnp.where(row == i % 8, new_row, inv[i // 8])
    return jnp.concatenate(inv, axis=1)


def _gdn_fwd(q, k, v, be, ge, ge_t, n_heads):
    t = q.shape[0]
    h = n_heads
    hg = GDN_HEADS_PER_STEP
    tr = _pick(t, GDN_ROWS)
    nc = tr // GDN_CHUNK

    def kern(q_ref, k_ref, v_ref, b_ref, g_ref, gt_ref, o_ref, s_ref, tm_ref, state):
        consts = _gdn_consts()
        lower, tril, eye, sel = consts
        lanes = [pl.ds(hs * HEAD_DIM, HEAD_DIM) for hs in range(hg)]

        @pl.when(pl.program_id(1) == 0)
        def _():
            state[...] = jnp.zeros_like(state)

        lw_t = []
        for hs in range(hg):
            for ci in range(nc):
                rows = pl.ds(ci * GDN_CHUNK, GDN_CHUNK)
                lw_t.append(_wy_lower_t(k_ref[rows, lanes[hs]], b_ref[rows, lanes[hs]],
                                        g_ref[rows, lanes[hs]], gt_ref[hs, 0:1, rows]))
        t_all = _unit_lower_inverse(jnp.stack(lw_t))
        for hs in range(hg):
            for ci in range(nc):
                tm_ref[hs, pl.ds(ci * GDN_CHUNK, GDN_CHUNK), :] = t_all[hs * nc + ci]

        local = {}
        for ci in range(nc):
            rows = pl.ds(ci * GDN_CHUNK, GDN_CHUNK)
            for hs in range(hg):
                ln = lanes[hs]
                loc = _gdn_local(q_ref[rows, ln], k_ref[rows, ln], v_ref[rows, ln], b_ref[rows, ln],
                                 g_ref[rows, ln], gt_ref[hs, 0:1, rows], consts)
                t_mat = t_all[hs * nc + ci]
                local[ci, hs] = (_bdot(t_mat, loc["vb"]), _bdot(t_mat, loc["kg"]).astype(BF16),
                                 loc["q_dec"].astype(BF16), loc["intra"].astype(BF16),
                                 loc["k_dec"].astype(BF16), loc["gamma"])
        states = [state[hs] for hs in range(hg)]
        for ci in range(nc):
            rows = pl.ds(ci * GDN_CHUNK, GDN_CHUNK)
            for hs in range(hg):
                u, w, q_dec, intra, k_dec, gamma = local[ci, hs]
                s = states[hs]
                s_ref[hs, ci] = s
                v_new = u - _bdot(w, s)
                o_ref[rows, lanes[hs]] = _bdot(q_dec, s) + _bdot(intra, v_new)
                states[hs] = s * gamma + _bdot(k_dec, v_new, TN)
        for hs in range(hg):
            state[hs] = states[hs]

    slab = pl.BlockSpec((tr, hg * HEAD_DIM), lambda hp, j: (j, hp))
    return pl.pallas_call(
        kern, name="gdn_fwd", grid=(h // hg, t // tr),
        in_specs=[slab] * 5 + [pl.BlockSpec((hg, 8, tr), lambda hp, j: (hp, 0, j))],
        out_specs=[slab,
                   pl.BlockSpec((hg, nc, HEAD_DIM, HEAD_DIM), lambda hp, j: (hp, j, 0, 0)),
                   pl.BlockSpec((hg, tr, GDN_CHUNK), lambda hp, j: (hp, j, 0))],
        out_shape=[jax.ShapeDtypeStruct((t, h * HEAD_DIM), F32),
                   jax.ShapeDtypeStruct((h, t // GDN_CHUNK, HEAD_DIM, HEAD_DIM), F32),
                   jax.ShapeDtypeStruct((h, t, GDN_CHUNK), F32)],
        scratch_shapes=[pltpu.VMEM((hg, HEAD_DIM, HEAD_DIM), F32)],
        compiler_params=pltpu.CompilerParams(
            dimension_semantics=("arbitrary", "arbitrary"), vmem_limit_bytes=VMEM_LIMIT),
    )(q, k, v, be, ge, ge_t)


def _gdn_bwd(q, k, v, be, ge, ge_t, do, states, tms, n_heads):
    t = q.shape[0]
    h = n_heads
    hg = GDN_HEADS_PER_STEP
    tr = _pick(t, GDN_ROWS)
    nc = tr // GDN_CHUNK
    nj = t // tr

    def kern(q_ref, k_ref, v_ref, b_ref, g_ref, gt_ref, do_ref, s_ref, tm_ref,
             dq_ref, dk_ref, dv_ref, db_ref, dg_ref, dstate):
        consts = _gdn_consts()
        lower, tril, eye, sel = consts
        last_row = lax.broadcasted_iota(jnp.int32, (GDN_CHUNK, HEAD_DIM), 0) == GDN_CHUNK - 1

        @pl.when(pl.program_id(1) == 0)
        def _():
            dstate[...] = jnp.zeros_like(dstate)

        def lsum(x):
            return jnp.sum(x, axis=1, keepdims=True)

        def before(ci, hs):
            rows = pl.ds(ci * GDN_CHUNK, GDN_CHUNK)
            ln = pl.ds(hs * HEAD_DIM, HEAD_DIM)
            qv, kv, vv, bev = q_ref[rows, ln], k_ref[rows, ln], v_ref[rows, ln], b_ref[rows, ln]
            loc = _gdn_local(qv, kv, vv, bev, g_ref[rows, ln], gt_ref[hs, 0:1, rows], consts)
            t_mat = tm_ref[hs, rows, :]
            s = s_ref[hs, ci]
            d_o = do_ref[rows, ln]
            w = _bdot(t_mat, loc["kg"])
            v_new = _bdot(t_mat, loc["vb"]) - _bdot(w, s)
            return dict(loc=loc, qv=qv, kv=kv, vv=vv, bev=bev, t_mat=t_mat, s=s, w=w, v_new=v_new,
                        dv_new0=_bdot(loc["intra"], d_o, TN), ds0=_bdot(loc["q_dec"], d_o, TN),
                        d_intra=jnp.where(tril, _bdot(d_o, v_new, NT), 0.0),
                        dq_dec=_bdot(d_o, s, NT))

        def recur(c, d_s):
            loc = c["loc"]
            dv_new = c["dv_new0"] + _bdot(loc["k_dec"], d_s)
            c.update(dv_new=dv_new, dk_dec=_bdot(c["v_new"], d_s, NT),
                     dgamma=jnp.sum(lsum(d_s * c["s"]), axis=0, keepdims=True))
            return c["ds0"] + loc["gamma"] * d_s - _bdot(c["w"], dv_new, TN)

        def after(c):
            loc, qv, kv, vv, bev, t_mat, s = (c[n] for n in ("loc", "qv", "kv", "vv", "bev", "t_mat",
                                                             "s"))
            dv_new, dk_dec, dgamma, d_intra, dq_dec = (c[n] for n in ("dv_new", "dk_dec", "dgamma",
                                                                      "d_intra", "dq_dec"))
            dw = -_bdot(dv_new, s, NT)
            dtm = _bdot(dv_new, loc["vb"], NT) + _bdot(dw, loc["kg"], NT)
            dvb = _bdot(t_mat, dv_new, TN)
            dkg = _bdot(t_mat, dw, TN)
            dlw = jnp.where(lower, -_dot3(t_mat, _dot3(dtm, t_mat, NT), TN), 0.0)
            dkk = dlw * loc["dec"]
            dqk = d_intra * loc["dec"]
            ddec = dlw * loc["kk"] + d_intra * loc["qk"]
            dkb = _bdot(dkk, kv) + dkg * loc["e_g"]
            dk = (_bdot(dkk, loc["kb"], TN) + _bdot(dqk, qv, TN) + dk_dec * loc["kdec_f"]
                  + dkb * bev)
            dq = _bdot(dqk, kv) + dq_dec * loc["e_g"]
            dgd = ddec * loc["dec"]
            r_kdec = lsum(dk_dec * loc["k_dec"])
            col_sums = lsum(_dot2(dgd, sel, TN))
            d_big_g = (lsum(dgd) - col_sums
                       + lsum(dq_dec * loc["q_dec"]) - r_kdec + lsum(dkg * loc["kg"]))
            d_last = jnp.sum(r_kdec, axis=0, keepdims=True) + dgamma * loc["gamma"][:, 0:1]
            d_big_g = jnp.broadcast_to(d_big_g, (GDN_CHUNK, HEAD_DIM)) + jnp.where(last_row, d_last,
                                                                                  0.0)
            d_beta = jnp.broadcast_to(lsum(dkb * kv) + lsum(dvb * vv), (GDN_CHUNK, HEAD_DIM))
            return dq, dk, dvb * bev, d_beta, d_big_g

        work = {(ci, hs): before(ci, hs) for ci in range(nc) for hs in range(hg)}
        d_states = [dstate[hs] for hs in range(hg)]
        for ci in range(nc - 1, -1, -1):
            for hs in range(hg):
                d_states[hs] = recur(work[ci, hs], d_states[hs])
        for hs in range(hg):
            dstate[hs] = d_states[hs]
        for ci in range(nc):
            rows = pl.ds(ci * GDN_CHUNK, GDN_CHUNK)
            for hs in range(hg):
                ln = pl.ds(hs * HEAD_DIM, HEAD_DIM)
                for ref, val in zip((dq_ref, dk_ref, dv_ref, db_ref, dg_ref), after(work[ci, hs])):
                    ref[rows, ln] = val

    slab = pl.BlockSpec((tr, hg * HEAD_DIM), lambda hp, j: (nj - 1 - j, hp))
    out = jax.ShapeDtypeStruct((t, h * HEAD_DIM), F32)
    return pl.pallas_call(
        kern, name="gdn_bwd", grid=(h // hg, nj),
        in_specs=[slab] * 5 + [pl.BlockSpec((hg, 8, tr), lambda hp, j: (hp, 0, nj - 1 - j)), slab] + [
            pl.BlockSpec((hg, nc, HEAD_DIM, HEAD_DIM), lambda hp, j: (hp, nj - 1 - j, 0, 0)),
            pl.BlockSpec((hg, tr, GDN_CHUNK), lambda hp, j: (hp, nj - 1 - j, 0))],
        out_specs=[slab] * 5,
        out_shape=[out] * 5,
        scratch_shapes=[pltpu.VMEM((hg, HEAD_DIM, HEAD_DIM), F32)],
        compiler_params=pltpu.CompilerParams(
            dimension_semantics=("arbitrary", "arbitrary"), vmem_limit_bytes=VMEM_LIMIT),
    )(q, k, v, be, ge, ge_t, do, states, tms)


def _gdn_pre_bwd(proj_b, proj_c, conv_w8, ab, dq, dk, dv, dbe, dge, n_heads):
    t = proj_b.shape[0]
    h = n_heads
    d = h * HEAD_DIM
    tr = _pick(t, TR_HEAD)
    slab, halo = _gdn_specs(t, tr, h, 1)
    scale = HEAD_DIM ** -0.5

    def body(idx, q_c, q_h, k_c, k_h, v_c, v_h, wq, wk, wv, gbga, ab_v, dq_v, dk_v, dv_v, dbe_v,
             dge_v):
        i, hh = idx
        live = (i > 0).astype(F32)
        outs = []
        for cur, hal, w, dy, kind in ((q_c, q_h, wq, dq_v, "q"), (k_c, k_h, wk, dk_v, "k"),
                                      (v_c, v_h, wv, dv_v, "v")):
            u = _conv_fwd(cur, hal * live, w)
            sg = _sigmoid(u)
            if kind == "v":
                ds = dy
            else:
                s = u * sg
                r = lax.rsqrt(jnp.sum(s * s, axis=1, keepdims=True) + L2_EPS)
                y = s * r
                if kind == "q":
                    dy = dy * scale
                ds = r * (dy - y * jnp.sum(dy * y, axis=1, keepdims=True))
            outs.append(ds * (sg * (1.0 + u * (1.0 - sg))))
        lane = lax.broadcasted_iota(jnp.int32, (tr, LANES), 1)
        lane1 = lax.broadcasted_iota(jnp.int32, (1, LANES), 1)
        beta = _sigmoid(_lane_pick(gbga, hh))
        a_neg = -jnp.exp(_lane_pick(ab_v[0:1, :], hh))
        xg = _lane_pick(gbga, h + hh) + _lane_pick(ab_v[1:2, :], hh)
        g = a_neg * _softplus(xg)
        dgb = dbe_v * (beta * (1.0 - beta))
        dg = _cdot3(_chunk_tri(tr, True), dge_v)
        dga = dg * (a_neg * _sigmoid(xg))
        d_gates = jnp.where(lane == hh, dgb, 0.0) + jnp.where(lane == h + hh, dga, 0.0)
        d_ab = (jnp.where(lane1 == hh, jnp.sum(dg * g, axis=0, keepdims=True), 0.0)
                + jnp.where(lane1 == h + hh, jnp.sum(dga, axis=0, keepdims=True), 0.0))
        return outs + [d_gates, d_ab]

    wspec = lambda seg: ((8, HEAD_DIM), lambda i, hh: (0, seg * h + hh))
    hs = ((tr, HEAD_DIM), lambda i, hh: (i, hh))
    out = ((t, d), F32) + hs + ("set",)
    return _rowcall(
        "gdn_pre_bwd", body, (t // tr, h),
        [(proj_b,) + slab(1), (proj_b,) + halo(1), (proj_b,) + slab(2), (proj_b,) + halo(2),
         (proj_b,) + slab(3), (proj_b,) + halo(3),
         (conv_w8,) + wspec(0), (conv_w8,) + wspec(1), (conv_w8,) + wspec(2),
         (proj_c, (tr, LANES), lambda i, hh: (i, 0)), _full(ab),
         (dq,) + hs, (dk,) + hs, (dv,) + hs, (dbe,) + hs, (dge,) + hs],
        [out, out, out,
         ((t, LANES), F32, (tr, LANES), lambda i, hh: (i, 0), "acc_inner"),
         ((1, LANES), F32, (1, LANES), lambda i, hh: (0, 0), "acc_all")])


def _conv_bwd(proj_b, conv_w8, du_q, du_k, du_v, n_heads):
    t = proj_b.shape[0]
    h = n_heads
    d = h * HEAD_DIM
    tr = _pick(t, TR_HEAD)
    ni = t // tr

    def body(idx, q_c, q_h, k_c, k_h, v_c, v_h, wq, wk, wv, uq, uq_n, uk, uk_n, uv, uv_n):
        hh, i = idx
        live_b = (i > 0).astype(F32)
        live_a = (i < ni - 1).astype(F32)
        d_ins, d_ws = [], []
        for cur, hal, w, du, du_n in ((q_c, q_h, wq, uq, uq_n), (k_c, k_h, wk, uk, uk_n),
                                      (v_c, v_h, wv, uv, uv_n)):
            hal = hal * live_b
            du_n = du_n * live_a
            d_in = du * w[GDN_CONV - 1:GDN_CONV, :]
            rows = [jnp.sum(du * cur, axis=0, keepdims=True)]
            for i_tap in range(GDN_CONV - 2, -1, -1):
                kshift = GDN_CONV - 1 - i_tap
                d_in = d_in + _shift_rows(du, du_n, kshift, False) * w[i_tap:i_tap + 1, :]
                rows.insert(0, jnp.sum(du * _shift_rows(cur, hal, kshift, True), axis=0,
                                       keepdims=True))
            d_ins.append(d_in)
            tap = lax.broadcasted_iota(jnp.int32, (8, HEAD_DIM), 0)
            d_w = jnp.zeros((8, HEAD_DIM), F32)
            for i_tap in range(GDN_CONV):
                d_w = d_w + jnp.where(tap == i_tap, rows[i_tap], 0.0)
            d_ws.append(d_w)
        return d_ins + d_ws

    def slab(seg):
        return ((tr, HEAD_DIM), lambda hh, i: (i, seg * h + hh))

    def halo_b(seg):
        return ((8, HEAD_DIM), lambda hh, i: (jnp.maximum(i * (tr // 8) - 1, 0), seg * h + hh))

    hs = ((tr, HEAD_DIM), lambda hh, i: (i, hh))
    halo_a = ((8, HEAD_DIM), lambda hh, i: (jnp.minimum((i + 1) * (tr // 8), t // 8 - 1), hh))
    wspec = lambda seg: ((8, HEAD_DIM), lambda hh, i: (0, seg * h + hh))
    wout = ((8, d), F32, (8, HEAD_DIM), lambda hh, i: (0, hh), "acc_inner")
    out = ((t, d), BF16) + hs + ("set",)
    res = _rowcall(
        "conv_bwd", body, (h, ni),
        [(proj_b,) + slab(1), (proj_b,) + halo_b(1), (proj_b,) + slab(2), (proj_b,) + halo_b(2),
         (proj_b,) + slab(3), (proj_b,) + halo_b(3),
         (conv_w8,) + wspec(0), (conv_w8,) + wspec(1), (conv_w8,) + wspec(2),
         (du_q,) + hs, (du_q,) + halo_a, (du_k,) + hs, (du_k,) + halo_a, (du_v,) + hs,
         (du_v,) + halo_a],
        [out, out, out, wout, wout, wout])
    return res


def _gdn_post_fwd(o, proj_b, gnw, n_heads):
    t, d = o.shape
    h = n_heads
    tr = _pick(t, TR_HEAD)

    def body(idx, o_v, z, w):
        rstd = lax.rsqrt(jnp.mean(o_v * o_v, axis=1, keepdims=True) + NORM_EPS)
        return ((o_v * rstd) * w * (z * _sigmoid(z)),)

    hs = ((tr, HEAD_DIM), lambda i, hh: (i, hh))
    return _rowcall("gdn_post_fwd", body, (t // tr, h),
                    [(o,) + hs, (proj_b, (tr, HEAD_DIM), lambda i, hh: (i, 4 * h + hh)), _full(gnw)],
                    [((t, d), BF16) + hs + ("set",)])[0]


def _gdn_post_bwd(do_gdn, o, proj_b, gnw, n_heads):
    t, d = o.shape
    h = n_heads
    tr = _pick(t, TR_HEAD)

    def body(idx, dog, o_v, z, w):
        rstd = lax.rsqrt(jnp.mean(o_v * o_v, axis=1, keepdims=True) + NORM_EPS)
        n = o_v * rstd
        sg = _sigmoid(z)
        don = dog * (z * sg)
        dz = dog * (n * w) * (sg * (1.0 + z * (1.0 - sg)))
        dn = don * w
        d_o = rstd * (dn - n * jnp.mean(dn * n, axis=1, keepdims=True))
        return d_o, dz, jnp.sum(don * n, axis=0, keepdims=True)

    hs = ((tr, HEAD_DIM), lambda i, hh: (i, hh))
    return _rowcall("gdn_post_bwd", body, (t // tr, h),
                    [(do_gdn,) + hs, (o,) + hs,
                     (proj_b, (tr, HEAD_DIM), lambda i, hh: (i, 4 * h + hh)), _full(gnw)],
                    [((t, d), F32) + hs + ("set",), ((t, d), BF16) + hs + ("set",),
                     ((1, HEAD_DIM), F32, (1, HEAD_DIM), lambda i, hh: (0, 0), "acc_all")])


def _seg(arr, tr, d, seg):
    return (arr, (tr, d), lambda i: (i, seg))


def _sb_post_fwd(o_raw, proj_b):
    t, d = o_raw.shape
    tr = _pick(t, TR_WIDE)

    def body(idx, o_v, z):
        return (o_v * (z * _sigmoid(z)),)

    return _rowcall("sb_post_fwd", body, (t // tr,), [_seg(o_raw, tr, d, 0), _seg(proj_b, tr, d, 0)],
                    [((t, d), BF16, (tr, d), lambda i: (i, 0), "set")])[0]


def _sb_post_bwd(do_sb, o_raw, proj_b):
    t, d = o_raw.shape
    tr = _pick(t, TR_WIDE)

    def body(idx, dos, o_v, z):
        sg = _sigmoid(z)
        return dos * (z * sg), dos * o_v * (sg * (1.0 + z * (1.0 - sg)))

    out = ((t, d), BF16, (tr, d), lambda i: (i, 0), "set")
    return _rowcall("sb_post_bwd", body, (t // tr,),
                    [_seg(do_sb, tr, d, 0), _seg(o_raw, tr, d, 0), _seg(proj_b, tr, d, 0)],
                    [out, out])


def _merge_fwd(proj_b, p_sb, p_gdn):
    t, d = p_sb.shape
    tr = _pick(t, TR_WIDE)

    def body(idx, m_sb, m_gdn, ps, pg):
        return (_sigmoid(m_sb) * ps + _sigmoid(m_gdn) * pg,)

    return _rowcall("merge_fwd", body, (t // tr,),
                    [_seg(proj_b, tr, d, 5), _seg(proj_b, tr, d, 6), _seg(p_sb, tr, d, 0),
                     _seg(p_gdn, tr, d, 0)],
                    [((t, d), BF16, (tr, d), lambda i: (i, 0), "set")])[0]


def _merge_bwd(dy, proj_b, p_sb, p_gdn):
    t, d = p_sb.shape
    tr = _pick(t, TR_WIDE)

    def body(idx, dy_v, m_sb, m_gdn, ps, pg):
        s1 = _sigmoid(m_sb)
        s2 = _sigmoid(m_gdn)
        return s1 * dy_v, s2 * dy_v, dy_v * ps * (s1 * (1.0 - s1)), dy_v * pg * (s2 * (1.0 - s2))

    out = ((t, d), BF16, (tr, d), lambda i: (i, 0), "set")
    return _rowcall("merge_bwd", body, (t // tr,),
                    [_seg(dy, tr, d, 0), _seg(proj_b, tr, d, 5), _seg(proj_b, tr, d, 6),
                     _seg(p_sb, tr, d, 0), _seg(p_gdn, tr, d, 0)],
                    [out] * 4)


def _tail(x, r, target, gate, final_w):
    t, d = x.shape
    tr = _pick(t, TR_WIDE)

    def body(idx, x_v, r_v, tg, gt, fw):
        x2 = x_v + gt * r_v
        rstd = lax.rsqrt(jnp.mean(x2 * x2, axis=1, keepdims=True) + NORM_EPS)
        n = x2 * rstd
        diff = n * fw - tg
        loss = 0.5 * jnp.sum(jnp.mean(diff * diff, axis=1, keepdims=True), axis=0, keepdims=True)
        dout = diff * (1.0 / d)
        dn = dout * fw
        dx2 = rstd * (dn - n * jnp.mean(dn * n, axis=1, keepdims=True))
        return (dx2, gt * dx2, jnp.sum(dout * n, axis=0, keepdims=True),
                jnp.sum(dx2 * r_v, axis=0, keepdims=True), jnp.broadcast_to(loss, (1, LANES)))

    rb = ((tr, d), lambda i: (i, 0))
    vec = ((1, d), F32, (1, d), lambda i: (0, 0), "acc_all")
    return _rowcall("tail", body, (t // tr,),
                    [(x,) + rb, (r,) + rb, (target,) + rb, _full(gate), _full(final_w)],
                    [((t, d), F32) + rb + ("set",), ((t, d), BF16) + rb + ("set",), vec, vec,
                     ((1, LANES), F32, (1, LANES), lambda i: (0, 0), "acc_all")])


def _pad_to(a, rows, cols):
    return jnp.pad(a, ((0, rows - a.shape[0]), (0, cols - a.shape[1])))


def kernel(x, c, w_ada, b_ada, norm_w, w_in, gdn_conv_w, gdn_a_log, gdn_dt_bias, gdn_norm_w, w_proj_sb, w_proj_gdn, w_out, final_norm_w, loss_target, m_w_ada, m_b_ada, m_norm_w, m_w_in, m_gdn_conv_w, m_gdn_a_log, m_gdn_dt_bias, m_gdn_norm_w, m_w_proj_sb, m_w_proj_gdn, m_w_out, m_final_norm_w, v_w_ada, v_b_ada, v_norm_w, v_w_in, v_gdn_conv_w, v_gdn_a_log, v_gdn_dt_bias, v_gdn_norm_w, v_w_proj_sb, v_w_proj_gdn, v_w_out, v_final_norm_w):
    t, d = x.shape[1], x.shape[2]
    h = d // HEAD_DIM
    me = 4 * lax.axis_index("x") + 2 * lax.axis_index("y") + lax.axis_index("c")
    x2d = x[0]
    tgt = loss_target[0]
    ada_cols = w_ada.shape[2]
    in_cols = w_in.shape[2]
    rows_p = w_out.shape[1]

    w_in_all = _gather_two_level("gather_w_in", w_in[0].astype(BF16))
    w_in_full = jnp.transpose(w_in_all, (1, 0, 2)).reshape(d, N_DEV * in_cols)
    w_main = jnp.concatenate([w_in_full[:, :8 * d], w_in_full[:, 8 * d + 2 * h:]], axis=1)
    w_g = _pad_to(w_in_full[:, 8 * d:8 * d + 2 * h], d, LANES)
    w_main_t = w_main.T
    w_g_t = w_g.T
    w_sq = jnp.stack([w_proj_sb[0], w_proj_gdn[0], w_out[0]]).astype(BF16)
    conv_all = _exchange("gather_conv", _pad_to(gdn_conv_w[0], 8, gdn_conv_w.shape[2]), True)
    conv_w8 = jnp.transpose(conv_all, (1, 0, 2)).reshape(8, 3 * d)
    c_all = _exchange("gather_c", _pad_to(c, 8, d), True)[:, 0, :]

    sc_all = c_all * _sigmoid(c_all)
    mod_part = _matmul("ada_fwd", _pad_to(sc_all, 16, d).astype(BF16), w_ada[0].astype(BF16), F32)
    mod_part = mod_part[:N_DEV] + lax.dynamic_slice(b_ada, (0, me * ada_cols), (1, ada_cols))
    mod_rows = _exchange("a2a_mod", _pad_to(mod_part, 8, ada_cols).reshape(N_DEV, 1, ada_cols)
                         * jnp.ones((1, 8, 1), F32), False)
    mod = mod_rows[:, 0, :].reshape(1, 3 * d)
    shift, scale, gate = mod[:, :d], mod[:, d:2 * d], mod[:, 2 * d:]

    hmod = _norm_mod_fwd(x2d, shift, scale, norm_w)
    proj_a = _matmul("in_proj_a", hmod, w_main, BF16, n_cols=3 * d, col0=0)
    proj_b, w_sq_all = _matmul("in_proj_b", hmod, w_main, F32, n_cols=7 * d, col0=3 * d,
                               xchg=w_sq, gather=True)
    w_sq_full = jnp.transpose(w_sq_all, (1, 0, 2, 3)).reshape(3, d, d)
    wp_sb, wp_gdn, wo = w_sq_full[0], w_sq_full[1], w_sq_full[2]
    proj_c = _matmul("in_proj_c", hmod, w_g, F32)
    o_sb_raw, ctot, sb_visited = _sb_fwd(proj_a, h)
    o_sb = _sb_post_fwd(o_sb_raw, proj_b)
    ab = _pad_to(jnp.concatenate([gdn_a_log, gdn_dt_bias], axis=0), 8, LANES)
    gq, gk, gv, g_beta, g_cum = _gdn_pre_fwd(proj_b, proj_c, conv_w8, ab, h)
    g_cum_t = jnp.broadcast_to(g_cum[:, ::HEAD_DIM].T[:, None, :], (h, 8, t))
    o_gdn_raw, states, tms = _gdn_fwd(gq, gk, gv, g_beta, g_cum, g_cum_t, h)
    o_gdn = _gdn_post_fwd(o_gdn_raw, proj_b, gdn_norm_w, h)
    p_sb = _matmul("proj_sb", o_sb, wp_sb, F32)
    p_gdn = _matmul("proj_gdn", o_gdn, wp_gdn, F32)
    y = _merge_fwd(proj_b, p_sb, p_gdn)
    r = _matmul("out_proj", y, wo, F32)
    dx2, dr, d_final_w, d_gate, loss_part = _tail(x2d, r, tgt, gate, final_norm_w.reshape(1, d))

    dy = _matmul("d_out_proj", dr, wo.T, F32)
    dw_out = _matmul("dw_out", y.T, dr, BF16)
    dp_sb, dp_gdn, dm_sb, dm_gdn = _merge_bwd(dy, proj_b, p_sb, p_gdn)
    do_sb = _matmul("d_proj_sb", dp_sb, wp_sb.T, F32)
    dw_p_sb = _matmul("dw_proj_sb", o_sb.T, dp_sb, BF16)
    do_gdn = _matmul("d_proj_gdn", dp_gdn, wp_gdn.T, F32)
    dw_p_gdn = _matmul("dw_proj_gdn", o_gdn.T, dp_gdn, BF16)
    do_sb_raw, d_sbz = _sb_post_bwd(do_sb, o_sb_raw, proj_b)
    d_sbq, d_sbk, d_sbv = _sb_bwd(proj_a, do_sb_raw, ctot, sb_visited, h)
    d_o_gdn, d_gz, d_gnw = _gdn_post_bwd(do_gdn, o_gdn_raw, proj_b, gdn_norm_w, h)
    dgq, dgk, dgv, dgbe, dgcum = _gdn_bwd(gq, gk, gv, g_beta, g_cum, g_cum_t, d_o_gdn, states, tms, h)
    du_q, du_k, du_v, d_gates, d_ab = _gdn_pre_bwd(proj_b, proj_c, conv_w8, ab, dgq, dgk, dgv,
                                                  dgbe, dgcum, h)
    d_gq, d_gk, d_gv, dcw_q, dcw_k, dcw_v = _conv_bwd(proj_b, conv_w8, du_q, du_k, du_v, h)
    dproj = jnp.concatenate([d_sbq, d_sbk, d_sbv, d_sbz, d_gq, d_gk, d_gv, d_gz, dm_sb, dm_gdn],
                            axis=1)
    d_gates_b = d_gates.astype(BF16)
    hmod_t = hmod.T
    dw_sq = jnp.stack([dw_p_sb, dw_p_gdn, dw_out]).reshape(3, N_DEV, rows_p, d)
    dw_main, dw_sq_parts = _matmul("dw_in", hmod_t, dproj, BF16,
                                   xchg=jnp.transpose(dw_sq, (1, 0, 2, 3)))
    dw_g = _matmul("dw_in_g", hmod_t, d_gates_b, BF16)
    dw_in_full = jnp.concatenate([dw_main[:, :8 * d], dw_g[:, :2 * h], dw_main[:, 8 * d:]], axis=1)
    dh_a, dw_in_parts = _matmul(
        "d_in_proj", dproj, w_main_t, F32,
        xchg=jnp.transpose(dw_in_full.reshape(d, N_DEV, in_cols), (1, 0, 2)))
    dh_b = _matmul("d_in_proj_g", d_gates_b, w_g_t, F32)
    grad_x, d_shift, d_scale, d_norm_w = _norm_mod_bwd(dh_a, dh_b, x2d, dx2, scale, norm_w)

    dmod = jnp.concatenate([d_shift, d_scale, d_gate], axis=1)
    small = jnp.concatenate([dmod, d_norm_w, d_final_w, d_ab[:, :h], d_ab[:, h:2 * h], d_gnw,
                             loss_part], axis=1)
    n_small = small.shape[1]
    small_all = _exchange("gather_small", _pad_to(small, 8, n_small), True)[:, 0:1, :]
    small_w = jnp.concatenate([b_ada, norm_w, final_norm_w.reshape(1, d), gdn_a_log, gdn_dt_bias,
                               gdn_norm_w, jnp.zeros((1, LANES), F32)], axis=1)
    small_m = jnp.concatenate([m_b_ada, m_norm_w, m_final_norm_w.reshape(1, d), m_gdn_a_log,
                               m_gdn_dt_bias, m_gdn_norm_w, jnp.zeros((1, LANES), F32)], axis=1)
    small_v = jnp.concatenate([v_b_ada, v_norm_w, v_final_norm_w.reshape(1, d), v_gdn_a_log,
                               v_gdn_dt_bias, v_gdn_norm_w, jnp.ones((1, LANES), F32)], axis=1)
    s_g, s_d, s_m, s_v = _adamw("adamw_small", small_all, small_w, small_m, small_v)
    cuts = [3 * d, 4 * d, 5 * d, 5 * d + h, 5 * d + 2 * h, 5 * d + 2 * h + HEAD_DIM]

    def split_small(a):
        b, nw, fw, al, dtb, gn, _ = jnp.split(a, cuts, axis=1)
        return b, nw, fw.reshape(d), al, dtb, gn

    loss = s_g[0, cuts[-1]]

    dmod_all = small_all[:, 0, :3 * d]
    dmod_mine = lax.dynamic_slice(dmod_all, (0, me * ada_cols), (N_DEV, ada_cols))
    dw_ada = _matmul("dw_ada", _pad_to(sc_all.T, d, LANES).astype(BF16),
                     _pad_to(dmod_mine, LANES, ada_cols).astype(BF16), F32)
    ada = _adamw("adamw_ada", dw_ada[None], w_ada[0], m_w_ada[0], v_w_ada[0])

    win = _adamw("adamw_w_in", dw_in_parts, w_in[0], m_w_in[0], v_w_in[0])
    sq = _adamw("adamw_sq", dw_sq_parts.reshape(N_DEV, 3 * rows_p, d),
                jnp.concatenate([w_proj_sb[0], w_proj_gdn[0], w_out[0]], axis=0),
                jnp.concatenate([m_w_proj_sb[0], m_w_proj_gdn[0], m_w_out[0]], axis=0),
                jnp.concatenate([v_w_proj_sb[0], v_w_proj_gdn[0], v_w_out[0]], axis=0))
    dcw = jnp.concatenate([dcw_q, dcw_k, dcw_v], axis=1)
    cw_cols = gdn_conv_w.shape[2]
    dcw_parts = _exchange("a2a_dconv",
                          jnp.transpose(dcw.reshape(8, N_DEV, cw_cols), (1, 0, 2)), False)
    cw = _adamw("adamw_conv", dcw_parts, _pad_to(gdn_conv_w[0], 8, cw_cols),
                _pad_to(m_gdn_conv_w[0], 8, cw_cols),
                jnp.pad(v_gdn_conv_w[0], ((0, 8 - GDN_CONV), (0, 0)), constant_values=1.0))

    outs = [loss, grad_x[None]]
    for k_out in range(4):
        b, nw, fw, al, dtb, gn = split_small((s_g, s_d, s_m, s_v)[k_out])
        sq3 = sq[k_out].reshape(3, 1, rows_p, d)
        outs += [ada[k_out][None], b, nw, win[k_out][None], cw[k_out][None, :GDN_CONV], al, dtb, gn,
                 sq3[0], sq3[1], sq3[2], fw]
    return tuple(outs)
```

```python
import jax
import jax.numpy as jnp
from jax import lax
from jax.experimental import pallas as pl
from jax.experimental.pallas import tpu as pltpu

F32 = jnp.float32
BF16 = jnp.bfloat16
N_DEV = 8
HEAD_DIM = 128
LANES = 128
GDN_CHUNK = 64
GDN_CONV = 4
NORM_EPS = 1e-6
L2_EPS = 1e-6
ADAM_LR = 0.001
ADAM_B1 = 0.9
ADAM_B2 = 0.999
ADAM_EPS = 1e-08
ADAM_WD = 0.01
ADAM_STEP = 10
VMEM_LIMIT = 56 * 1024 * 1024
MESH = pl.DeviceIdType.MESH
NT = (((1,), (1,)), ((), ()))
TN = (((0,), (0,)), ((), ()))


def _pick(n, pref):
    t = min(pref, n)
    while n % t:
        t //= 2
    return t


def _sigmoid(x):
    return 1.0 / (1.0 + jnp.exp(-x))


def _bdot(a, b, dims=None):
    a = a.astype(BF16)
    b = b.astype(BF16)
    if dims is None:
        return jnp.dot(a, b, preferred_element_type=F32)
    return lax.dot_general(a, b, dims, preferred_element_type=F32)


def _split2(x):
    hi = x.astype(BF16)
    lo = (x - hi.astype(F32)).astype(BF16)
    return hi, lo


def _split3(x):
    p1 = x.astype(BF16)
    r = x - p1.astype(F32)
    p2 = r.astype(BF16)
    p3 = (r - p2.astype(F32)).astype(BF16)
    return p1, p2, p3


def _dot2(x, c, dims=None):
    hi, lo = _split2(x)
    return _bdot(hi, c, dims) + _bdot(lo, c, dims)


def _cdot3(c, x):
    p1, p2, p3 = _split3(x)
    return _bdot(c, p1) + _bdot(c, p2) + _bdot(c, p3)


def _dot3(a, b, dims=None):
    ah, al = _split2(a)
    bh, bl = _split2(b)
    return _bdot(ah, bh, dims) + (_bdot(ah, bl, dims) + _bdot(al, bh, dims))


def _dot6(a, b, dims=None):
    a1, a2, a3 = _split3(a)
    b1, b2, b3 = _split3(b)
    small = (_bdot(a1, b3, dims) + _bdot(a3, b1, dims)) + _bdot(a2, b2, dims)
    return _bdot(a1, b1, dims) + ((_bdot(a1, b2, dims) + _bdot(a2, b1, dims)) + small)


def _full(a):
    nd = a.ndim
    return (a, a.shape, lambda *idx: (0,) * nd)


def _rowcall(name, body, grid, ins, outs):
    n_in = len(ins)
    modes = [o[4] for o in outs]
    n_ax = len(grid)

    def kern(*refs):
        idx = tuple(pl.program_id(a) for a in range(n_ax))
        vals = body(idx, *[r[...] for r in refs[:n_in]])
        first_all = idx[0] == 0
        for a in range(1, n_ax):
            first_all = jnp.logical_and(first_all, idx[a] == 0)
        first_inner = idx[-1] == 0
        for r, v, mode in zip(refs[n_in:], vals, modes):
            v = v.astype(r.dtype)
            if mode == "set":
                r[...] = v
            else:
                first = first_all if mode == "acc_all" else first_inner

                @pl.when(first)
                def _(r=r, v=v):
                    r[...] = v

                @pl.when(jnp.logical_not(first))
                def _(r=r, v=v):
                    r[...] += v

    return pl.pallas_call(
        kern, name=name, grid=grid,
        in_specs=[pl.BlockSpec(b, m) for (_, b, m) in ins],
        out_specs=[pl.BlockSpec(o[2], o[3]) for o in outs],
        out_shape=[jax.ShapeDtypeStruct(o[0], o[1]) for o in outs],
        compiler_params=pltpu.CompilerParams(
            dimension_semantics=("arbitrary",) * n_ax, vmem_limit_bytes=VMEM_LIMIT),
    )(*[a for (a, _, _) in ins])


BIG_K_TILE = 4096

EXCHANGE_SEMS = [pltpu.SemaphoreType.DMA((N_DEV - 1,)), pltpu.SemaphoreType.DMA((N_DEV - 1,)),
                 pltpu.SemaphoreType.DMA]


def _exchange_shape(x, gather):
    return jax.ShapeDtypeStruct((N_DEV,) + tuple(x.shape if gather else x.shape[1:]), x.dtype)


def _exchange_copies(x_ref, out_ref, send_sems, recv_sems, local_sem, gather, start, wait):
    xi, yi, ci = lax.axis_index("x"), lax.axis_index("y"), lax.axis_index("c")
    me = 4 * xi + 2 * yi + ci
    mine = pltpu.make_async_copy(x_ref if gather else x_ref.at[me], out_ref.at[me], local_sem)
    if start:
        mine.start()
    for k in range(1, N_DEV):
        kx, ky, kc = (k >> 2) & 1, (k >> 1) & 1, k & 1
        pid = me ^ k
        src = x_ref if gather else x_ref.at[pid]
        if start:
            pltpu.make_async_remote_copy(
                src_ref=src, dst_ref=out_ref.at[me], send_sem=send_sems.at[k - 1],
                recv_sem=recv_sems.at[k - 1], device_id=(xi ^ kx, yi ^ ky, ci ^ kc),
                device_id_type=MESH).start()
        if wait:
            pltpu.make_async_remote_copy(
                src_ref=src, dst_ref=out_ref.at[pid], send_sem=send_sems.at[k - 1],
                recv_sem=recv_sems.at[k - 1], device_id=(xi, yi, ci), device_id_type=MESH).wait()
    if wait:
        mine.wait()


def _matmul(name, a, b, out_dtype, n_cols=None, col0=0, tm=1024, tn=1024, tk=2048, xchg=None,
            gather=False):
    m, k = a.shape
    n = b.shape[1] if n_cols is None else n_cols
    tm = _pick(m, tm)
    tn = _pick(n, tn)
    while col0 % tn:
        tn //= 2
    tk = _pick(k, tk)
    nk = k // tk
    cb = col0 // tn
    grid = (m // tm, n // tn, nk)

    def kern(a_ref, b_ref, *rest):
        if xchg is None:
            o_ref, acc_ref = rest
        else:
            x_ref, o_ref, xo_ref, acc_ref, send_sems, recv_sems, local_sem = rest
            step = (pl.program_id(0) * grid[1] + pl.program_id(1)) * nk + pl.program_id(2)

            @pl.when(step == 0)
            def _():
                _exchange_copies(x_ref, xo_ref, send_sems, recv_sems, local_sem, gather, True, False)

        kk = pl.program_id(2)
        part = jnp.dot(a_ref[...], b_ref[...], preferred_element_type=F32)
        if nk == 1:
            o_ref[...] = part.astype(o_ref.dtype)
        else:
            @pl.when(kk == 0)
            def _():
                acc_ref[...] = part

            @pl.when(kk > 0)
            def _():
                acc_ref[...] += part

            @pl.when(kk == nk - 1)
            def _():
                o_ref[...] = acc_ref[...].astype(o_ref.dtype)

        if xchg is not None:
            @pl.when(step == grid[0] * grid[1] * nk - 1)
            def _():
                _exchange_copies(x_ref, xo_ref, send_sems, recv_sems, local_sem, gather, False, True)

    in_specs = [pl.BlockSpec((tm, tk), lambda i, j, kk: (i, kk)),
                pl.BlockSpec((tk, tn), lambda i, j, kk: (kk, j + cb))]
    out_specs = [pl.BlockSpec((tm, tn), lambda i, j, kk: (i, j))]
    out_shape = [jax.ShapeDtypeStruct((m, n), out_dtype)]
    scratch = [pltpu.VMEM((tm, tn), F32)]
    args = (a, b)
    if xchg is not None:
        in_specs.append(pl.BlockSpec(memory_space=pl.ANY))
        out_specs.append(pl.BlockSpec(memory_space=pl.ANY))
        out_shape.append(_exchange_shape(xchg, gather))
        scratch += EXCHANGE_SEMS
        args = (a, b, xchg)
    res = pl.pallas_call(
        kern, name=name, grid=grid, in_specs=in_specs, out_specs=out_specs, out_shape=out_shape,
        scratch_shapes=scratch,
        compiler_params=pltpu.CompilerParams(
            dimension_semantics=("arbitrary", "arbitrary", "arbitrary"),
            vmem_limit_bytes=VMEM_LIMIT),
    )(*args)
    return res[0] if xchg is None else (res[0], res[1])


def _exchange(name, x, gather):
    def body(x_ref, out_ref, send_sems, recv_sems, local_sem):
        _exchange_copies(x_ref, out_ref, send_sems, recv_sems, local_sem, gather, True, True)

    return pl.pallas_call(
        body, name=name,
        out_shape=_exchange_shape(x, gather),
        in_specs=[pl.BlockSpec(memory_space=pl.ANY)],
        out_specs=pl.BlockSpec(memory_space=pl.ANY),
        scratch_shapes=EXCHANGE_SEMS,
    )(x)


def _gather_two_level(name, x):
    def body(x_ref, out_ref, send_sems, recv_sems, local_sem):
        xi, yi, ci = lax.axis_index("x"), lax.axis_index("y"), lax.axis_index("c")
        me, sibling = (xi, yi, ci), (xi, yi, 1 - ci)
        chips = [(1 - xi, yi), (xi, 1 - yi), (1 - xi, 1 - yi)]

        def slot(px, py, pc):
            return out_ref.at[4 * px + 2 * py + pc]

        def copy(k, block, to, src=None):
            return pltpu.make_async_remote_copy(
                src_ref=slot(*block) if src is None else src, dst_ref=slot(*block),
                send_sem=send_sems.at[k], recv_sem=recv_sems.at[k], device_id=to,
                device_id_type=MESH)

        mine = pltpu.make_async_copy(x_ref, slot(*me), local_sem)
        mine.start()
        first = [copy(0, me, sibling, src=x_ref)]
        first += [copy(1 + j, me, (*chip, ci), src=x_ref) for j, chip in enumerate(chips)]
        for cp in first:
            cp.start()
        passed = [copy(4 + j, (*chip, ci), sibling) for j, chip in enumerate(chips)]
        for j, chip in enumerate(chips):
            copy(1 + j, (*chip, ci), me).wait_recv()
            passed[j].start()
        copy(0, sibling, me).wait_recv()
        for j, chip in enumerate(chips):
            copy(4 + j, (*chip, 1 - ci), me).wait_recv()
        for cp in first + passed:
            cp.wait_send()
        mine.wait()

    return pl.pallas_call(
        body, name=name,
        out_shape=_exchange_shape(x, True),
        in_specs=[pl.BlockSpec(memory_space=pl.ANY)],
        out_specs=pl.BlockSpec(memory_space=pl.ANY),
        scratch_shapes=EXCHANGE_SEMS,
    )(x)


def _adamw(name, parts, w, m, v):
    p, r, c = parts.shape
    tr = r if r <= 64 else _pick(r, 64)

    def body(idx, parts_v, w_v, m_v, v_v):
        g = parts_v[0].astype(F32)
        for s in range(1, p):
            g = g + parts_v[s].astype(F32)
        m2 = ADAM_B1 * m_v + (1.0 - ADAM_B1) * g
        v2 = ADAM_B2 * v_v + (1.0 - ADAM_B2) * (g * g)
        m_hat = m2 / (1.0 - ADAM_B1 ** ADAM_STEP)
        v_hat = v2 / (1.0 - ADAM_B2 ** ADAM_STEP)
        delta = -ADAM_LR * (m_hat / (jnp.sqrt(v_hat) + ADAM_EPS) + ADAM_WD * w_v)
        return g, delta, m2, v2

    rb = ((tr, c), lambda i: (i, 0))
    return _rowcall(
        name, body, (r // tr,),
        [(parts, (p, tr, c), lambda i: (0, i, 0)), (w,) + rb, (m,) + rb, (v,) + rb],
        [((r, c), F32) + rb + ("set",)] * 4)


TR_WIDE = 256


def _norm_mod_fwd(x, shift, scale, norm_w):
    t, d = x.shape
    tr = _pick(t, TR_WIDE)

    def body(idx, x_v, sh, sc, nw):
        rstd = lax.rsqrt(jnp.mean(x_v * x_v, axis=1, keepdims=True) + NORM_EPS)
        return ((x_v * rstd) * nw * (1.0 + sc) + sh,)

    rb = ((tr, d), lambda i: (i, 0))
    return _rowcall("norm_mod_fwd", body, (t // tr,),
                    [(x,) + rb, _full(shift), _full(scale), _full(norm_w)],
                    [((t, d), BF16) + rb + ("set",)])[0]


def _norm_mod_bwd(dh_a, dh_b, x, dx2, scale, norm_w):
    t, d = x.shape
    tr = _pick(t, TR_WIDE)

    def body(idx, dha, dhb, x_v, dx2_v, sc, nw):
        dh = dha + dhb
        rstd = lax.rsqrt(jnp.mean(x_v * x_v, axis=1, keepdims=True) + NORM_EPS)
        xn = x_v * rstd
        m1 = 1.0 + sc
        dxn = dh * nw * m1
        dx = rstd * (dxn - xn * jnp.mean(dxn * xn, axis=1, keepdims=True))
        dhx = dh * xn
        return (dx2_v + dx,
                jnp.sum(dh, axis=0, keepdims=True),
                jnp.sum(dhx * nw, axis=0, keepdims=True),
                jnp.sum(dhx * m1, axis=0, keepdims=True))

    rb = ((tr, d), lambda i: (i, 0))
    vec = ((1, d), F32, (1, d), lambda i: (0, 0), "acc_all")
    return _rowcall("norm_mod_bwd", body, (t // tr,),
                    [(dh_a,) + rb, (dh_b,) + rb, (x,) + rb, (dx2,) + rb, _full(scale), _full(norm_w)],
                    [((t, d), F32) + rb + ("set",), vec, vec, vec])


SB_TILE = 512


SB_SCAN = 256
SB_DEAD = 105.0


def _neg_log_not_beta(z):
    return jnp.maximum(z, 0.0) + jnp.log(1.0 + jnp.exp(-jnp.abs(z)))


def _key_scan(x, tri, later):
    n_blk = x.shape[1] // SB_SCAN
    parts = [x[:, g * SB_SCAN:(g + 1) * SB_SCAN] for g in range(n_blk)]
    sums = [jnp.sum(p, axis=1, keepdims=True) for p in parts]
    outs = []
    for g in range(n_blk):
        o = _dot2(parts[g], tri)
        others = range(g + 1, n_blk) if later else range(g)
        for g2 in others:
            o = o + sums[g2]
        outs.append(o)
    total = sums[0]
    for s in sums[1:]:
        total = total + s
    return (outs[0] if n_blk == 1 else jnp.concatenate(outs, axis=1)), total


def _scan_tri(kind):
    row = lax.broadcasted_iota(jnp.int32, (SB_SCAN, SB_SCAN), 0)
    col = lax.broadcasted_iota(jnp.int32, (SB_SCAN, SB_SCAN), 1)
    return {"after": row > col, "upto": row <= col, "before": row < col}[kind].astype(BF16)


def _sb_fwd(qkv, n_heads):
    t = qkv.shape[0]
    tq = _pick(t, SB_TILE)
    nq = t // tq
    scale = HEAD_DIM ** -0.5

    def kern(q_ref, k_ref, v_ref, o_ref, c_ref, nd_ref):
        i_blk = pl.program_id(1)
        row = lax.broadcasted_iota(jnp.int32, (tq, tq), 0)
        col = lax.broadcasted_iota(jnp.int32, (tq, tq), 1)
        causal = col < row
        after = _scan_tri("after")
        qb = q_ref[...]

        def tile(j_blk, c, acc, diag):
            r0 = pl.multiple_of(j_blk * tq, tq)
            kb = k_ref[pl.ds(r0, tq), :]
            vb = v_ref[pl.ds(r0, tq), :]
            z = lax.dot_general(qb, kb, NT, preferred_element_type=F32) * scale
            n = _neg_log_not_beta(z)
            if diag:
                n = jnp.where(causal, n, 0.0)
            later, total = _key_scan(n, after, True)
            a = jnp.exp(z - (n + later + c))
            if diag:
                a = jnp.where(causal, a, 0.0)
            acc = acc + jnp.dot(a.astype(BF16), vb, preferred_element_type=F32)
            return c + total, acc

        c, acc = tile(i_blk, jnp.zeros((tq, 1), F32), jnp.zeros((tq, HEAD_DIM), F32), True)

        def more(st):
            return jnp.logical_and(st[0] < i_blk, jnp.min(st[1]) <= SB_DEAD)

        def step(st):
            c2, acc2 = tile(i_blk - 1 - st[0], st[1], st[2], False)
            return st[0] + 1, c2, acc2

        n_done, c, acc = lax.while_loop(more, step, (jnp.int32(0), c, acc))
        o_ref[...] = acc
        c_ref[...] = jnp.broadcast_to(c, (tq, HEAD_DIM))
        nd_ref[...] = jnp.full((8, LANES), n_done.astype(F32))

    h = n_heads
    return pl.pallas_call(
        kern, name="sb_fwd", grid=(h, nq),
        in_specs=[pl.BlockSpec((tq, HEAD_DIM), lambda hh, i: (i, hh)),
                  pl.BlockSpec((t, HEAD_DIM), lambda hh, i: (0, h + hh)),
                  pl.BlockSpec((t, HEAD_DIM), lambda hh, i: (0, 2 * h + hh))],
        out_specs=[pl.BlockSpec((tq, HEAD_DIM), lambda hh, i: (i, hh)),
                   pl.BlockSpec((tq, HEAD_DIM), lambda hh, i: (i, hh)),
                   pl.BlockSpec((None, None, 8, LANES), lambda hh, i: (hh, i, 0, 0))],
        out_shape=[jax.ShapeDtypeStruct((t, h * HEAD_DIM), F32),
                   jax.ShapeDtypeStruct((t, h * HEAD_DIM), F32),
                   jax.ShapeDtypeStruct((h, nq, 8, LANES), F32)],
        compiler_params=pltpu.CompilerParams(
            dimension_semantics=("arbitrary", "arbitrary"), vmem_limit_bytes=VMEM_LIMIT),
    )(qkv, qkv, qkv)


def _sb_bwd(qkv, do, ctot, n_visited, n_heads):
    t = qkv.shape[0]
    tq = _pick(t, SB_TILE)
    nq = t // tq
    scale = HEAD_DIM ** -0.5

    def kern(q_ref, do_ref, c_ref, nd_ref, k_ref, v_ref, dq_ref, dk_ref, dv_ref, dk_acc, dv_acc):
        i_blk = pl.program_id(1)
        row = lax.broadcasted_iota(jnp.int32, (tq, tq), 0)
        col = lax.broadcasted_iota(jnp.int32, (tq, tq), 1)
        causal = col < row
        upto = _scan_tri("upto")
        before = _scan_tri("before")
        qb = q_ref[...]
        dob = do_ref[...]
        ctot_v = c_ref[...][:, 0:1]

        @pl.when(i_blk == 0)
        def _():
            dk_acc[...] = jnp.zeros_like(dk_acc)
            dv_acc[...] = jnp.zeros_like(dv_acc)

        def tile(j_blk, cl, pe, dq, diag):
            r0 = pl.multiple_of(j_blk * tq, tq)
            kb = k_ref[pl.ds(r0, tq), :]
            vb = v_ref[pl.ds(r0, tq), :]
            z = lax.dot_general(qb, kb, NT, preferred_element_type=F32) * scale
            n = _neg_log_not_beta(z)
            if diag:
                n = jnp.where(causal, n, 0.0)
            upto_s, n_total = _key_scan(n, upto, False)
            lb = z - n
            a = jnp.exp(lb - ((ctot_v - cl) - upto_s))
            if diag:
                a = jnp.where(causal, a, 0.0)
            da = lax.dot_general(dob, vb, NT, preferred_element_type=F32)
            e = da * a
            before_s, e_total = _key_scan(e, before, False)
            beta = jnp.exp(lb)
            dz = (e - beta * (e + (pe + before_s))) * scale
            if diag:
                dz = jnp.where(causal, dz, 0.0)
            dzb = dz.astype(BF16)
            dq = dq + jnp.dot(dzb, kb, preferred_element_type=F32)
            dk_acc[pl.ds(r0, tq), :] += lax.dot_general(dzb, qb, TN, preferred_element_type=F32)
            dv_acc[pl.ds(r0, tq), :] += lax.dot_general(a.astype(BF16), dob, TN,
                                                        preferred_element_type=F32)
            return cl + n_total, pe + e_total, dq

        zero = jnp.zeros((tq, 1), F32)
        first = i_blk - jnp.max(nd_ref[...]).astype(jnp.int32)
        cl, pe, dq = lax.fori_loop(first, i_blk, lambda j, s: tile(j, s[0], s[1], s[2], False),
                                   (zero, zero, jnp.zeros((tq, HEAD_DIM), F32)))
        cl, pe, dq = tile(i_blk, cl, pe, dq, True)
        dq_ref[...] = dq.astype(BF16)

        @pl.when(i_blk == nq - 1)
        def _():
            dk_ref[...] = dk_acc[...].astype(BF16)
            dv_ref[...] = dv_acc[...].astype(BF16)

    h = n_heads
    qspec = pl.BlockSpec((tq, HEAD_DIM), lambda hh, i: (i, hh))
    hspec = pl.BlockSpec((t, HEAD_DIM), lambda hh, i: (0, hh))
    out = jax.ShapeDtypeStruct((t, h * HEAD_DIM), BF16)
    return pl.pallas_call(
        kern, name="sb_bwd", grid=(h, nq),
        in_specs=[qspec, qspec, qspec,
                  pl.BlockSpec((None, None, 8, LANES), lambda hh, i: (hh, i, 0, 0)),
                  pl.BlockSpec((t, HEAD_DIM), lambda hh, i: (0, h + hh)),
                  pl.BlockSpec((t, HEAD_DIM), lambda hh, i: (0, 2 * h + hh))],
        out_specs=[qspec, hspec, hspec],
        out_shape=[out, out, out],
        scratch_shapes=[pltpu.VMEM((t, HEAD_DIM), F32), pltpu.VMEM((t, HEAD_DIM), F32)],
        compiler_params=pltpu.CompilerParams(
            dimension_semantics=("arbitrary", "arbitrary"), vmem_limit_bytes=VMEM_LIMIT),
    )(qkv, do, ctot, n_visited, qkv, qkv)


GDN_ROWS = 512
GDN_HEADS_PER_STEP = 2
TR_HEAD = 512


def _shift_rows(cur, halo, k, back):
    n = cur.shape[0]
    ext = jnp.concatenate([cur, halo], axis=0)
    return pltpu.roll(ext, k if back else n + 8 - k, 0)[:n]


def _conv_fwd(cur, halo, w):
    out = cur * w[GDN_CONV - 1:GDN_CONV, :]
    for i in range(GDN_CONV - 1):
        out = out + _shift_rows(cur, halo, GDN_CONV - 1 - i, True) * w[i:i + 1, :]
    return out


def _chunk_tri(n, upper):
    row = lax.broadcasted_iota(jnp.int32, (n, n), 0)
    col = lax.broadcasted_iota(jnp.int32, (n, n), 1)
    same = (row // GDN_CHUNK) == (col // GDN_CHUNK)
    tri = (col >= row) if upper else (col <= row)
    return jnp.logical_and(same, tri).astype(BF16)


def _lane_pick(x, lane):
    idx = lax.broadcasted_iota(jnp.int32, x.shape, 1)
    return jnp.sum(jnp.where(idx == lane, x, 0.0), axis=1, keepdims=True)


def _softplus(x):
    y = jnp.exp(-jnp.abs(x))
    u = 1.0 + y
    log1p = jnp.where(u == 1.0, y, jnp.log(u) * (y / jnp.where(u == 1.0, 1.0, u - 1.0)))
    return jnp.maximum(x, 0.0) + log1p


def _gdn_specs(t, tr, h, proj_seg0):
    def slab(seg):
        return ((tr, HEAD_DIM), lambda i, hh: (i, seg * h + hh))

    def halo_before(seg):
        return ((8, HEAD_DIM), lambda i, hh: (jnp.maximum(i * (tr // 8) - 1, 0), seg * h + hh))

    return slab, halo_before


def _gdn_pre_fwd(proj_b, proj_c, conv_w8, ab, n_heads):
    t = proj_b.shape[0]
    h = n_heads
    d = h * HEAD_DIM
    tr = _pick(t, TR_HEAD)
    slab, halo = _gdn_specs(t, tr, h, 1)
    scale = HEAD_DIM ** -0.5

    def body(idx, q_c, q_h, k_c, k_h, v_c, v_h, wq, wk, wv, gbga, ab_v):
        i, hh = idx
        live = (i > 0).astype(F32)
        outs = []
        for cur, hal, w, kind in ((q_c, q_h, wq, "q"), (k_c, k_h, wk, "k"), (v_c, v_h, wv, "v")):
            u = _conv_fwd(cur, hal * live, w)
            s = u * _sigmoid(u)
            if kind != "v":
                s = s * lax.rsqrt(jnp.sum(s * s, axis=1, keepdims=True) + L2_EPS)
            if kind == "q":
                s = s * scale
            outs.append(s)
        beta = _sigmoid(_lane_pick(gbga, hh))
        a_log = _lane_pick(ab_v[0:1, :], hh)
        dt = _lane_pick(ab_v[1:2, :], hh)
        g = -jnp.exp(a_log) * _softplus(_lane_pick(gbga, h + hh) + dt)
        g_rep = jnp.broadcast_to(g, (tr, HEAD_DIM))
        big_g = _cdot3(_chunk_tri(tr, False), g_rep)
        return outs + [jnp.broadcast_to(beta, (tr, HEAD_DIM)), big_g]

    wspec = lambda seg: ((8, HEAD_DIM), lambda i, hh: (0, seg * h + hh))
    out = ((t, d), F32, (tr, HEAD_DIM), lambda i, hh: (i, hh), "set")
    return _rowcall(
        "gdn_pre_fwd", body, (t // tr, h),
        [(proj_b,) + slab(1), (proj_b,) + halo(1), (proj_b,) + slab(2), (proj_b,) + halo(2),
         (proj_b,) + slab(3), (proj_b,) + halo(3),
         (conv_w8,) + wspec(0), (conv_w8,) + wspec(1), (conv_w8,) + wspec(2),
         (proj_c, (tr, LANES), lambda i, hh: (i, 0)), _full(ab)],
        [out] * 5)


def _gdn_consts():
    row = lax.broadcasted_iota(jnp.int32, (GDN_CHUNK, GDN_CHUNK), 0)
    col = lax.broadcasted_iota(jnp.int32, (GDN_CHUNK, GDN_CHUNK), 1)
    lane = lax.broadcasted_iota(jnp.int32, (GDN_CHUNK, HEAD_DIM), 1)
    return row > col, row >= col, (row == col).astype(F32), (lane == 0).astype(BF16)


def _gdn_local(q, k, v, be, ge, g_lanes, consts):
    lower, tril, eye, sel = consts
    kb_ = k * be
    vb_ = v * be
    e_g = jnp.exp(ge)
    kg = kb_ * e_g
    g_i = ge[:, :GDN_CHUNK]
    g_j = jnp.broadcast_to(g_lanes, (GDN_CHUNK, GDN_CHUNK))
    dec = jnp.where(tril, jnp.exp(jnp.minimum(g_i - g_j, 0.0)), 0.0)
    kk = _bdot(kb_, k, NT)
    qk = _bdot(q, k, NT)
    g_last = jnp.min(ge, axis=0, keepdims=True)
    kdec_f = jnp.exp(g_last - ge)
    return dict(kb=kb_, vb=vb_, e_g=e_g, kg=kg, dec=dec, kk=kk, qk=qk, kdec_f=kdec_f,
                k_dec=k * kdec_f, q_dec=q * e_g, gamma=jnp.exp(g_last),
                intra=jnp.where(tril, qk * dec, 0.0))


def _wy_lower_t(k, be, ge, g_lanes):
    row = lax.broadcasted_iota(jnp.int32, (GDN_CHUNK, GDN_CHUNK), 0)
    col = lax.broadcasted_iota(jnp.int32, (GDN_CHUNK, GDN_CHUNK), 1)
    g_row = ge[:, :GDN_CHUNK]
    g_col = jnp.broadcast_to(g_lanes, (GDN_CHUNK, GDN_CHUNK))
    dec_t = jnp.exp(jnp.minimum(g_col - g_row, 0.0))
    return jnp.where(col > row, _bdot(k, k * be, NT) * dec_t, 0.0)


def _unit_lower_inverse(lw_t):
    n = lw_t.shape[0]
    n_slab = GDN_CHUNK // 8
    row = lax.broadcasted_iota(jnp.int32, (n, 8, GDN_CHUNK), 1)
    col = lax.broadcasted_iota(jnp.int32, (n, 8, GDN_CHUNK), 2)
    unit = lax.broadcasted_iota(jnp.int32, (1, 1, GDN_CHUNK), 2)
    lw = [lw_t[:, 8 * g:8 * g + 8, :] for g in range(n_slab)]
    inv = [(col == row + 8 * g).astype(F32) for g in range(n_slab)]
    for i in range(1, GDN_CHUNK):
        acc = lw[0][:, :, i:i + 1] * inv[0]
        for g in range(1, (i - 1) // 8 + 1):
            acc = acc + lw[g][:, :, i:i + 1] * inv[g]
        new_row = (unit == i).astype(F32) - jnp.sum(acc, axis=1, keepdims=True)
        inv[i // 8] = jnp.where(row == i % 8, new_row, inv[i // 8])
    return jnp.concatenate(inv, axis=1)


def _gdn_fwd(q, k, v, be, ge, ge_t, n_heads):
    t = q.shape[0]
    h = n_heads
    hg = GDN_HEADS_PER_STEP
    tr = _pick(t, GDN_ROWS)
    nc = tr // GDN_CHUNK

    def kern(q_ref, k_ref, v_ref, b_ref, g_ref, gt_ref, o_ref, s_ref, tm_ref, state):
        consts = _gdn_consts()
        lower, tril, eye, sel = consts
        lanes = [pl.ds(hs * HEAD_DIM, HEAD_DIM) for hs in range(hg)]

        @pl.when(pl.program_id(1) == 0)
        def _():
            state[...] = jnp.zeros_like(state)

        lw_t = []
        for hs in range(hg):
            for ci in range(nc):
                rows = pl.ds(ci * GDN_CHUNK, GDN_CHUNK)
                lw_t.append(_wy_lower_t(k_ref[rows, lanes[hs]], b_ref[rows, lanes[hs]],
                                        g_ref[rows, lanes[hs]], gt_ref[hs, 0:1, rows]))
        t_all = _unit_lower_inverse(jnp.stack(lw_t))
        for hs in range(hg):
            for ci in range(nc):
                tm_ref[hs, pl.ds(ci * GDN_CHUNK, GDN_CHUNK), :] = t_all[hs * nc + ci]

        local = {}
        for ci in range(nc):
            rows = pl.ds(ci * GDN_CHUNK, GDN_CHUNK)
            for hs in range(hg):
                ln = lanes[hs]
                loc = _gdn_local(q_ref[rows, ln], k_ref[rows, ln], v_ref[rows, ln], b_ref[rows, ln],
                                 g_ref[rows, ln], gt_ref[hs, 0:1, rows], consts)
                t_mat = t_all[hs * nc + ci]
                local[ci, hs] = (_bdot(t_mat, loc["vb"]), _bdot(t_mat, loc["kg"]).astype(BF16),
                                 loc["q_dec"].astype(BF16), loc["intra"].astype(BF16),
                                 loc["k_dec"].astype(BF16), loc["gamma"])
        states = [state[hs] for hs in range(hg)]
        for ci in range(nc):
            rows = pl.ds(ci * GDN_CHUNK, GDN_CHUNK)
            for hs in range(hg):
                u, w, q_dec, intra, k_dec, gamma = local[ci, hs]
                s = states[hs]
                s_ref[hs, ci] = s
                v_new = u - _bdot(w, s)
                o_ref[rows, lanes[hs]] = _bdot(q_dec, s) + _bdot(intra, v_new)
                states[hs] = s * gamma + _bdot(k_dec, v_new, TN)
        for hs in range(hg):
            state[hs] = states[hs]

    slab = pl.BlockSpec((tr, hg * HEAD_DIM), lambda hp, j: (j, hp))
    return pl.pallas_call(
        kern, name="gdn_fwd", grid=(h // hg, t // tr),
        in_specs=[slab] * 5 + [pl.BlockSpec((hg, 8, tr), lambda hp, j: (hp, 0, j))],
        out_specs=[slab,
                   pl.BlockSpec((hg, nc, HEAD_DIM, HEAD_DIM), lambda hp, j: (hp, j, 0, 0)),
                   pl.BlockSpec((hg, tr, GDN_CHUNK), lambda hp, j: (hp, j, 0))],
        out_shape=[jax.ShapeDtypeStruct((t, h * HEAD_DIM), F32),
                   jax.ShapeDtypeStruct((h, t // GDN_CHUNK, HEAD_DIM, HEAD_DIM), F32),
                   jax.ShapeDtypeStruct((h, t, GDN_CHUNK), F32)],
        scratch_shapes=[pltpu.VMEM((hg, HEAD_DIM, HEAD_DIM), F32)],
        compiler_params=pltpu.CompilerParams(
            dimension_semantics=("arbitrary", "arbitrary"), vmem_limit_bytes=VMEM_LIMIT),
    )(q, k, v, be, ge, ge_t)


def _gdn_bwd(q, k, v, be, ge, ge_t, do, states, tms, n_heads):
    t = q.shape[0]
    h = n_heads
    hg = GDN_HEADS_PER_STEP
    tr = _pick(t, GDN_ROWS)
    nc = tr // GDN_CHUNK
    nj = t // tr

    def kern(q_ref, k_ref, v_ref, b_ref, g_ref, gt_ref, do_ref, s_ref, tm_ref,
             dq_ref, dk_ref, dv_ref, db_ref, dg_ref, dstate):
        consts = _gdn_consts()
        lower, tril, eye, sel = consts
        last_row = lax.broadcasted_iota(jnp.int32, (GDN_CHUNK, HEAD_DIM), 0) == GDN_CHUNK - 1

        @pl.when(pl.program_id(1) == 0)
        def _():
            dstate[...] = jnp.zeros_like(dstate)

        def lsum(x):
            return jnp.sum(x, axis=1, keepdims=True)

        def before(ci, hs):
            rows = pl.ds(ci * GDN_CHUNK, GDN_CHUNK)
            ln = pl.ds(hs * HEAD_DIM, HEAD_DIM)
            qv, kv, vv, bev = q_ref[rows, ln], k_ref[rows, ln], v_ref[rows, ln], b_ref[rows, ln]
            loc = _gdn_local(qv, kv, vv, bev, g_ref[rows, ln], gt_ref[hs, 0:1, rows], consts)
            t_mat = tm_ref[hs, rows, :]
            s = s_ref[hs, ci]
            d_o = do_ref[rows, ln]
            w = _bdot(t_mat, loc["kg"])
            v_new = _bdot(t_mat, loc["vb"]) - _bdot(w, s)
            return dict(loc=loc, qv=qv, kv=kv, vv=vv, bev=bev, t_mat=t_mat, s=s, w=w, v_new=v_new,
                        dv_new0=_bdot(loc["intra"], d_o, TN), ds0=_bdot(loc["q_dec"], d_o, TN),
                        d_intra=jnp.where(tril, _bdot(d_o, v_new, NT), 0.0),
                        dq_dec=_bdot(d_o, s, NT))

        def recur(c, d_s):
            loc = c["loc"]
            dv_new = c["dv_new0"] + _bdot(loc["k_dec"], d_s)
            c.update(dv_new=dv_new, dk_dec=_bdot(c["v_new"], d_s, NT),
                     dgamma=jnp.sum(lsum(d_s * c["s"]), axis=0, keepdims=True))
            return c["ds0"] + loc["gamma"] * d_s - _bdot(c["w"], dv_new, TN)

        def after(c):
            loc, qv, kv, vv, bev, t_mat, s = (c[n] for n in ("loc", "qv", "kv", "vv", "bev", "t_mat",
                                                             "s"))
            dv_new, dk_dec, dgamma, d_intra, dq_dec = (c[n] for n in ("dv_new", "dk_dec", "dgamma",
                                                                      "d_intra", "dq_dec"))
            dw = -_bdot(dv_new, s, NT)
            dtm = _bdot(dv_new, loc["vb"], NT) + _bdot(dw, loc["kg"], NT)
            dvb = _bdot(t_mat, dv_new, TN)
            dkg = _bdot(t_mat, dw, TN)
            dlw = jnp.where(lower, -_dot3(t_mat, _dot3(dtm, t_mat, NT), TN), 0.0)
            dkk = dlw * loc["dec"]
            dqk = d_intra * loc["dec"]
            ddec = dlw * loc["kk"] + d_intra * loc["qk"]
            dkb = _bdot(dkk, kv) + dkg * loc["e_g"]
            dk = (_bdot(dkk, loc["kb"], TN) + _bdot(dqk, qv, TN) + dk_dec * loc["kdec_f"]
                  + dkb * bev)
            dq = _bdot(dqk, kv) + dq_dec * loc["e_g"]
            dgd = ddec * loc["dec"]
            r_kdec = lsum(dk_dec * loc["k_dec"])
            col_sums = lsum(_dot2(dgd, sel, TN))
            d_big_g = (lsum(dgd) - col_sums
                       + lsum(dq_dec * loc["q_dec"]) - r_kdec + lsum(dkg * loc["kg"]))
            d_last = jnp.sum(r_kdec, axis=0, keepdims=True) + dgamma * loc["gamma"][:, 0:1]
            d_big_g = jnp.broadcast_to(d_big_g, (GDN_CHUNK, HEAD_DIM)) + jnp.where(last_row, d_last,
                                                                                  0.0)
            d_beta = jnp.broadcast_to(lsum(dkb * kv) + lsum(dvb * vv), (GDN_CHUNK, HEAD_DIM))
            return dq, dk, dvb * bev, d_beta, d_big_g

        work = {(ci, hs): before(ci, hs) for ci in range(nc) for hs in range(hg)}
        d_states = [dstate[hs] for hs in range(hg)]
        for ci in range(nc - 1, -1, -1):
            for hs in range(hg):
                d_states[hs] = recur(work[ci, hs], d_states[hs])
        for hs in range(hg):
            dstate[hs] = d_states[hs]
        for ci in range(nc):
            rows = pl.ds(ci * GDN_CHUNK, GDN_CHUNK)
            for hs in range(hg):
                ln = pl.ds(hs * HEAD_DIM, HEAD_DIM)
                for ref, val in zip((dq_ref, dk_ref, dv_ref, db_ref, dg_ref), after(work[ci, hs])):
                    ref[rows, ln] = val

    slab = pl.BlockSpec((tr, hg * HEAD_DIM), lambda hp, j: (nj - 1 - j, hp))
    out = jax.ShapeDtypeStruct((t, h * HEAD_DIM), F32)
    return pl.pallas_call(
        kern, name="gdn_bwd", grid=(h // hg, nj),
        in_specs=[slab] * 5 + [pl.BlockSpec((hg, 8, tr), lambda hp, j: (hp, 0, nj - 1 - j)), slab] + [
            pl.BlockSpec((hg, nc, HEAD_DIM, HEAD_DIM), lambda hp, j: (hp, nj - 1 - j, 0, 0)),
            pl.BlockSpec((hg, tr, GDN_CHUNK), lambda hp, j: (hp, nj - 1 - j, 0))],
        out_specs=[slab] * 5,
        out_shape=[out] * 5,
        scratch_shapes=[pltpu.VMEM((hg, HEAD_DIM, HEAD_DIM), F32)],
        compiler_params=pltpu.CompilerParams(
            dimension_semantics=("arbitrary", "arbitrary"), vmem_limit_bytes=VMEM_LIMIT),
    )(q, k, v, be, ge, ge_t, do, states, tms)


def _gdn_pre_bwd(proj_b, proj_c, conv_w8, ab, dq, dk, dv, dbe, dge, n_heads):
    t = proj_b.shape[0]
    h = n_heads
    d = h * HEAD_DIM
    tr = _pick(t, TR_HEAD)
    slab, halo = _gdn_specs(t, tr, h, 1)
    scale = HEAD_DIM ** -0.5

    def body(idx, q_c, q_h, k_c, k_h, v_c, v_h, wq, wk, wv, gbga, ab_v, dq_v, dk_v, dv_v, dbe_v,
             dge_v):
        i, hh = idx
        live = (i > 0).astype(F32)
        outs = []
        for cur, hal, w, dy, kind in ((q_c, q_h, wq, dq_v, "q"), (k_c, k_h, wk, dk_v, "k"),
                                      (v_c, v_h, wv, dv_v, "v")):
            u = _conv_fwd(cur, hal * live, w)
            sg = _sigmoid(u)
            if kind == "v":
                ds = dy
            else:
                s = u * sg
                r = lax.rsqrt(jnp.sum(s * s, axis=1, keepdims=True) + L2_EPS)
                y = s * r
                if kind == "q":
                    dy = dy * scale
                ds = r * (dy - y * jnp.sum(dy * y, axis=1, keepdims=True))
            outs.append(ds * (sg * (1.0 + u * (1.0 - sg))))
        lane = lax.broadcasted_iota(jnp.int32, (tr, LANES), 1)
        lane1 = lax.broadcasted_iota(jnp.int32, (1, LANES), 1)
        beta = _sigmoid(_lane_pick(gbga, hh))
        a_neg = -jnp.exp(_lane_pick(ab_v[0:1, :], hh))
        xg = _lane_pick(gbga, h + hh) + _lane_pick(ab_v[1:2, :], hh)
        g = a_neg * _softplus(xg)
        dgb = dbe_v * (beta * (1.0 - beta))
        dg = _cdot3(_chunk_tri(tr, True), dge_v)
        dga = dg * (a_neg * _sigmoid(xg))
        d_gates = jnp.where(lane == hh, dgb, 0.0) + jnp.where(lane == h + hh, dga, 0.0)
        d_ab = (jnp.where(lane1 == hh, jnp.sum(dg * g, axis=0, keepdims=True), 0.0)
                + jnp.where(lane1 == h + hh, jnp.sum(dga, axis=0, keepdims=True), 0.0))
        return outs + [d_gates, d_ab]

    wspec = lambda seg: ((8, HEAD_DIM), lambda i, hh: (0, seg * h + hh))
    hs = ((tr, HEAD_DIM), lambda i, hh: (i, hh))
    out = ((t, d), F32) + hs + ("set",)
    return _rowcall(
        "gdn_pre_bwd", body, (t // tr, h),
        [(proj_b,) + slab(1), (proj_b,) + halo(1), (proj_b,) + slab(2), (proj_b,) + halo(2),
         (proj_b,) + slab(3), (proj_b,) + halo(3),
         (conv_w8,) + wspec(0), (conv_w8,) + wspec(1), (conv_w8,) + wspec(2),
         (proj_c, (tr, LANES), lambda i, hh: (i, 0)), _full(ab),
         (dq,) + hs, (dk,) + hs, (dv,) + hs, (dbe,) + hs, (dge,) + hs],
        [out, out, out,
         ((t, LANES), F32, (tr, LANES), lambda i, hh: (i, 0), "acc_inner"),
         ((1, LANES), F32, (1, LANES), lambda i, hh: (0, 0), "acc_all")])


def _conv_bwd(proj_b, conv_w8, du_q, du_k, du_v, n_heads):
    t = proj_b.shape[0]
    h = n_heads
    d = h * HEAD_DIM
    tr = _pick(t, TR_HEAD)
    ni = t // tr

    def body(idx, q_c, q_h, k_c, k_h, v_c, v_h, wq, wk, wv, uq, uq_n, uk, uk_n, uv, uv_n):
        hh, i = idx
        live_b = (i > 0).astype(F32)
        live_a = (i < ni - 1).astype(F32)
        d_ins, d_ws = [], []
        for cur, hal, w, du, du_n in ((q_c, q_h, wq, uq, uq_n), (k_c, k_h, wk, uk, uk_n),
                                      (v_c, v_h, wv, uv, uv_n)):
            hal = hal * live_b
            du_n = du_n * live_a
            d_in = du * w[GDN_CONV - 1:GDN_CONV, :]
            rows = [jnp.sum(du * cur, axis=0, keepdims=True)]
            for i_tap in range(GDN_CONV - 2, -1, -1):
                kshift = GDN_CONV - 1 - i_tap
                d_in = d_in + _shift_rows(du, du_n, kshift, False) * w[i_tap:i_tap + 1, :]
                rows.insert(0, jnp.sum(du * _shift_rows(cur, hal, kshift, True), axis=0,
                                       keepdims=True))
            d_ins.append(d_in)
            tap = lax.broadcasted_iota(jnp.int32, (8, HEAD_DIM), 0)
            d_w = jnp.zeros((8, HEAD_DIM), F32)
            for i_tap in range(GDN_CONV):
                d_w = d_w + jnp.where(tap == i_tap, rows[i_tap], 0.0)
            d_ws.append(d_w)
        return d_ins + d_ws

    def slab(seg):
        return ((tr, HEAD_DIM), lambda hh, i: (i, seg * h + hh))

    def halo_b(seg):
        return ((8, HEAD_DIM), lambda hh, i: (jnp.maximum(i * (tr // 8) - 1, 0), seg * h + hh))

    hs = ((tr, HEAD_DIM), lambda hh, i: (i, hh))
    halo_a = ((8, HEAD_DIM), lambda hh, i: (jnp.minimum((i + 1) * (tr // 8), t // 8 - 1), hh))
    wspec = lambda seg: ((8, HEAD_DIM), lambda hh, i: (0, seg * h + hh))
    wout = ((8, d), F32, (8, HEAD_DIM), lambda hh, i: (0, hh), "acc_inner")
    out = ((t, d), BF16) + hs + ("set",)
    res = _rowcall(
        "conv_bwd", body, (h, ni),
        [(proj_b,) + slab(1), (proj_b,) + halo_b(1), (proj_b,) + slab(2), (proj_b,) + halo_b(2),
         (proj_b,) + slab(3), (proj_b,) + halo_b(3),
         (conv_w8,) + wspec(0), (conv_w8,) + wspec(1), (conv_w8,) + wspec(2),
         (du_q,) + hs, (du_q,) + halo_a, (du_k,) + hs, (du_k,) + halo_a, (du_v,) + hs,
         (du_v,) + halo_a],
        [out, out, out, wout, wout, wout])
    return res


def _gdn_post_fwd(o, proj_b, gnw, n_heads):
    t, d = o.shape
    h = n_heads
    tr = _pick(t, TR_HEAD)

    def body(idx, o_v, z, w):
        rstd = lax.rsqrt(jnp.mean(o_v * o_v, axis=1, keepdims=True) + NORM_EPS)
        return ((o_v * rstd) * w * (z * _sigmoid(z)),)

    hs = ((tr, HEAD_DIM), lambda i, hh: (i, hh))
    return _rowcall("gdn_post_fwd", body, (t // tr, h),
                    [(o,) + hs, (proj_b, (tr, HEAD_DIM), lambda i, hh: (i, 4 * h + hh)), _full(gnw)],
                    [((t, d), BF16) + hs + ("set",)])[0]


def _gdn_post_bwd(do_gdn, o, proj_b, gnw, n_heads):
    t, d = o.shape
    h = n_heads
    tr = _pick(t, TR_HEAD)

    def body(idx, dog, o_v, z, w):
        rstd = lax.rsqrt(jnp.mean(o_v * o_v, axis=1, keepdims=True) + NORM_EPS)
        n = o_v * rstd
        sg = _sigmoid(z)
        don = dog * (z * sg)
        dz = dog * (n * w) * (sg * (1.0 + z * (1.0 - sg)))
        dn = don * w
        d_o = rstd * (dn - n * jnp.mean(dn * n, axis=1, keepdims=True))
        return d_o, dz, jnp.sum(don * n, axis=0, keepdims=True)

    hs = ((tr, HEAD_DIM), lambda i, hh: (i, hh))
    return _rowcall("gdn_post_bwd", body, (t // tr, h),
                    [(do_gdn,) + hs, (o,) + hs,
                     (proj_b, (tr, HEAD_DIM), lambda i, hh: (i, 4 * h + hh)), _full(gnw)],
                    [((t, d), F32) + hs + ("set",), ((t, d), BF16) + hs + ("set",),
                     ((1, HEAD_DIM), F32, (1, HEAD_DIM), lambda i, hh: (0, 0), "acc_all")])


def _seg(arr, tr, d, seg):
    return (arr, (tr, d), lambda i: (i, seg))


def _sb_post_fwd(o_raw, proj_b):
    t, d = o_raw.shape
    tr = _pick(t, TR_WIDE)

    def body(idx, o_v, z):
        return (o_v * (z * _sigmoid(z)),)

    return _rowcall("sb_post_fwd", body, (t // tr,), [_seg(o_raw, tr, d, 0), _seg(proj_b, tr, d, 0)],
                    [((t, d), BF16, (tr, d), lambda i: (i, 0), "set")])[0]


def _sb_post_bwd(do_sb, o_raw, proj_b):
    t, d = o_raw.shape
    tr = _pick(t, TR_WIDE)

    def body(idx, dos, o_v, z):
        sg = _sigmoid(z)
        return dos * (z * sg), dos * o_v * (sg * (1.0 + z * (1.0 - sg)))

    out = ((t, d), BF16, (tr, d), lambda i: (i, 0), "set")
    return _rowcall("sb_post_bwd", body, (t // tr,),
                    [_seg(do_sb, tr, d, 0), _seg(o_raw, tr, d, 0), _seg(proj_b, tr, d, 0)],
                    [out, out])


def _merge_fwd(proj_b, p_sb, p_gdn):
    t, d = p_sb.shape
    tr = _pick(t, TR_WIDE)

    def body(idx, m_sb, m_gdn, ps, pg):
        return (_sigmoid(m_sb) * ps + _sigmoid(m_gdn) * pg,)

    return _rowcall("merge_fwd", body, (t // tr,),
                    [_seg(proj_b, tr, d, 5), _seg(proj_b, tr, d, 6), _seg(p_sb, tr, d, 0),
                     _seg(p_gdn, tr, d, 0)],
                    [((t, d), BF16, (tr, d), lambda i: (i, 0), "set")])[0]


def _merge_bwd(dy, proj_b, p_sb, p_gdn):
    t, d = p_sb.shape
    tr = _pick(t, TR_WIDE)

    def body(idx, dy_v, m_sb, m_gdn, ps, pg):
        s1 = _sigmoid(m_sb)
        s2 = _sigmoid(m_gdn)
        return s1 * dy_v, s2 * dy_v, dy_v * ps * (s1 * (1.0 - s1)), dy_v * pg * (s2 * (1.0 - s2))

    out = ((t, d), BF16, (tr, d), lambda i: (i, 0), "set")
    return _rowcall("merge_bwd", body, (t // tr,),
                    [_seg(dy, tr, d, 0), _seg(proj_b, tr, d, 5), _seg(proj_b, tr, d, 6),
                     _seg(p_sb, tr, d, 0), _seg(p_gdn, tr, d, 0)],
                    [out] * 4)


def _tail(x, r, target, gate, final_w):
    t, d = x.shape
    tr = _pick(t, TR_WIDE)

    def body(idx, x_v, r_v, tg, gt, fw):
        x2 = x_v + gt * r_v
        rstd = lax.rsqrt(jnp.mean(x2 * x2, axis=1, keepdims=True) + NORM_EPS)
        n = x2 * rstd
        diff = n * fw - tg
        loss = 0.5 * jnp.sum(jnp.mean(diff * diff, axis=1, keepdims=True), axis=0, keepdims=True)
        dout = diff * (1.0 / d)
        dn = dout * fw
        dx2 = rstd * (dn - n * jnp.mean(dn * n, axis=1, keepdims=True))
        return (dx2, gt * dx2, jnp.sum(dout * n, axis=0, keepdims=True),
                jnp.sum(dx2 * r_v, axis=0, keepdims=True), jnp.broadcast_to(loss, (1, LANES)))

    rb = ((tr, d), lambda i: (i, 0))
    vec = ((1, d), F32, (1, d), lambda i: (0, 0), "acc_all")
    return _rowcall("tail", body, (t // tr,),
                    [(x,) + rb, (r,) + rb, (target,) + rb, _full(gate), _full(final_w)],
                    [((t, d), F32) + rb + ("set",), ((t, d), BF16) + rb + ("set",), vec, vec,
                     ((1, LANES), F32, (1, LANES), lambda i: (0, 0), "acc_all")])


def _pad_to(a, rows, cols):
    return jnp.pad(a, ((0, rows - a.shape[0]), (0, cols - a.shape[1])))


def kernel(x, c, w_ada, b_ada, norm_w, w_in, gdn_conv_w, gdn_a_log, gdn_dt_bias, gdn_norm_w, w_proj_sb, w_proj_gdn, w_out, final_norm_w, loss_target, m_w_ada, m_b_ada, m_norm_w, m_w_in, m_gdn_conv_w, m_gdn_a_log, m_gdn_dt_bias, m_gdn_norm_w, m_w_proj_sb, m_w_proj_gdn, m_w_out, m_final_norm_w, v_w_ada, v_b_ada, v_norm_w, v_w_in, v_gdn_conv_w, v_gdn_a_log, v_gdn_dt_bias, v_gdn_norm_w, v_w_proj_sb, v_w_proj_gdn, v_w_out, v_final_norm_w):
    t, d = x.shape[1], x.shape[2]
    h = d // HEAD_DIM
    me = 4 * lax.axis_index("x") + 2 * lax.axis_index("y") + lax.axis_index("c")
    x2d = x[0]
    tgt = loss_target[0]
    ada_cols = w_ada.shape[2]
    in_cols = w_in.shape[2]
    rows_p = w_out.shape[1]

    w_in_all = _gather_two_level("gather_w_in", w_in[0].astype(BF16))
    w_in_full = jnp.transpose(w_in_all, (1, 0, 2)).reshape(d, N_DEV * in_cols)
    w_main = jnp.concatenate([w_in_full[:, :8 * d], w_in_full[:, 8 * d + 2 * h:]], axis=1)
    w_g = _pad_to(w_in_full[:, 8 * d:8 * d + 2 * h], d, LANES)
    w_main_t = w_main.T
    w_g_t = w_g.T
    w_sq = jnp.stack([w_proj_sb[0], w_proj_gdn[0], w_out[0]]).astype(BF16)
    conv_all = _exchange("gather_conv", _pad_to(gdn_conv_w[0], 8, gdn_conv_w.shape[2]), True)
    conv_w8 = jnp.transpose(conv_all, (1, 0, 2)).reshape(8, 3 * d)
    c_all = _exchange("gather_c", _pad_to(c, 8, d), True)[:, 0, :]

    sc_all = c_all * _sigmoid(c_all)
    mod_part = _matmul("ada_fwd", _pad_to(sc_all, 16, d).astype(BF16), w_ada[0].astype(BF16), F32)
    mod_part = mod_part[:N_DEV] + lax.dynamic_slice(b_ada, (0, me * ada_cols), (1, ada_cols))
    mod_rows = _exchange("a2a_mod", _pad_to(mod_part, 8, ada_cols).reshape(N_DEV, 1, ada_cols)
                         * jnp.ones((1, 8, 1), F32), False)
    mod = mod_rows[:, 0, :].reshape(1, 3 * d)
    shift, scale, gate = mod[:, :d], mod[:, d:2 * d], mod[:, 2 * d:]

    hmod = _norm_mod_fwd(x2d, shift, scale, norm_w)
    proj_a = _matmul("in_proj_a", hmod, w_main, BF16, n_cols=3 * d, col0=0)
    proj_b, w_sq_all = _matmul("in_proj_b", hmod, w_main, F32, n_cols=7 * d, col0=3 * d,
                               xchg=w_sq, gather=True)
    w_sq_full = jnp.transpose(w_sq_all, (1, 0, 2, 3)).reshape(3, d, d)
    wp_sb, wp_gdn, wo = w_sq_full[0], w_sq_full[1], w_sq_full[2]
    proj_c = _matmul("in_proj_c", hmod, w_g, F32)
    o_sb_raw, ctot, sb_visited = _sb_fwd(proj_a, h)
    o_sb = _sb_post_fwd(o_sb_raw, proj_b)
    ab = _pad_to(jnp.concatenate([gdn_a_log, gdn_dt_bias], axis=0), 8, LANES)
    gq, gk, gv, g_beta, g_cum = _gdn_pre_fwd(proj_b, proj_c, conv_w8, ab, h)
    g_cum_t = jnp.broadcast_to(g_cum[:, ::HEAD_DIM].T[:, None, :], (h, 8, t))
    o_gdn_raw, states, tms = _gdn_fwd(gq, gk, gv, g_beta, g_cum, g_cum_t, h)
    o_gdn = _gdn_post_fwd(o_gdn_raw, proj_b, gdn_norm_w, h)
    p_sb = _matmul("proj_sb", o_sb, wp_sb, F32)
    p_gdn = _matmul("proj_gdn", o_gdn, wp_gdn, F32)
    y = _merge_fwd(proj_b, p_sb, p_gdn)
    r = _matmul("out_proj", y, wo, F32)
    dx2, dr, d_final_w, d_gate, loss_part = _tail(x2d, r, tgt, gate, final_norm_w.reshape(1, d))

    dy = _matmul("d_out_proj", dr, wo.T, F32)
    dw_out = _matmul("dw_out", y.T, dr, BF16)
    dp_sb, dp_gdn, dm_sb, dm_gdn = _merge_bwd(dy, proj_b, p_sb, p_gdn)
    do_sb = _matmul("d_proj_sb", dp_sb, wp_sb.T, F32)
    dw_p_sb = _matmul("dw_proj_sb", o_sb.T, dp_sb, BF16)
    do_gdn = _matmul("d_proj_gdn", dp_gdn, wp_gdn.T, F32)
    dw_p_gdn = _matmul("dw_proj_gdn", o_gdn.T, dp_gdn, BF16)
    do_sb_raw, d_sbz = _sb_post_bwd(do_sb, o_sb_raw, proj_b)
    d_sbq, d_sbk, d_sbv = _sb_bwd(proj_a, do_sb_raw, ctot, sb_visited, h)
    d_o_gdn, d_gz, d_gnw = _gdn_post_bwd(do_gdn, o_gdn_raw, proj_b, gdn_norm_w, h)
    dgq, dgk, dgv, dgbe, dgcum = _gdn_bwd(gq, gk, gv, g_beta, g_cum, g_cum_t, d_o_gdn, states, tms, h)
    du_q, du_k, du_v, d_gates, d_ab = _gdn_pre_bwd(proj_b, proj_c, conv_w8, ab, dgq, dgk, dgv,
                                                  dgbe, dgcum, h)
    d_gq, d_gk, d_gv, dcw_q, dcw_k, dcw_v = _conv_bwd(proj_b, conv_w8, du_q, du_k, du_v, h)
    dproj = jnp.concatenate([d_sbq, d_sbk, d_sbv, d_sbz, d_gq, d_gk, d_gv, d_gz, dm_sb, dm_gdn],
                            axis=1)
    d_gates_b = d_gates.astype(BF16)
    hmod_t = hmod.T
    dw_sq = jnp.stack([dw_p_sb, dw_p_gdn, dw_out]).reshape(3, N_DEV, rows_p, d)
    dw_main, dw_sq_parts = _matmul("dw_in", hmod_t, dproj, BF16, tk=BIG_K_TILE,
                                   xchg=jnp.transpose(dw_sq, (1, 0, 2, 3)))
    dw_g = _matmul("dw_in_g", hmod_t, d_gates_b, BF16)
    dw_in_full = jnp.concatenate([dw_main[:, :8 * d], dw_g[:, :2 * h], dw_main[:, 8 * d:]], axis=1)
    dh_a, dw_in_parts = _matmul(
        "d_in_proj", dproj, w_main_t, F32, tk=BIG_K_TILE,
        xchg=jnp.transpose(dw_in_full.reshape(d, N_DEV, in_cols), (1, 0, 2)))
    dh_b = _matmul("d_in_proj_g", d_gates_b, w_g_t, F32)
    grad_x, d_shift, d_scale, d_norm_w = _norm_mod_bwd(dh_a, dh_b, x2d, dx2, scale, norm_w)

    dmod = jnp.concatenate([d_shift, d_scale, d_gate], axis=1)
    small = jnp.concatenate([dmod, d_norm_w, d_final_w, d_ab[:, :h], d_ab[:, h:2 * h], d_gnw,
                             loss_part], axis=1)
    n_small = small.shape[1]
    small_all = _exchange("gather_small", _pad_to(small, 8, n_small), True)[:, 0:1, :]
    small_w = jnp.concatenate([b_ada, norm_w, final_norm_w.reshape(1, d), gdn_a_log, gdn_dt_bias,
                               gdn_norm_w, jnp.zeros((1, LANES), F32)], axis=1)
    small_m = jnp.concatenate([m_b_ada, m_norm_w, m_final_norm_w.reshape(1, d), m_gdn_a_log,
                               m_gdn_dt_bias, m_gdn_norm_w, jnp.zeros((1, LANES), F32)], axis=1)
    small_v = jnp.concatenate([v_b_ada, v_norm_w, v_final_norm_w.reshape(1, d), v_gdn_a_log,
                               v_gdn_dt_bias, v_gdn_norm_w, jnp.ones((1, LANES), F32)], axis=1)
    s_g, s_d, s_m, s_v = _adamw("adamw_small", small_all, small_w, small_m, small_v)
    cuts = [3 * d, 4 * d, 5 * d, 5 * d + h, 5 * d + 2 * h, 5 * d + 2 * h + HEAD_DIM]

    def split_small(a):
        b, nw, fw, al, dtb, gn, _ = jnp.split(a, cuts, axis=1)
        return b, nw, fw.reshape(d), al, dtb, gn

    loss = s_g[0, cuts[-1]]

    dmod_all = small_all[:, 0, :3 * d]
    dmod_mine = lax.dynamic_slice(dmod_all, (0, me * ada_cols), (N_DEV, ada_cols))
    dw_ada = _matmul("dw_ada", _pad_to(sc_all.T, d, LANES).astype(BF16),
                     _pad_to(dmod_mine, LANES, ada_cols).astype(BF16), F32)
    ada = _adamw("adamw_ada", dw_ada[None], w_ada[0], m_w_ada[0], v_w_ada[0])

    win = _adamw("adamw_w_in", dw_in_parts, w_in[0], m_w_in[0], v_w_in[0])
    sq = _adamw("adamw_sq", dw_sq_parts.reshape(N_DEV, 3 * rows_p, d),
                jnp.concatenate([w_proj_sb[0], w_proj_gdn[0], w_out[0]], axis=0),
                jnp.concatenate([m_w_proj_sb[0], m_w_proj_gdn[0], m_w_out[0]], axis=0),
                jnp.concatenate([v_w_proj_sb[0], v_w_proj_gdn[0], v_w_out[0]], axis=0))
    dcw = jnp.concatenate([dcw_q, dcw_k, dcw_v], axis=1)
    cw_cols = gdn_conv_w.shape[2]
    dcw_parts = _exchange("a2a_dconv",
                          jnp.transpose(dcw.reshape(8, N_DEV, cw_cols), (1, 0, 2)), False)
    cw = _adamw("adamw_conv", dcw_parts, _pad_to(gdn_conv_w[0], 8, cw_cols),
                _pad_to(m_gdn_conv_w[0], 8, cw_cols),
                jnp.pad(v_gdn_conv_w[0], ((0, 8 - GDN_CONV), (0, 0)), constant_values=1.0))

    outs = [loss, grad_x[None]]
    for k_out in range(4):
        b, nw, fw, al, dtb, gn = split_small((s_g, s_d, s_m, s_v)[k_out])
        sq3 = sq[k_out].reshape(3, 1, rows_p, d)
        outs += [ada[k_out][None], b, nw, win[k_out][None], cw[k_out][None, :GDN_CONV], al, dtb, gn,
                 sq3[0], sq3[1], sq3[2], fw]
    return tuple(outs)
```

```python
import jax
import jax.numpy as jnp
from jax import lax
from jax.experimental import pallas as pl
from jax.experimental.pallas import tpu as pltpu

F32 = jnp.float32
BF16 = jnp.bfloat16
N_DEV = 8
HEAD_DIM = 128
LANES = 128
GDN_CHUNK = 64
GDN_CONV = 4
NORM_EPS = 1e-6
L2_EPS = 1e-6
ADAM_LR = 0.001
ADAM_B1 = 0.9
ADAM_B2 = 0.999
ADAM_EPS = 1e-08
ADAM_WD = 0.01
ADAM_STEP = 10
VMEM_LIMIT = 56 * 1024 * 1024
MESH = pl.DeviceIdType.MESH
NT = (((1,), (1,)), ((), ()))
TN = (((0,), (0,)), ((), ()))


def _pick(n, pref):
    t = min(pref, n)
    while n % t:
        t //= 2
    return t


def _sigmoid(x):
    return 1.0 / (1.0 + jnp.exp(-x))


def _bdot(a, b, dims=None):
    a = a.astype(BF16)
    b = b.astype(BF16)
    if dims is None:
        return jnp.dot(a, b, preferred_element_type=F32)
    return lax.dot_general(a, b, dims, preferred_element_type=F32)


def _split2(x):
    hi = x.astype(BF16)
    lo = (x - hi.astype(F32)).astype(BF16)
    return hi, lo


def _split3(x):
    p1 = x.astype(BF16)
    r = x - p1.astype(F32)
    p2 = r.astype(BF16)
    p3 = (r - p2.astype(F32)).astype(BF16)
    return p1, p2, p3


def _dot2(x, c, dims=None):
    hi, lo = _split2(x)
    return _bdot(hi, c, dims) + _bdot(lo, c, dims)


def _cdot3(c, x):
    p1, p2, p3 = _split3(x)
    return _bdot(c, p1) + _bdot(c, p2) + _bdot(c, p3)


def _dot3(a, b, dims=None):
    ah, al = _split2(a)
    bh, bl = _split2(b)
    return _bdot(ah, bh, dims) + (_bdot(ah, bl, dims) + _bdot(al, bh, dims))


def _dot6(a, b, dims=None):
    a1, a2, a3 = _split3(a)
    b1, b2, b3 = _split3(b)
    small = (_bdot(a1, b3, dims) + _bdot(a3, b1, dims)) + _bdot(a2, b2, dims)
    return _bdot(a1, b1, dims) + ((_bdot(a1, b2, dims) + _bdot(a2, b1, dims)) + small)


def _full(a):
    nd = a.ndim
    return (a, a.shape, lambda *idx: (0,) * nd)


def _rowcall(name, body, grid, ins, outs):
    n_in = len(ins)
    modes = [o[4] for o in outs]
    n_ax = len(grid)

    def kern(*refs):
        idx = tuple(pl.program_id(a) for a in range(n_ax))
        vals = body(idx, *[r[...] for r in refs[:n_in]])
        first_all = idx[0] == 0
        for a in range(1, n_ax):
            first_all = jnp.logical_and(first_all, idx[a] == 0)
        first_inner = idx[-1] == 0
        for r, v, mode in zip(refs[n_in:], vals, modes):
            v = v.astype(r.dtype)
            if mode == "set":
                r[...] = v
            else:
                first = first_all if mode == "acc_all" else first_inner

                @pl.when(first)
                def _(r=r, v=v):
                    r[...] = v

                @pl.when(jnp.logical_not(first))
                def _(r=r, v=v):
                    r[...] += v

    return pl.pallas_call(
        kern, name=name, grid=grid,
        in_specs=[pl.BlockSpec(b, m) for (_, b, m) in ins],
        out_specs=[pl.BlockSpec(o[2], o[3]) for o in outs],
        out_shape=[jax.ShapeDtypeStruct(o[0], o[1]) for o in outs],
        compiler_params=pltpu.CompilerParams(
            dimension_semantics=("arbitrary",) * n_ax, vmem_limit_bytes=VMEM_LIMIT),
    )(*[a for (a, _, _) in ins])


SQUARE_TILES = dict(tm=512, tn=2048)
BIG_K_TILE = 4096

EXCHANGE_SEMS = [pltpu.SemaphoreType.DMA((N_DEV - 1,)), pltpu.SemaphoreType.DMA((N_DEV - 1,)),
                 pltpu.SemaphoreType.DMA]


def _exchange_shape(x, gather):
    return jax.ShapeDtypeStruct((N_DEV,) + tuple(x.shape if gather else x.shape[1:]), x.dtype)


def _exchange_copies(x_ref, out_ref, send_sems, recv_sems, local_sem, gather, start, wait):
    xi, yi, ci = lax.axis_index("x"), lax.axis_index("y"), lax.axis_index("c")
    me = 4 * xi + 2 * yi + ci
    mine = pltpu.make_async_copy(x_ref if gather else x_ref.at[me], out_ref.at[me], local_sem)
    if start:
        mine.start()
    for k in range(1, N_DEV):
        kx, ky, kc = (k >> 2) & 1, (k >> 1) & 1, k & 1
        pid = me ^ k
        src = x_ref if gather else x_ref.at[pid]
        if start:
            pltpu.make_async_remote_copy(
                src_ref=src, dst_ref=out_ref.at[me], send_sem=send_sems.at[k - 1],
                recv_sem=recv_sems.at[k - 1], device_id=(xi ^ kx, yi ^ ky, ci ^ kc),
                device_id_type=MESH).start()
        if wait:
            pltpu.make_async_remote_copy(
                src_ref=src, dst_ref=out_ref.at[pid], send_sem=send_sems.at[k - 1],
                recv_sem=recv_sems.at[k - 1], device_id=(xi, yi, ci), device_id_type=MESH).wait()
    if wait:
        mine.wait()


def _matmul(name, a, b, out_dtype, n_cols=None, col0=0, tm=1024, tn=1024, tk=2048, xchg=None,
            gather=False):
    m, k = a.shape
    n = b.shape[1] if n_cols is None else n_cols
    tm = _pick(m, tm)
    tn = _pick(n, tn)
    while col0 % tn:
        tn //= 2
    tk = _pick(k, tk)
    nk = k // tk
    cb = col0 // tn
    grid = (m // tm, n // tn, nk)

    def kern(a_ref, b_ref, *rest):
        if xchg is None:
            o_ref, acc_ref = rest
        else:
            x_ref, o_ref, xo_ref, acc_ref, send_sems, recv_sems, local_sem = rest
            step = (pl.program_id(0) * grid[1] + pl.program_id(1)) * nk + pl.program_id(2)

            @pl.when(step == 0)
            def _():
                _exchange_copies(x_ref, xo_ref, send_sems, recv_sems, local_sem, gather, True, False)

        kk = pl.program_id(2)
        part = jnp.dot(a_ref[...], b_ref[...], preferred_element_type=F32)
        if nk == 1:
            o_ref[...] = part.astype(o_ref.dtype)
        else:
            @pl.when(kk == 0)
            def _():
                acc_ref[...] = part

            @pl.when(kk > 0)
            def _():
                acc_ref[...] += part

            @pl.when(kk == nk - 1)
            def _():
                o_ref[...] = acc_ref[...].astype(o_ref.dtype)

        if xchg is not None:
            @pl.when(step == grid[0] * grid[1] * nk - 1)
            def _():
                _exchange_copies(x_ref, xo_ref, send_sems, recv_sems, local_sem, gather, False, True)

    in_specs = [pl.BlockSpec((tm, tk), lambda i, j, kk: (i, kk)),
                pl.BlockSpec((tk, tn), lambda i, j, kk: (kk, j + cb))]
    out_specs = [pl.BlockSpec((tm, tn), lambda i, j, kk: (i, j))]
    out_shape = [jax.ShapeDtypeStruct((m, n), out_dtype)]
    scratch = [pltpu.VMEM((tm, tn), F32)]
    args = (a, b)
    if xchg is not None:
        in_specs.append(pl.BlockSpec(memory_space=pl.ANY))
        out_specs.append(pl.BlockSpec(memory_space=pl.ANY))
        out_shape.append(_exchange_shape(xchg, gather))
        scratch += EXCHANGE_SEMS
        args = (a, b, xchg)
    res = pl.pallas_call(
        kern, name=name, grid=grid, in_specs=in_specs, out_specs=out_specs, out_shape=out_shape,
        scratch_shapes=scratch,
        compiler_params=pltpu.CompilerParams(
            dimension_semantics=("arbitrary", "arbitrary", "arbitrary"),
            vmem_limit_bytes=VMEM_LIMIT),
    )(*args)
    return res[0] if xchg is None else (res[0], res[1])


def _exchange(name, x, gather):
    def body(x_ref, out_ref, send_sems, recv_sems, local_sem):
        _exchange_copies(x_ref, out_ref, send_sems, recv_sems, local_sem, gather, True, True)

    return pl.pallas_call(
        body, name=name,
        out_shape=_exchange_shape(x, gather),
        in_specs=[pl.BlockSpec(memory_space=pl.ANY)],
        out_specs=pl.BlockSpec(memory_space=pl.ANY),
        scratch_shapes=EXCHANGE_SEMS,
    )(x)


def _gather_two_level(name, x):
    def body(x_ref, out_ref, send_sems, recv_sems, local_sem):
        xi, yi, ci = lax.axis_index("x"), lax.axis_index("y"), lax.axis_index("c")
        me, sibling = (xi, yi, ci), (xi, yi, 1 - ci)
        chips = [(1 - xi, yi), (xi, 1 - yi), (1 - xi, 1 - yi)]

        def slot(px, py, pc):
            return out_ref.at[4 * px + 2 * py + pc]

        def copy(k, block, to, src=None):
            return pltpu.make_async_remote_copy(
                src_ref=slot(*block) if src is None else src, dst_ref=slot(*block),
                send_sem=send_sems.at[k], recv_sem=recv_sems.at[k], device_id=to,
                device_id_type=MESH)

        mine = pltpu.make_async_copy(x_ref, slot(*me), local_sem)
        mine.start()
        first = [copy(0, me, sibling, src=x_ref)]
        first += [copy(1 + j, me, (*chip, ci), src=x_ref) for j, chip in enumerate(chips)]
        for cp in first:
            cp.start()
        passed = [copy(4 + j, (*chip, ci), sibling) for j, chip in enumerate(chips)]
        for j, chip in enumerate(chips):
            copy(1 + j, (*chip, ci), me).wait_recv()
            passed[j].start()
        copy(0, sibling, me).wait_recv()
        for j, chip in enumerate(chips):
            copy(4 + j, (*chip, 1 - ci), me).wait_recv()
        for cp in first + passed:
            cp.wait_send()
        mine.wait()

    return pl.pallas_call(
        body, name=name,
        out_shape=_exchange_shape(x, True),
        in_specs=[pl.BlockSpec(memory_space=pl.ANY)],
        out_specs=pl.BlockSpec(memory_space=pl.ANY),
        scratch_shapes=EXCHANGE_SEMS,
    )(x)


def _adamw(name, parts, w, m, v):
    p, r, c = parts.shape
    tr = r if r <= 64 else _pick(r, 64)

    def body(idx, parts_v, w_v, m_v, v_v):
        g = parts_v[0].astype(F32)
        for s in range(1, p):
            g = g + parts_v[s].astype(F32)
        m2 = ADAM_B1 * m_v + (1.0 - ADAM_B1) * g
        v2 = ADAM_B2 * v_v + (1.0 - ADAM_B2) * (g * g)
        m_hat = m2 / (1.0 - ADAM_B1 ** ADAM_STEP)
        v_hat = v2 / (1.0 - ADAM_B2 ** ADAM_STEP)
        delta = -ADAM_LR * (m_hat / (jnp.sqrt(v_hat) + ADAM_EPS) + ADAM_WD * w_v)
        return g, delta, m2, v2

    rb = ((tr, c), lambda i: (i, 0))
    return _rowcall(
        name, body, (r // tr,),
        [(parts, (p, tr, c), lambda i: (0, i, 0)), (w,) + rb, (m,) + rb, (v,) + rb],
        [((r, c), F32) + rb + ("set",)] * 4)


TR_WIDE = 256


def _norm_mod_fwd(x, shift, scale, norm_w):
    t, d = x.shape
    tr = _pick(t, TR_WIDE)

    def body(idx, x_v, sh, sc, nw):
        rstd = lax.rsqrt(jnp.mean(x_v * x_v, axis=1, keepdims=True) + NORM_EPS)
        return ((x_v * rstd) * nw * (1.0 + sc) + sh,)

    rb = ((tr, d), lambda i: (i, 0))
    return _rowcall("norm_mod_fwd", body, (t // tr,),
                    [(x,) + rb, _full(shift), _full(scale), _full(norm_w)],
                    [((t, d), BF16) + rb + ("set",)])[0]


def _norm_mod_bwd(dh_a, dh_b, x, dx2, scale, norm_w):
    t, d = x.shape
    tr = _pick(t, TR_WIDE)

    def body(idx, dha, dhb, x_v, dx2_v, sc, nw):
        dh = dha + dhb
        rstd = lax.rsqrt(jnp.mean(x_v * x_v, axis=1, keepdims=True) + NORM_EPS)
        xn = x_v * rstd
        m1 = 1.0 + sc
        dxn = dh * nw * m1
        dx = rstd * (dxn - xn * jnp.mean(dxn * xn, axis=1, keepdims=True))
        dhx = dh * xn
        return (dx2_v + dx,
                jnp.sum(dh, axis=0, keepdims=True),
                jnp.sum(dhx * nw, axis=0, keepdims=True),
                jnp.sum(dhx * m1, axis=0, keepdims=True))

    rb = ((tr, d), lambda i: (i, 0))
    vec = ((1, d), F32, (1, d), lambda i: (0, 0), "acc_all")
    return _rowcall("norm_mod_bwd", body, (t // tr,),
                    [(dh_a,) + rb, (dh_b,) + rb, (x,) + rb, (dx2,) + rb, _full(scale), _full(norm_w)],
                    [((t, d), F32) + rb + ("set",), vec, vec, vec])


SB_TILE = 512


SB_SCAN = 256
SB_DEAD = 105.0


def _neg_log_not_beta(z):
    return jnp.maximum(z, 0.0) + jnp.log(1.0 + jnp.exp(-jnp.abs(z)))


def _key_scan(x, tri, later):
    n_blk = x.shape[1] // SB_SCAN
    parts = [x[:, g * SB_SCAN:(g + 1) * SB_SCAN] for g in range(n_blk)]
    sums = [jnp.sum(p, axis=1, keepdims=True) for p in parts]
    outs = []
    for g in range(n_blk):
        o = _dot2(parts[g], tri)
        others = range(g + 1, n_blk) if later else range(g)
        for g2 in others:
            o = o + sums[g2]
        outs.append(o)
    total = sums[0]
    for s in sums[1:]:
        total = total + s
    return (outs[0] if n_blk == 1 else jnp.concatenate(outs, axis=1)), total


def _scan_tri(kind):
    row = lax.broadcasted_iota(jnp.int32, (SB_SCAN, SB_SCAN), 0)
    col = lax.broadcasted_iota(jnp.int32, (SB_SCAN, SB_SCAN), 1)
    return {"after": row > col, "upto": row <= col, "before": row < col}[kind].astype(BF16)


def _sb_fwd(qkv, n_heads):
    t = qkv.shape[0]
    tq = _pick(t, SB_TILE)
    nq = t // tq
    scale = HEAD_DIM ** -0.5

    def kern(q_ref, k_ref, v_ref, o_ref, c_ref, nd_ref):
        i_blk = pl.program_id(1)
        row = lax.broadcasted_iota(jnp.int32, (tq, tq), 0)
        col = lax.broadcasted_iota(jnp.int32, (tq, tq), 1)
        causal = col < row
        after = _scan_tri("after")
        qb = q_ref[...]

        def tile(j_blk, c, acc, diag):
            r0 = pl.multiple_of(j_blk * tq, tq)
            kb = k_ref[pl.ds(r0, tq), :]
            vb = v_ref[pl.ds(r0, tq), :]
            z = lax.dot_general(qb, kb, NT, preferred_element_type=F32) * scale
            n = _neg_log_not_beta(z)
            if diag:
                n = jnp.where(causal, n, 0.0)
            later, total = _key_scan(n, after, True)
            a = jnp.exp(z - (n + later + c))
            if diag:
                a = jnp.where(causal, a, 0.0)
            acc = acc + jnp.dot(a.astype(BF16), vb, preferred_element_type=F32)
            return c + total, acc

        c, acc = tile(i_blk, jnp.zeros((tq, 1), F32), jnp.zeros((tq, HEAD_DIM), F32), True)

        def more(st):
            return jnp.logical_and(st[0] < i_blk, jnp.min(st[1]) <= SB_DEAD)

        def step(st):
            c2, acc2 = tile(i_blk - 1 - st[0], st[1], st[2], False)
            return st[0] + 1, c2, acc2

        n_done, c, acc = lax.while_loop(more, step, (jnp.int32(0), c, acc))
        o_ref[...] = acc
        c_ref[...] = jnp.broadcast_to(c, (tq, HEAD_DIM))
        nd_ref[...] = jnp.full((8, LANES), n_done.astype(F32))

    h = n_heads
    return pl.pallas_call(
        kern, name="sb_fwd", grid=(h, nq),
        in_specs=[pl.BlockSpec((tq, HEAD_DIM), lambda hh, i: (i, hh)),
                  pl.BlockSpec((t, HEAD_DIM), lambda hh, i: (0, h + hh)),
                  pl.BlockSpec((t, HEAD_DIM), lambda hh, i: (0, 2 * h + hh))],
        out_specs=[pl.BlockSpec((tq, HEAD_DIM), lambda hh, i: (i, hh)),
                   pl.BlockSpec((tq, HEAD_DIM), lambda hh, i: (i, hh)),
                   pl.BlockSpec((None, None, 8, LANES), lambda hh, i: (hh, i, 0, 0))],
        out_shape=[jax.ShapeDtypeStruct((t, h * HEAD_DIM), F32),
                   jax.ShapeDtypeStruct((t, h * HEAD_DIM), F32),
                   jax.ShapeDtypeStruct((h, nq, 8, LANES), F32)],
        compiler_params=pltpu.CompilerParams(
            dimension_semantics=("arbitrary", "arbitrary"), vmem_limit_bytes=VMEM_LIMIT),
    )(qkv, qkv, qkv)


def _sb_bwd(qkv, do, ctot, n_visited, n_heads):
    t = qkv.shape[0]
    tq = _pick(t, SB_TILE)
    nq = t // tq
    scale = HEAD_DIM ** -0.5

    def kern(q_ref, do_ref, c_ref, nd_ref, k_ref, v_ref, dq_ref, dk_ref, dv_ref, dk_acc, dv_acc):
        i_blk = pl.program_id(1)
        row = lax.broadcasted_iota(jnp.int32, (tq, tq), 0)
        col = lax.broadcasted_iota(jnp.int32, (tq, tq), 1)
        causal = col < row
        upto = _scan_tri("upto")
        before = _scan_tri("before")
        qb = q_ref[...]
        dob = do_ref[...]
        ctot_v = c_ref[...][:, 0:1]

        @pl.when(i_blk == 0)
        def _():
            dk_acc[...] = jnp.zeros_like(dk_acc)
            dv_acc[...] = jnp.zeros_like(dv_acc)

        def tile(j_blk, cl, pe, dq, diag):
            r0 = pl.multiple_of(j_blk * tq, tq)
            kb = k_ref[pl.ds(r0, tq), :]
            vb = v_ref[pl.ds(r0, tq), :]
            z = lax.dot_general(qb, kb, NT, preferred_element_type=F32) * scale
            n = _neg_log_not_beta(z)
            if diag:
                n = jnp.where(causal, n, 0.0)
            upto_s, n_total = _key_scan(n, upto, False)
            lb = z - n
            a = jnp.exp(lb - ((ctot_v - cl) - upto_s))
            if diag:
                a = jnp.where(causal, a, 0.0)
            da = lax.dot_general(dob, vb, NT, preferred_element_type=F32)
            e = da * a
            before_s, e_total = _key_scan(e, before, False)
            beta = jnp.exp(lb)
            dz = (e - beta * (e + (pe + before_s))) * scale
            if diag:
                dz = jnp.where(causal, dz, 0.0)
            dzb = dz.astype(BF16)
            dq = dq + jnp.dot(dzb, kb, preferred_element_type=F32)
            dk_acc[pl.ds(r0, tq), :] += lax.dot_general(dzb, qb, TN, preferred_element_type=F32)
            dv_acc[pl.ds(r0, tq), :] += lax.dot_general(a.astype(BF16), dob, TN,
                                                        preferred_element_type=F32)
            return cl + n_total, pe + e_total, dq

        zero = jnp.zeros((tq, 1), F32)
        first = i_blk - jnp.max(nd_ref[...]).astype(jnp.int32)
        cl, pe, dq = lax.fori_loop(first, i_blk, lambda j, s: tile(j, s[0], s[1], s[2], False),
                                   (zero, zero, jnp.zeros((tq, HEAD_DIM), F32)))
        cl, pe, dq = tile(i_blk, cl, pe, dq, True)
        dq_ref[...] = dq.astype(BF16)

        @pl.when(i_blk == nq - 1)
        def _():
            dk_ref[...] = dk_acc[...].astype(BF16)
            dv_ref[...] = dv_acc[...].astype(BF16)

    h = n_heads
    qspec = pl.BlockSpec((tq, HEAD_DIM), lambda hh, i: (i, hh))
    hspec = pl.BlockSpec((t, HEAD_DIM), lambda hh, i: (0, hh))
    out = jax.ShapeDtypeStruct((t, h * HEAD_DIM), BF16)
    return pl.pallas_call(
        kern, name="sb_bwd", grid=(h, nq),
        in_specs=[qspec, qspec, qspec,
                  pl.BlockSpec((None, None, 8, LANES), lambda hh, i: (hh, i, 0, 0)),
                  pl.BlockSpec((t, HEAD_DIM), lambda hh, i: (0, h + hh)),
                  pl.BlockSpec((t, HEAD_DIM), lambda hh, i: (0, 2 * h + hh))],
        out_specs=[qspec, hspec, hspec],
        out_shape=[out, out, out],
        scratch_shapes=[pltpu.VMEM((t, HEAD_DIM), F32), pltpu.VMEM((t, HEAD_DIM), F32)],
        compiler_params=pltpu.CompilerParams(
            dimension_semantics=("arbitrary", "arbitrary"), vmem_limit_bytes=VMEM_LIMIT),
    )(qkv, do, ctot, n_visited, qkv, qkv)


GDN_ROWS = 512
GDN_HEADS_PER_STEP = 2
TR_HEAD = 512


def _shift_rows(cur, halo, k, back):
    n = cur.shape[0]
    ext = jnp.concatenate([cur, halo], axis=0)
    return pltpu.roll(ext, k if back else n + 8 - k, 0)[:n]


def _conv_fwd(cur, halo, w):
    out = cur * w[GDN_CONV - 1:GDN_CONV, :]
    for i in range(GDN_CONV - 1):
        out = out + _shift_rows(cur, halo, GDN_CONV - 1 - i, True) * w[i:i + 1, :]
    return out


def _chunk_tri(n, upper):
    row = lax.broadcasted_iota(jnp.int32, (n, n), 0)
    col = lax.broadcasted_iota(jnp.int32, (n, n), 1)
    same = (row // GDN_CHUNK) == (col // GDN_CHUNK)
    tri = (col >= row) if upper else (col <= row)
    return jnp.logical_and(same, tri).astype(BF16)


def _lane_pick(x, lane):
    idx = lax.broadcasted_iota(jnp.int32, x.shape, 1)
    return jnp.sum(jnp.where(idx == lane, x, 0.0), axis=1, keepdims=True)


def _softplus(x):
    y = jnp.exp(-jnp.abs(x))
    u = 1.0 + y
    log1p = jnp.where(u == 1.0, y, jnp.log(u) * (y / jnp.where(u == 1.0, 1.0, u - 1.0)))
    return jnp.maximum(x, 0.0) + log1p


def _gdn_specs(t, tr, h, proj_seg0):
    def slab(seg):
        return ((tr, HEAD_DIM), lambda i, hh: (i, seg * h + hh))

    def halo_before(seg):
        return ((8, HEAD_DIM), lambda i, hh: (jnp.maximum(i * (tr // 8) - 1, 0), seg * h + hh))

    return slab, halo_before


def _gdn_pre_fwd(proj_b, proj_c, conv_w8, ab, n_heads):
    t = proj_b.shape[0]
    h = n_heads
    d = h * HEAD_DIM
    tr = _pick(t, TR_HEAD)
    slab, halo = _gdn_specs(t, tr, h, 1)
    scale = HEAD_DIM ** -0.5

    def body(idx, q_c, q_h, k_c, k_h, v_c, v_h, wq, wk, wv, gbga, ab_v):
        i, hh = idx
        live = (i > 0).astype(F32)
        outs = []
        for cur, hal, w, kind in ((q_c, q_h, wq, "q"), (k_c, k_h, wk, "k"), (v_c, v_h, wv, "v")):
            u = _conv_fwd(cur, hal * live, w)
            s = u * _sigmoid(u)
            if kind != "v":
                s = s * lax.rsqrt(jnp.sum(s * s, axis=1, keepdims=True) + L2_EPS)
            if kind == "q":
                s = s * scale
            outs.append(s)
        beta = _sigmoid(_lane_pick(gbga, hh))
        a_log = _lane_pick(ab_v[0:1, :], hh)
        dt = _lane_pick(ab_v[1:2, :], hh)
        g = -jnp.exp(a_log) * _softplus(_lane_pick(gbga, h + hh) + dt)
        g_rep = jnp.broadcast_to(g, (tr, HEAD_DIM))
        big_g = _cdot3(_chunk_tri(tr, False), g_rep)
        return outs + [jnp.broadcast_to(beta, (tr, HEAD_DIM)), big_g]

    wspec = lambda seg: ((8, HEAD_DIM), lambda i, hh: (0, seg * h + hh))
    out = ((t, d), F32, (tr, HEAD_DIM), lambda i, hh: (i, hh), "set")
    return _rowcall(
        "gdn_pre_fwd", body, (t // tr, h),
        [(proj_b,) + slab(1), (proj_b,) + halo(1), (proj_b,) + slab(2), (proj_b,) + halo(2),
         (proj_b,) + slab(3), (proj_b,) + halo(3),
         (conv_w8,) + wspec(0), (conv_w8,) + wspec(1), (conv_w8,) + wspec(2),
         (proj_c, (tr, LANES), lambda i, hh: (i, 0)), _full(ab)],
        [out] * 5)


def _gdn_consts():
    row = lax.broadcasted_iota(jnp.int32, (GDN_CHUNK, GDN_CHUNK), 0)
    col = lax.broadcasted_iota(jnp.int32, (GDN_CHUNK, GDN_CHUNK), 1)
    lane = lax.broadcasted_iota(jnp.int32, (GDN_CHUNK, HEAD_DIM), 1)
    return row > col, row >= col, (row == col).astype(F32), (lane == 0).astype(BF16)


def _gdn_local(q, k, v, be, ge, g_lanes, consts):
    lower, tril, eye, sel = consts
    kb_ = k * be
    vb_ = v * be
    e_g = jnp.exp(ge)
    kg = kb_ * e_g
    g_i = ge[:, :GDN_CHUNK]
    g_j = jnp.broadcast_to(g_lanes, (GDN_CHUNK, GDN_CHUNK))
    dec = jnp.where(tril, jnp.exp(jnp.minimum(g_i - g_j, 0.0)), 0.0)
    kk = _bdot(kb_, k, NT)
    qk = _bdot(q, k, NT)
    g_last = jnp.min(ge, axis=0, keepdims=True)
    kdec_f = jnp.exp(g_last - ge)
    return dict(kb=kb_, vb=vb_, e_g=e_g, kg=kg, dec=dec, kk=kk, qk=qk, kdec_f=kdec_f,
                k_dec=k * kdec_f, q_dec=q * e_g, gamma=jnp.exp(g_last),
                intra=jnp.where(tril, qk * dec, 0.0))


def _wy_lower_t(k, be, ge, g_lanes):
    row = lax.broadcasted_iota(jnp.int32, (GDN_CHUNK, GDN_CHUNK), 0)
    col = lax.broadcasted_iota(jnp.int32, (GDN_CHUNK, GDN_CHUNK), 1)
    g_row = ge[:, :GDN_CHUNK]
    g_col = jnp.broadcast_to(g_lanes, (GDN_CHUNK, GDN_CHUNK))
    dec_t = jnp.exp(jnp.minimum(g_col - g_row, 0.0))
    return jnp.where(col > row, _bdot(k, k * be, NT) * dec_t, 0.0)


def _unit_lower_inverse(lw_t):
    n = lw_t.shape[0]
    n_slab = GDN_CHUNK // 8
    row = lax.broadcasted_iota(jnp.int32, (n, 8, GDN_CHUNK), 1)
    col = lax.broadcasted_iota(jnp.int32, (n, 8, GDN_CHUNK), 2)
    unit = lax.broadcasted_iota(jnp.int32, (1, 1, GDN_CHUNK), 2)
    lw = [lw_t[:, 8 * g:8 * g + 8, :] for g in range(n_slab)]
    inv = [(col == row + 8 * g).astype(F32) for g in range(n_slab)]
    for i in range(1, GDN_CHUNK):
        acc = lw[0][:, :, i:i + 1] * inv[0]
        for g in range(1, (i - 1) // 8 + 1):
            acc = acc + lw[g][:, :, i:i + 1] * inv[g]
        new_row = (unit == i).astype(F32) - jnp.sum(acc, axis=1, keepdims=True)
        inv[i // 8] = jnp.where(row == i % 8, new_row, inv[i // 8])
    return jnp.concatenate(inv, axis=1)


def _gdn_fwd(q, k, v, be, ge, ge_t, n_heads):
    t = q.shape[0]
    h = n_heads
    hg = GDN_HEADS_PER_STEP
    tr = _pick(t, GDN_ROWS)
    nc = tr // GDN_CHUNK

    def kern(q_ref, k_ref, v_ref, b_ref, g_ref, gt_ref, o_ref, s_ref, tm_ref, state):
        consts = _gdn_consts()
        lower, tril, eye, sel = consts
        lanes = [pl.ds(hs * HEAD_DIM, HEAD_DIM) for hs in range(hg)]

        @pl.when(pl.program_id(1) == 0)
        def _():
            state[...] = jnp.zeros_like(state)

        lw_t = []
        for hs in range(hg):
            for ci in range(nc):
                rows = pl.ds(ci * GDN_CHUNK, GDN_CHUNK)
                lw_t.append(_wy_lower_t(k_ref[rows, lanes[hs]], b_ref[rows, lanes[hs]],
                                        g_ref[rows, lanes[hs]], gt_ref[hs, 0:1, rows]))
        t_all = _unit_lower_inverse(jnp.stack(lw_t))
        for hs in range(hg):
            for ci in range(nc):
                tm_ref[hs, pl.ds(ci * GDN_CHUNK, GDN_CHUNK), :] = t_all[hs * nc + ci]

        local = {}
        for ci in range(nc):
            rows = pl.ds(ci * GDN_CHUNK, GDN_CHUNK)
            for hs in range(hg):
                ln = lanes[hs]
                loc = _gdn_local(q_ref[rows, ln], k_ref[rows, ln], v_ref[rows, ln], b_ref[rows, ln],
                                 g_ref[rows, ln], gt_ref[hs, 0:1, rows], consts)
                t_mat = t_all[hs * nc + ci]
                local[ci, hs] = (_bdot(t_mat, loc["vb"]), _bdot(t_mat, loc["kg"]).astype(BF16),
                                 loc["q_dec"].astype(BF16), loc["intra"].astype(BF16),
                                 loc["k_dec"].astype(BF16), loc["gamma"])
        states = [state[hs] for hs in range(hg)]
        for ci in range(nc):
            rows = pl.ds(ci * GDN_CHUNK, GDN_CHUNK)
            for hs in range(hg):
                u, w, q_dec, intra, k_dec, gamma = local[ci, hs]
                s = states[hs]
                s_ref[hs, ci] = s
                v_new = u - _bdot(w, s)
                o_ref[rows, lanes[hs]] = _bdot(q_dec, s) + _bdot(intra, v_new)
                states[hs] = s * gamma + _bdot(k_dec, v_new, TN)
        for hs in range(hg):
            state[hs] = states[hs]

    slab = pl.BlockSpec((tr, hg * HEAD_DIM), lambda hp, j: (j, hp))
    return pl.pallas_call(
        kern, name="gdn_fwd", grid=(h // hg, t // tr),
        in_specs=[slab] * 5 + [pl.BlockSpec((hg, 8, tr), lambda hp, j: (hp, 0, j))],
        out_specs=[slab,
                   pl.BlockSpec((hg, nc, HEAD_DIM, HEAD_DIM), lambda hp, j: (hp, j, 0, 0)),
                   pl.BlockSpec((hg, tr, GDN_CHUNK), lambda hp, j: (hp, j, 0))],
        out_shape=[jax.ShapeDtypeStruct((t, h * HEAD_DIM), F32),
                   jax.ShapeDtypeStruct((h, t // GDN_CHUNK, HEAD_DIM, HEAD_DIM), F32),
                   jax.ShapeDtypeStruct((h, t, GDN_CHUNK), F32)],
        scratch_shapes=[pltpu.VMEM((hg, HEAD_DIM, HEAD_DIM), F32)],
        compiler_params=pltpu.CompilerParams(
            dimension_semantics=("arbitrary", "arbitrary"), vmem_limit_bytes=VMEM_LIMIT),
    )(q, k, v, be, ge, ge_t)


def _gdn_bwd(q, k, v, be, ge, ge_t, do, states, tms, n_heads):
    t = q.shape[0]
    h = n_heads
    hg = GDN_HEADS_PER_STEP
    tr = _pick(t, GDN_ROWS)
    nc = tr // GDN_CHUNK
    nj = t // tr

    def kern(q_ref, k_ref, v_ref, b_ref, g_ref, gt_ref, do_ref, s_ref, tm_ref,
             dq_ref, dk_ref, dv_ref, db_ref, dg_ref, dstate):
        consts = _gdn_consts()
        lower, tril, eye, sel = consts
        last_row = lax.broadcasted_iota(jnp.int32, (GDN_CHUNK, HEAD_DIM), 0) == GDN_CHUNK - 1

        @pl.when(pl.program_id(1) == 0)
        def _():
            dstate[...] = jnp.zeros_like(dstate)

        def lsum(x):
            return jnp.sum(x, axis=1, keepdims=True)

        def before(ci, hs):
            rows = pl.ds(ci * GDN_CHUNK, GDN_CHUNK)
            ln = pl.ds(hs * HEAD_DIM, HEAD_DIM)
            qv, kv, vv, bev = q_ref[rows, ln], k_ref[rows, ln], v_ref[rows, ln], b_ref[rows, ln]
            loc = _gdn_local(qv, kv, vv, bev, g_ref[rows, ln], gt_ref[hs, 0:1, rows], consts)
            t_mat = tm_ref[hs, rows, :]
            s = s_ref[hs, ci]
            d_o = do_ref[rows, ln]
            w = _bdot(t_mat, loc["kg"])
            v_new = _bdot(t_mat, loc["vb"]) - _bdot(w, s)
            return dict(loc=loc, qv=qv, kv=kv, vv=vv, bev=bev, t_mat=t_mat, s=s, w=w, v_new=v_new,
                        dv_new0=_bdot(loc["intra"], d_o, TN), ds0=_bdot(loc["q_dec"], d_o, TN),
                        d_intra=jnp.where(tril, _bdot(d_o, v_new, NT), 0.0),
                        dq_dec=_bdot(d_o, s, NT))

        def recur(c, d_s):
            loc = c["loc"]
            dv_new = c["dv_new0"] + _bdot(loc["k_dec"], d_s)
            c.update(dv_new=dv_new, dk_dec=_bdot(c["v_new"], d_s, NT),
                     dgamma=jnp.sum(lsum(d_s * c["s"]), axis=0, keepdims=True))
            return c["ds0"] + loc["gamma"] * d_s - _bdot(c["w"], dv_new, TN)

        def after(c):
            loc, qv, kv, vv, bev, t_mat, s = (c[n] for n in ("loc", "qv", "kv", "vv", "bev", "t_mat",
                                                             "s"))
            dv_new, dk_dec, dgamma, d_intra, dq_dec = (c[n] for n in ("dv_new", "dk_dec", "dgamma",
                                                                      "d_intra", "dq_dec"))
            dw = -_bdot(dv_new, s, NT)
            dtm = _bdot(dv_new, loc["vb"], NT) + _bdot(dw, loc["kg"], NT)
            dvb = _bdot(t_mat, dv_new, TN)
            dkg = _bdot(t_mat, dw, TN)
            dlw = jnp.where(lower, -_dot3(t_mat, _dot3(dtm, t_mat, NT), TN), 0.0)
            dkk = dlw * loc["dec"]
            dqk = d_intra * loc["dec"]
            ddec = dlw * loc["kk"] + d_intra * loc["qk"]
            dkb = _bdot(dkk, kv) + dkg * loc["e_g"]
            dk = (_bdot(dkk, loc["kb"], TN) + _bdot(dqk, qv, TN) + dk_dec * loc["kdec_f"]
                  + dkb * bev)
            dq = _bdot(dqk, kv) + dq_dec * loc["e_g"]
            dgd = ddec * loc["dec"]
            r_kdec = lsum(dk_dec * loc["k_dec"])
            col_sums = lsum(_dot2(dgd, sel, TN))
            d_big_g = (lsum(dgd) - col_sums
                       + lsum(dq_dec * loc["q_dec"]) - r_kdec + lsum(dkg * loc["kg"]))
            d_last = jnp.sum(r_kdec, axis=0, keepdims=True) + dgamma * loc["gamma"][:, 0:1]
            d_big_g = jnp.broadcast_to(d_big_g, (GDN_CHUNK, HEAD_DIM)) + jnp.where(last_row, d_last,
                                                                                  0.0)
            d_beta = jnp.broadcast_to(lsum(dkb * kv) + lsum(dvb * vv), (GDN_CHUNK, HEAD_DIM))
            return dq, dk, dvb * bev, d_beta, d_big_g

        work = {(ci, hs): before(ci, hs) for ci in range(nc) for hs in range(hg)}
        d_states = [dstate[hs] for hs in range(hg)]
        for ci in range(nc - 1, -1, -1):
            for hs in range(hg):
                d_states[hs] = recur(work[ci, hs], d_states[hs])
        for hs in range(hg):
            dstate[hs] = d_states[hs]
        for ci in range(nc):
            rows = pl.ds(ci * GDN_CHUNK, GDN_CHUNK)
            for hs in range(hg):
                ln = pl.ds(hs * HEAD_DIM, HEAD_DIM)
                for ref, val in zip((dq_ref, dk_ref, dv_ref, db_ref, dg_ref), after(work[ci, hs])):
                    ref[rows, ln] = val

    slab = pl.BlockSpec((tr, hg * HEAD_DIM), lambda hp, j: (nj - 1 - j, hp))
    out = jax.ShapeDtypeStruct((t, h * HEAD_DIM), F32)
    return pl.pallas_call(
        kern, name="gdn_bwd", grid=(h // hg, nj),
        in_specs=[slab] * 5 + [pl.BlockSpec((hg, 8, tr), lambda hp, j: (hp, 0, nj - 1 - j)), slab] + [
            pl.BlockSpec((hg, nc, HEAD_DIM, HEAD_DIM), lambda hp, j: (hp, nj - 1 - j, 0, 0)),
            pl.BlockSpec((hg, tr, GDN_CHUNK), lambda hp, j: (hp, nj - 1 - j, 0))],
        out_specs=[slab] * 5,
        out_shape=[out] * 5,
        scratch_shapes=[pltpu.VMEM((hg, HEAD_DIM, HEAD_DIM), F32)],
        compiler_params=pltpu.CompilerParams(
            dimension_semantics=("arbitrary", "arbitrary"), vmem_limit_bytes=VMEM_LIMIT),
    )(q, k, v, be, ge, ge_t, do, states, tms)


def _gdn_pre_bwd(proj_b, proj_c, conv_w8, ab, dq, dk, dv, dbe, dge, n_heads):
    t = proj_b.shape[0]
    h = n_heads
    d = h * HEAD_DIM
    tr = _pick(t, TR_HEAD)
    slab, halo = _gdn_specs(t, tr, h, 1)
    scale = HEAD_DIM ** -0.5

    def body(idx, q_c, q_h, k_c, k_h, v_c, v_h, wq, wk, wv, gbga, ab_v, dq_v, dk_v, dv_v, dbe_v,
             dge_v):
        i, hh = idx
        live = (i > 0).astype(F32)
        outs = []
        for cur, hal, w, dy, kind in ((q_c, q_h, wq, dq_v, "q"), (k_c, k_h, wk, dk_v, "k"),
                                      (v_c, v_h, wv, dv_v, "v")):
            u = _conv_fwd(cur, hal * live, w)
            sg = _sigmoid(u)
            if kind == "v":
                ds = dy
            else:
                s = u * sg
                r = lax.rsqrt(jnp.sum(s * s, axis=1, keepdims=True) + L2_EPS)
                y = s * r
                if kind == "q":
                    dy = dy * scale
                ds = r * (dy - y * jnp.sum(dy * y, axis=1, keepdims=True))
            outs.append(ds * (sg * (1.0 + u * (1.0 - sg))))
        lane = lax.broadcasted_iota(jnp.int32, (tr, LANES), 1)
        lane1 = lax.broadcasted_iota(jnp.int32, (1, LANES), 1)
        beta = _sigmoid(_lane_pick(gbga, hh))
        a_neg = -jnp.exp(_lane_pick(ab_v[0:1, :], hh))
        xg = _lane_pick(gbga, h + hh) + _lane_pick(ab_v[1:2, :], hh)
        g = a_neg * _softplus(xg)
        dgb = dbe_v * (beta * (1.0 - beta))
        dg = _cdot3(_chunk_tri(tr, True), dge_v)
        dga = dg * (a_neg * _sigmoid(xg))
        d_gates = jnp.where(lane == hh, dgb, 0.0) + jnp.where(lane == h + hh, dga, 0.0)
        d_ab = (jnp.where(lane1 == hh, jnp.sum(dg * g, axis=0, keepdims=True), 0.0)
                + jnp.where(lane1 == h + hh, jnp.sum(dga, axis=0, keepdims=True), 0.0))
        return outs + [d_gates, d_ab]

    wspec = lambda seg: ((8, HEAD_DIM), lambda i, hh: (0, seg * h + hh))
    hs = ((tr, HEAD_DIM), lambda i, hh: (i, hh))
    out = ((t, d), F32) + hs + ("set",)
    return _rowcall(
        "gdn_pre_bwd", body, (t // tr, h),
        [(proj_b,) + slab(1), (proj_b,) + halo(1), (proj_b,) + slab(2), (proj_b,) + halo(2),
         (proj_b,) + slab(3), (proj_b,) + halo(3),
         (conv_w8,) + wspec(0), (conv_w8,) + wspec(1), (conv_w8,) + wspec(2),
         (proj_c, (tr, LANES), lambda i, hh: (i, 0)), _full(ab),
         (dq,) + hs, (dk,) + hs, (dv,) + hs, (dbe,) + hs, (dge,) + hs],
        [out, out, out,
         ((t, LANES), F32, (tr, LANES), lambda i, hh: (i, 0), "acc_inner"),
         ((1, LANES), F32, (1, LANES), lambda i, hh: (0, 0), "acc_all")])


def _conv_bwd(proj_b, conv_w8, du_q, du_k, du_v, n_heads):
    t = proj_b.shape[0]
    h = n_heads
    d = h * HEAD_DIM
    tr = _pick(t, TR_HEAD)
    ni = t // tr

    def body(idx, q_c, q_h, k_c, k_h, v_c, v_h, wq, wk, wv, uq, uq_n, uk, uk_n, uv, uv_n):
        hh, i = idx
        live_b = (i > 0).astype(F32)
        live_a = (i < ni - 1).astype(F32)
        d_ins, d_ws = [], []
        for cur, hal, w, du, du_n in ((q_c, q_h, wq, uq, uq_n), (k_c, k_h, wk, uk, uk_n),
                                      (v_c, v_h, wv, uv, uv_n)):
            hal = hal * live_b
            du_n = du_n * live_a
            d_in = du * w[GDN_CONV - 1:GDN_CONV, :]
            rows = [jnp.sum(du * cur, axis=0, keepdims=True)]
            for i_tap in range(GDN_CONV - 2, -1, -1):
                kshift = GDN_CONV - 1 - i_tap
                d_in = d_in + _shift_rows(du, du_n, kshift, False) * w[i_tap:i_tap + 1, :]
                rows.insert(0, jnp.sum(du * _shift_rows(cur, hal, kshift, True), axis=0,
                                       keepdims=True))
            d_ins.append(d_in)
            tap = lax.broadcasted_iota(jnp.int32, (8, HEAD_DIM), 0)
            d_w = jnp.zeros((8, HEAD_DIM), F32)
            for i_tap in range(GDN_CONV):
                d_w = d_w + jnp.where(tap == i_tap, rows[i_tap], 0.0)
            d_ws.append(d_w)
        return d_ins + d_ws

    def slab(seg):
        return ((tr, HEAD_DIM), lambda hh, i: (i, seg * h + hh))

    def halo_b(seg):
        return ((8, HEAD_DIM), lambda hh, i: (jnp.maximum(i * (tr // 8) - 1, 0), seg * h + hh))

    hs = ((tr, HEAD_DIM), lambda hh, i: (i, hh))
    halo_a = ((8, HEAD_DIM), lambda hh, i: (jnp.minimum((i + 1) * (tr // 8), t // 8 - 1), hh))
    wspec = lambda seg: ((8, HEAD_DIM), lambda hh, i: (0, seg * h + hh))
    wout = ((8, d), F32, (8, HEAD_DIM), lambda hh, i: (0, hh), "acc_inner")
    out = ((t, d), BF16) + hs + ("set",)
    res = _rowcall(
        "conv_bwd", body, (h, ni),
        [(proj_b,) + slab(1), (proj_b,) + halo_b(1), (proj_b,) + slab(2), (proj_b,) + halo_b(2),
         (proj_b,) + slab(3), (proj_b,) + halo_b(3),
         (conv_w8,) + wspec(0), (conv_w8,) + wspec(1), (conv_w8,) + wspec(2),
         (du_q,) + hs, (du_q,) + halo_a, (du_k,) + hs, (du_k,) + halo_a, (du_v,) + hs,
         (du_v,) + halo_a],
        [out, out, out, wout, wout, wout])
    return res


def _gdn_post_fwd(o, proj_b, gnw, n_heads):
    t, d = o.shape
    h = n_heads
    tr = _pick(t, TR_HEAD)

    def body(idx, o_v, z, w):
        rstd = lax.rsqrt(jnp.mean(o_v * o_v, axis=1, keepdims=True) + NORM_EPS)
        return ((o_v * rstd) * w * (z * _sigmoid(z)),)

    hs = ((tr, HEAD_DIM), lambda i, hh: (i, hh))
    return _rowcall("gdn_post_fwd", body, (t // tr, h),
                    [(o,) + hs, (proj_b, (tr, HEAD_DIM), lambda i, hh: (i, 4 * h + hh)), _full(gnw)],
                    [((t, d), BF16) + hs + ("set",)])[0]


def _gdn_post_bwd(do_gdn, o, proj_b, gnw, n_heads):
    t, d = o.shape
    h = n_heads
    tr = _pick(t, TR_HEAD)

    def body(idx, dog, o_v, z, w):
        rstd = lax.rsqrt(jnp.mean(o_v * o_v, axis=1, keepdims=True) + NORM_EPS)
        n = o_v * rstd
        sg = _sigmoid(z)
        don = dog * (z * sg)
        dz = dog * (n * w) * (sg * (1.0 + z * (1.0 - sg)))
        dn = don * w
        d_o = rstd * (dn - n * jnp.mean(dn * n, axis=1, keepdims=True))
        return d_o, dz, jnp.sum(don * n, axis=0, keepdims=True)

    hs = ((tr, HEAD_DIM), lambda i, hh: (i, hh))
    return _rowcall("gdn_post_bwd", body, (t // tr, h),
                    [(do_gdn,) + hs, (o,) + hs,
                     (proj_b, (tr, HEAD_DIM), lambda i, hh: (i, 4 * h + hh)), _full(gnw)],
                    [((t, d), F32) + hs + ("set",), ((t, d), BF16) + hs + ("set",),
                     ((1, HEAD_DIM), F32, (1, HEAD_DIM), lambda i, hh: (0, 0), "acc_all")])


def _seg(arr, tr, d, seg):
    return (arr, (tr, d), lambda i: (i, seg))


def _sb_post_fwd(o_raw, proj_b):
    t, d = o_raw.shape
    tr = _pick(t, TR_WIDE)

    def body(idx, o_v, z):
        return (o_v * (z * _sigmoid(z)),)

    return _rowcall("sb_post_fwd", body, (t // tr,), [_seg(o_raw, tr, d, 0), _seg(proj_b, tr, d, 0)],
                    [((t, d), BF16, (tr, d), lambda i: (i, 0), "set")])[0]


def _sb_post_bwd(do_sb, o_raw, proj_b):
    t, d = o_raw.shape
    tr = _pick(t, TR_WIDE)

    def body(idx, dos, o_v, z):
        sg = _sigmoid(z)
        return dos * (z * sg), dos * o_v * (sg * (1.0 + z * (1.0 - sg)))

    out = ((t, d), BF16, (tr, d), lambda i: (i, 0), "set")
    return _rowcall("sb_post_bwd", body, (t // tr,),
                    [_seg(do_sb, tr, d, 0), _seg(o_raw, tr, d, 0), _seg(proj_b, tr, d, 0)],
                    [out, out])


def _merge_fwd(proj_b, p_sb, p_gdn):
    t, d = p_sb.shape
    tr = _pick(t, TR_WIDE)

    def body(idx, m_sb, m_gdn, ps, pg):
        return (_sigmoid(m_sb) * ps + _sigmoid(m_gdn) * pg,)

    return _rowcall("merge_fwd", body, (t // tr,),
                    [_seg(proj_b, tr, d, 5), _seg(proj_b, tr, d, 6), _seg(p_sb, tr, d, 0),
                     _seg(p_gdn, tr, d, 0)],
                    [((t, d), BF16, (tr, d), lambda i: (i, 0), "set")])[0]


def _merge_bwd(dy, proj_b, p_sb, p_gdn):
    t, d = p_sb.shape
    tr = _pick(t, TR_WIDE)

    def body(idx, dy_v, m_sb, m_gdn, ps, pg):
        s1 = _sigmoid(m_sb)
        s2 = _sigmoid(m_gdn)
        return s1 * dy_v, s2 * dy_v, dy_v * ps * (s1 * (1.0 - s1)), dy_v * pg * (s2 * (1.0 - s2))

    out = ((t, d), BF16, (tr, d), lambda i: (i, 0), "set")
    return _rowcall("merge_bwd", body, (t // tr,),
                    [_seg(dy, tr, d, 0), _seg(proj_b, tr, d, 5), _seg(proj_b, tr, d, 6),
                     _seg(p_sb, tr, d, 0), _seg(p_gdn, tr, d, 0)],
                    [out] * 4)


def _tail(x, r, target, gate, final_w):
    t, d = x.shape
    tr = _pick(t, TR_WIDE)

    def body(idx, x_v, r_v, tg, gt, fw):
        x2 = x_v + gt * r_v
        rstd = lax.rsqrt(jnp.mean(x2 * x2, axis=1, keepdims=True) + NORM_EPS)
        n = x2 * rstd
        diff = n * fw - tg
        loss = 0.5 * jnp.sum(jnp.mean(diff * diff, axis=1, keepdims=True), axis=0, keepdims=True)
        dout = diff * (1.0 / d)
        dn = dout * fw
        dx2 = rstd * (dn - n * jnp.mean(dn * n, axis=1, keepdims=True))
        return (dx2, gt * dx2, jnp.sum(dout * n, axis=0, keepdims=True),
                jnp.sum(dx2 * r_v, axis=0, keepdims=True), jnp.broadcast_to(loss, (1, LANES)))

    rb = ((tr, d), lambda i: (i, 0))
    vec = ((1, d), F32, (1, d), lambda i: (0, 0), "acc_all")
    return _rowcall("tail", body, (t // tr,),
                    [(x,) + rb, (r,) + rb, (target,) + rb, _full(gate), _full(final_w)],
                    [((t, d), F32) + rb + ("set",), ((t, d), BF16) + rb + ("set",), vec, vec,
                     ((1, LANES), F32, (1, LANES), lambda i: (0, 0), "acc_all")])


def _pad_to(a, rows, cols):
    return jnp.pad(a, ((0, rows - a.shape[0]), (0, cols - a.shape[1])))


def kernel(x, c, w_ada, b_ada, norm_w, w_in, gdn_conv_w, gdn_a_log, gdn_dt_bias, gdn_norm_w, w_proj_sb, w_proj_gdn, w_out, final_norm_w, loss_target, m_w_ada, m_b_ada, m_norm_w, m_w_in, m_gdn_conv_w, m_gdn_a_log, m_gdn_dt_bias, m_gdn_norm_w, m_w_proj_sb, m_w_proj_gdn, m_w_out, m_final_norm_w, v_w_ada, v_b_ada, v_norm_w, v_w_in, v_gdn_conv_w, v_gdn_a_log, v_gdn_dt_bias, v_gdn_norm_w, v_w_proj_sb, v_w_proj_gdn, v_w_out, v_final_norm_w):
    t, d = x.shape[1], x.shape[2]
    h = d // HEAD_DIM
    me = 4 * lax.axis_index("x") + 2 * lax.axis_index("y") + lax.axis_index("c")
    x2d = x[0]
    tgt = loss_target[0]
    ada_cols = w_ada.shape[2]
    in_cols = w_in.shape[2]
    rows_p = w_out.shape[1]

    w_in_all = _gather_two_level("gather_w_in", w_in[0].astype(BF16))
    w_in_full = jnp.transpose(w_in_all, (1, 0, 2)).reshape(d, N_DEV * in_cols)
    w_main = jnp.concatenate([w_in_full[:, :8 * d], w_in_full[:, 8 * d + 2 * h:]], axis=1)
    w_g = _pad_to(w_in_full[:, 8 * d:8 * d + 2 * h], d, LANES)
    w_main_t = w_main.T
    w_g_t = w_g.T
    w_sq = jnp.stack([w_proj_sb[0], w_proj_gdn[0], w_out[0]]).astype(BF16)
    conv_all = _exchange("gather_conv", _pad_to(gdn_conv_w[0], 8, gdn_conv_w.shape[2]), True)
    conv_w8 = jnp.transpose(conv_all, (1, 0, 2)).reshape(8, 3 * d)
    c_all = _exchange("gather_c", _pad_to(c, 8, d), True)[:, 0, :]

    sc_all = c_all * _sigmoid(c_all)
    mod_part = _matmul("ada_fwd", _pad_to(sc_all, 16, d).astype(BF16), w_ada[0].astype(BF16), F32)
    mod_part = mod_part[:N_DEV] + lax.dynamic_slice(b_ada, (0, me * ada_cols), (1, ada_cols))
    mod_rows = _exchange("a2a_mod", _pad_to(mod_part, 8, ada_cols).reshape(N_DEV, 1, ada_cols)
                         * jnp.ones((1, 8, 1), F32), False)
    mod = mod_rows[:, 0, :].reshape(1, 3 * d)
    shift, scale, gate = mod[:, :d], mod[:, d:2 * d], mod[:, 2 * d:]

    hmod = _norm_mod_fwd(x2d, shift, scale, norm_w)
    proj_a = _matmul("in_proj_a", hmod, w_main, BF16, n_cols=3 * d, col0=0)
    proj_b, w_sq_all = _matmul("in_proj_b", hmod, w_main, F32, n_cols=7 * d, col0=3 * d,
                               xchg=w_sq, gather=True)
    w_sq_full = jnp.transpose(w_sq_all, (1, 0, 2, 3)).reshape(3, d, d)
    wp_sb, wp_gdn, wo = w_sq_full[0], w_sq_full[1], w_sq_full[2]
    proj_c = _matmul("in_proj_c", hmod, w_g, F32)
    o_sb_raw, ctot, sb_visited = _sb_fwd(proj_a, h)
    o_sb = _sb_post_fwd(o_sb_raw, proj_b)
    ab = _pad_to(jnp.concatenate([gdn_a_log, gdn_dt_bias], axis=0), 8, LANES)
    gq, gk, gv, g_beta, g_cum = _gdn_pre_fwd(proj_b, proj_c, conv_w8, ab, h)
    g_cum_t = jnp.broadcast_to(g_cum[:, ::HEAD_DIM].T[:, None, :], (h, 8, t))
    o_gdn_raw, states, tms = _gdn_fwd(gq, gk, gv, g_beta, g_cum, g_cum_t, h)
    o_gdn = _gdn_post_fwd(o_gdn_raw, proj_b, gdn_norm_w, h)
    p_sb = _matmul("proj_sb", o_sb, wp_sb, F32, **SQUARE_TILES)
    p_gdn = _matmul("proj_gdn", o_gdn, wp_gdn, F32, **SQUARE_TILES)
    y = _merge_fwd(proj_b, p_sb, p_gdn)
    r = _matmul("out_proj", y, wo, F32, **SQUARE_TILES)
    dx2, dr, d_final_w, d_gate, loss_part = _tail(x2d, r, tgt, gate, final_norm_w.reshape(1, d))

    dy = _matmul("d_out_proj", dr, wo.T, F32, **SQUARE_TILES)
    dw_out = _matmul("dw_out", y.T, dr, BF16)
    dp_sb, dp_gdn, dm_sb, dm_gdn = _merge_bwd(dy, proj_b, p_sb, p_gdn)
    do_sb = _matmul("d_proj_sb", dp_sb, wp_sb.T, F32, **SQUARE_TILES)
    dw_p_sb = _matmul("dw_proj_sb", o_sb.T, dp_sb, BF16)
    do_gdn = _matmul("d_proj_gdn", dp_gdn, wp_gdn.T, F32, **SQUARE_TILES)
    dw_p_gdn = _matmul("dw_proj_gdn", o_gdn.T, dp_gdn, BF16)
    do_sb_raw, d_sbz = _sb_post_bwd(do_sb, o_sb_raw, proj_b)
    d_sbq, d_sbk, d_sbv = _sb_bwd(proj_a, do_sb_raw, ctot, sb_visited, h)
    d_o_gdn, d_gz, d_gnw = _gdn_post_bwd(do_gdn, o_gdn_raw, proj_b, gdn_norm_w, h)
    dgq, dgk, dgv, dgbe, dgcum = _gdn_bwd(gq, gk, gv, g_beta, g_cum, g_cum_t, d_o_gdn, states, tms, h)
    du_q, du_k, du_v, d_gates, d_ab = _gdn_pre_bwd(proj_b, proj_c, conv_w8, ab, dgq, dgk, dgv,
                                                  dgbe, dgcum, h)
    d_gq, d_gk, d_gv, dcw_q, dcw_k, dcw_v = _conv_bwd(proj_b, conv_w8, du_q, du_k, du_v, h)
    dproj = jnp.concatenate([d_sbq, d_sbk, d_sbv, d_sbz, d_gq, d_gk, d_gv, d_gz, dm_sb, dm_gdn],
                            axis=1)
    d_gates_b = d_gates.astype(BF16)
    hmod_t = hmod.T
    dw_sq = jnp.stack([dw_p_sb, dw_p_gdn, dw_out]).reshape(3, N_DEV, rows_p, d)
    dw_main, dw_sq_parts = _matmul("dw_in", hmod_t, dproj, BF16, tk=BIG_K_TILE,
                                   xchg=jnp.transpose(dw_sq, (1, 0, 2, 3)))
    dw_g = _matmul("dw_in_g", hmod_t, d_gates_b, BF16)
    dw_in_full = jnp.concatenate([dw_main[:, :8 * d], dw_g[:, :2 * h], dw_main[:, 8 * d:]], axis=1)
    dh_a, dw_in_parts = _matmul(
        "d_in_proj", dproj, w_main_t, F32, tk=BIG_K_TILE,
        xchg=jnp.transpose(dw_in_full.reshape(d, N_DEV, in_cols), (1, 0, 2)))
    dh_b = _matmul("d_in_proj_g", d_gates_b, w_g_t, F32)
    grad_x, d_shift, d_scale, d_norm_w = _norm_mod_bwd(dh_a, dh_b, x2d, dx2, scale, norm_w)

    dmod = jnp.concatenate([d_shift, d_scale, d_gate], axis=1)
    small = jnp.concatenate([dmod, d_norm_w, d_final_w, d_ab[:, :h], d_ab[:, h:2 * h], d_gnw,
                             loss_part], axis=1)
    n_small = small.shape[1]
    small_all = _exchange("gather_small", _pad_to(small, 8, n_small), True)[:, 0:1, :]
    small_w = jnp.concatenate([b_ada, norm_w, final_norm_w.reshape(1, d), gdn_a_log, gdn_dt_bias,
                               gdn_norm_w, jnp.zeros((1, LANES), F32)], axis=1)
    small_m = jnp.concatenate([m_b_ada, m_norm_w, m_final_norm_w.reshape(1, d), m_gdn_a_log,
                               m_gdn_dt_bias, m_gdn_norm_w, jnp.zeros((1, LANES), F32)], axis=1)
    small_v = jnp.concatenate([v_b_ada, v_norm_w, v_final_norm_w.reshape(1, d), v_gdn_a_log,
                               v_gdn_dt_bias, v_gdn_norm_w, jnp.ones((1, LANES), F32)], axis=1)
    s_g, s_d, s_m, s_v = _adamw("adamw_small", small_all, small_w, small_m, small_v)
    cuts = [3 * d, 4 * d, 5 * d, 5 * d + h, 5 * d + 2 * h, 5 * d + 2 * h + HEAD_DIM]

    def split_small(a):
        b, nw, fw, al, dtb, gn, _ = jnp.split(a, cuts, axis=1)
        return b, nw, fw.reshape(d), al, dtb, gn

    loss = s_g[0, cuts[-1]]

    dmod_all = small_all[:, 0, :3 * d]
    dmod_mine = lax.dynamic_slice(dmod_all, (0, me * ada_cols), (N_DEV, ada_cols))
    dw_ada = _matmul("dw_ada", _pad_to(sc_all.T, d, LANES).astype(BF16),
                     _pad_to(dmod_mine, LANES, ada_cols).astype(BF16), F32)
    ada = _adamw("adamw_ada", dw_ada[None], w_ada[0], m_w_ada[0], v_w_ada[0])

    win = _adamw("adamw_w_in", dw_in_parts, w_in[0], m_w_in[0], v_w_in[0])
    sq = _adamw("adamw_sq", dw_sq_parts.reshape(N_DEV, 3 * rows_p, d),
                jnp.concatenate([w_proj_sb[0], w_proj_gdn[0], w_out[0]], axis=0),
                jnp.concatenate([m_w_proj_sb[0], m_w_proj_gdn[0], m_w_out[0]], axis=0),
                jnp.concatenate([v_w_proj_sb[0], v_w_proj_gdn[0], v_w_out[0]], axis=0))
    dcw = jnp.concatenate([dcw_q, dcw_k, dcw_v], axis=1)
    cw_cols = gdn_conv_w.shape[2]
    dcw_parts = _exchange("a2a_dconv",
                          jnp.transpose(dcw.reshape(8, N_DEV, cw_cols), (1, 0, 2)), False)
    cw = _adamw("adamw_conv", dcw_parts, _pad_to(gdn_conv_w[0], 8, cw_cols),
                _pad_to(m_gdn_conv_w[0], 8, cw_cols),
                jnp.pad(v_gdn_conv_w[0], ((0, 8 - GDN_CONV), (0, 0)), constant_values=1.0))

    outs = [loss, grad_x[None]]
    for k_out in range(4):
        b, nw, fw, al, dtb, gn = split_small((s_g, s_d, s_m, s_v)[k_out])
        sq3 = sq[k_out].reshape(3, 1, rows_p, d)
        outs += [ada[k_out][None], b, nw, win[k_out][None], cw[k_out][None, :GDN_CONV], al, dtb, gn,
                 sq3[0], sq3[1], sq3[2], fw]
    return tuple(outs)
```
